```python
import math
import jax, jax.numpy as jnp
from jax import lax
import numpy as np

D_MODEL = 1024
BATCH = 16
SEQ = 2048
DEPTH = 2

GRID_W = 64
CTX_LEN = 256
MIX_WIDTH = D_MODEL
DA_WIDTH = MIX_WIDTH // 2
DA_HEADS = 4
DA_VDIM = DA_WIDTH // DA_HEADS
DA_HEAD_DIM = DA_VDIM // 2
SG_WIDTH = MIX_WIDTH - DA_WIDTH
SG_GROUPS = 4
SG_GROUP_DIM = SG_WIDTH // SG_GROUPS
CHUNK = 128
Q_BLOCK = 128
ROPE_BASE = 10000.0
D_FF = 2816
N_EXPERTS = 8
TOP_K = 2
D_EXPERT = 3584
N_DENSE = (DEPTH + 1) // 2
N_MOE = DEPTH // 2
EPS = 1e-6
IN_COLS = 3 * DA_WIDTH + 2 * SG_WIDTH

kernel_name = "hybrid_diffattn_sgu_moe_dit"


def rmsnorm(x, g):
    xf = x.astype(jnp.float32)
    y = xf * lax.rsqrt(jnp.mean(xf * xf, axis=-1, keepdims=True) + EPS)
    return (y * g.astype(jnp.float32)).astype(x.dtype)


def layernorm(x, g, b):
    xf = x.astype(jnp.float32)
    mu = jnp.mean(xf, axis=-1, keepdims=True)
    var = jnp.mean(jnp.square(xf - mu), axis=-1, keepdims=True)
    y = (xf - mu) * lax.rsqrt(var + EPS)
    return (y * g.astype(jnp.float32) + b.astype(jnp.float32)).astype(x.dtype)


def modulate(h, shift, scale):
    return h * (1 + scale) + shift


def axial_rope_tables(n_tokens, dtype):
    rows = n_tokens // GRID_W
    row = jnp.repeat(jnp.arange(rows, dtype=jnp.float32), GRID_W)
    col = jnp.tile(jnp.arange(GRID_W, dtype=jnp.float32), rows)
    half = DA_HEAD_DIM // 2
    inv = ROPE_BASE ** (-jnp.arange(0, half, 2, dtype=jnp.float32) / half)
    ang_r = row[:, None] * inv
    ang_c = col[:, None] * inv
    ang = jnp.concatenate([ang_r, ang_r, ang_c, ang_c], axis=-1)
    return jnp.cos(ang).astype(dtype), jnp.sin(ang).astype(dtype)


def apply_rope(x, cos, sin):
    xs = x.reshape(x.shape[:-1] + (2, 2, DA_HEAD_DIM // 4))
    rot = jnp.concatenate([-xs[..., 1:2, :], xs[..., 0:1, :]], axis=-2).reshape(x.shape)
    return x * cos[:, None, None, :] + rot * sin[:, None, None, :]


def diff_attention(q, k, v, lam):
    b, sq = q.shape[0], q.shape[1]
    nb = sq // Q_BLOCK
    qb = q.reshape(b, nb, Q_BLOCK, DA_HEADS, 2, DA_HEAD_DIM).transpose(1, 0, 2, 3, 4, 5)
    scale = DA_HEAD_DIM ** -0.5

    def block(qi):
        s = jnp.einsum('bqhmd,bkhmd->bhmqk', qi, k).astype(jnp.float32) * scale
        p = jax.nn.softmax(s, axis=-1)
        w = p[:, :, 0] - lam.astype(jnp.float32) * p[:, :, 1]
        return jnp.einsum('bhqk,bkhe->bqhe', w.astype(v.dtype), v)

    out = lax.map(block, qb)
    return out.transpose(1, 0, 2, 3, 4).reshape(b, sq, DA_HEADS, DA_VDIM)


def diff_head_out(o, subln_g, lambda_init):
    o = rmsnorm(o, subln_g) * (1.0 - lambda_init)
    return o.reshape(o.shape[0], o.shape[1], DA_WIDTH)


def spatial_gating(u, gv, ln_g, ln_b, w_s, b_s):
    b, s, _ = u.shape
    vn = layernorm(gv, ln_g, ln_b)
    vc = vn.reshape(b, s // CHUNK, CHUNK, SG_GROUPS, SG_GROUP_DIM)
    mixed = jnp.einsum('gpq,bcqge->bcpge', w_s, vc) + b_s.T[None, None, :, :, None]
    return u * mixed.reshape(b, s, SG_WIDTH)


def swiglu(h, w1, w3, w2):
    return (jax.nn.silu(h @ w1) * (h @ w3)) @ w2


def moe_swiglu(h, router_w, w1, w3, w2):
    shp = h.shape
    t = h.reshape(-1, shp[-1])
    logits = (t @ router_w).astype(jnp.float32)
    top_v, top_i = lax.top_k(logits, TOP_K)
    top_w = jax.nn.softmax(top_v, axis=-1)
    gates = jnp.sum(jax.nn.one_hot(top_i, N_EXPERTS, dtype=jnp.float32) * top_w[..., None], axis=1)
    gates = gates.astype(h.dtype)
    y = jnp.zeros_like(t)
    for e in range(N_EXPERTS):
        y = y + gates[:, e:e + 1] * swiglu(t, w1[e], w3[e], w2[e])
    return y.reshape(shp)


def split_heads_qkv(p):
    q, k, v = p[..., :DA_WIDTH], p[..., DA_WIDTH:2 * DA_WIDTH], p[..., 2 * DA_WIDTH:3 * DA_WIDTH]
    b, s = p.shape[0], p.shape[1]
    q = q.reshape(b, s, DA_HEADS, 2, DA_HEAD_DIM)
    k = k.reshape(b, s, DA_HEADS, 2, DA_HEAD_DIM)
    v = v.reshape(b, s, DA_HEADS, DA_VDIM)
    return q, k, v


def setup_inputs(seed: int = 0) -> dict:
    key = jax.random.key(seed)
    ks = jax.random.split(key, 28)
    f32 = jnp.float32
    nrm = lambda k, shape, s: (jax.random.normal(k, shape, f32) * s)
    gain = lambda k, shape: 1.0 + 0.05 * jax.random.normal(k, shape, f32)
    D = D_MODEL
    return {
        "x": nrm(ks[0], (BATCH, SEQ, D), 1.0),
        "c": nrm(ks[1], (BATCH, D), 1.0),
        "ctx": nrm(ks[2], (BATCH, CTX_LEN, D), 1.0),
        "c_ctx": nrm(ks[3], (D,), 1.0),
        "ada_w": nrm(ks[4], (DEPTH, D, 6 * D), 0.5 * D ** -0.5),
        "ada_b": nrm(ks[5], (DEPTH, 6 * D), 0.02),
        "pre_mix_g": gain(ks[6], (DEPTH, D)),
        "post_mix_g": gain(ks[7], (DEPTH, D)),
        "pre_ffn_g": gain(ks[8], (DEPTH, D)),
        "post_ffn_g": gain(ks[9], (DEPTH, D)),
        "w_in": nrm(ks[10], (DEPTH, D, IN_COLS), D ** -0.5),
        "w_out": nrm(ks[11], (DEPTH, MIX_WIDTH, D), MIX_WIDTH ** -0.5),
        "lam_q1": nrm(ks[12], (DEPTH, DA_HEAD_DIM), 0.1),
        "lam_k1": nrm(ks[13], (DEPTH, DA_HEAD_DIM), 0.1),
        "lam_q2": nrm(ks[14], (DEPTH, DA_HEAD_DIM), 0.1),
        "lam_k2": nrm(ks[15], (DEPTH, DA_HEAD_DIM), 0.1),
        "subln_g": gain(ks[16], (DEPTH, DA_VDIM)),
        "sg_ln_g": gain(ks[17], (DEPTH, SG_WIDTH)),
        "sg_ln_b": nrm(ks[18], (DEPTH, SG_WIDTH), 0.02),
        "sg_w": nrm(ks[19], (DEPTH, SG_GROUPS, CHUNK, CHUNK), CHUNK ** -0.5),
        "sg_b": 1.0 + nrm(ks[20], (DEPTH, SG_GROUPS, CHUNK), 0.02),
        "ffn_w1": nrm(ks[21], (N_DENSE, D, D_FF), D ** -0.5),
        "ffn_w3": nrm(ks[22], (N_DENSE, D, D_FF), D ** -0.5),
        "ffn_w2": nrm(ks[23], (N_DENSE, D_FF, D), D_FF ** -0.5),
        "router_w": nrm(ks[24], (N_MOE, D, N_EXPERTS), D ** -0.5),
        "moe_w1": nrm(ks[25], (N_MOE, N_EXPERTS, D, D_EXPERT), D ** -0.5),
        "moe_w3": nrm(ks[26], (N_MOE, N_EXPERTS, D, D_EXPERT), D ** -0.5),
        "moe_w2": nrm(ks[27], (N_MOE, N_EXPERTS, D_EXPERT, D), D_EXPERT ** -0.5),
    }


def reference(x, c, ctx, c_ctx, ada_w, ada_b, pre_mix_g, post_mix_g, pre_ffn_g, post_ffn_g,
              w_in, w_out, lam_q1, lam_k1, lam_q2, lam_k2, subln_g, sg_ln_g, sg_ln_b, sg_w, sg_b,
              ffn_w1, ffn_w3, ffn_w2, router_w, moe_w1, moe_w3, moe_w2):
    n_lat = x.shape[1]
    cos, sin = axial_rope_tables(n_lat, x.dtype)
    x_lat, x_ctx = x, ctx
    for l in range(DEPTH):
        last = l == DEPTH - 1
        lambda_init = 0.8 - 0.6 * math.exp(-0.3 * l)
        ms = jnp.split(jax.nn.silu(c) @ ada_w[l] + ada_b[l], 6, axis=-1)
        ms = [m[:, None, :] for m in ms]
        mc = jnp.split(jax.nn.silu(c_ctx) @ ada_w[l] + ada_b[l], 6, axis=-1)
        lam = (jnp.exp(jnp.sum(lam_q1[l] * lam_k1[l])) - jnp.exp(jnp.sum(lam_q2[l] * lam_k2[l]))
               + lambda_init)

        h_lat = modulate(rmsnorm(x_lat, pre_mix_g[l]), ms[0], ms[1])
        h_ctx = modulate(rmsnorm(x_ctx, pre_mix_g[l]), mc[0], mc[1])
        p_lat = h_lat @ w_in[l]
        q_l, k_l, v_l = split_heads_qkv(p_lat)
        q_l = apply_rope(q_l, cos, sin)
        k_l = apply_rope(k_l, cos, sin)
        u_l = p_lat[..., 3 * DA_WIDTH:3 * DA_WIDTH + SG_WIDTH]
        g_l = p_lat[..., 3 * DA_WIDTH + SG_WIDTH:]
        if last:
            p_kv = h_ctx @ w_in[l][:, DA_WIDTH:3 * DA_WIDTH]
            bc, sc = p_kv.shape[0], p_kv.shape[1]
            k_c = p_kv[..., :DA_WIDTH].reshape(bc, sc, DA_HEADS, 2, DA_HEAD_DIM)
            v_c = p_kv[..., DA_WIDTH:].reshape(bc, sc, DA_HEADS, DA_VDIM)
        else:
            p_ctx = h_ctx @ w_in[l]
            q_c, k_c, v_c = split_heads_qkv(p_ctx)
        k_all = jnp.concatenate([k_c, k_l], axis=1)
        v_all = jnp.concatenate([v_c, v_l], axis=1)
        a_lat = diff_head_out(diff_attention(q_l, k_all, v_all, lam), subln_g[l], lambda_init)
        s_lat = spatial_gating(u_l, g_l, sg_ln_g[l], sg_ln_b[l], sg_w[l], sg_b[l])
        o_lat = jnp.concatenate([a_lat, s_lat], axis=-1) @ w_out[l]
        x_lat = x_lat + ms[2] * rmsnorm(o_lat, post_mix_g[l])
        if not last:
            u_c = p_ctx[..., 3 * DA_WIDTH:3 * DA_WIDTH + SG_WIDTH]
            g_c = p_ctx[..., 3 * DA_WIDTH + SG_WIDTH:]
            a_ctx = diff_head_out(diff_attention(q_c, k_c, v_c, lam), subln_g[l], lambda_init)
            s_ctx = spatial_gating(u_c, g_c, sg_ln_g[l], sg_ln_b[l], sg_w[l], sg_b[l])
            o_ctx = jnp.concatenate([a_ctx, s_ctx], axis=-1) @ w_out[l]
            x_ctx = x_ctx + mc[2] * rmsnorm(o_ctx, post_mix_g[l])

        f_lat = modulate(rmsnorm(x_lat, pre_ffn_g[l]), ms[3], ms[4])
        if l % 2 == 0:
            i = l // 2
            y_lat = swiglu(f_lat, ffn_w1[i], ffn_w3[i], ffn_w2[i])
        else:
            i = l // 2
            y_lat = moe_swiglu(f_lat, router_w[i], moe_w1[i], moe_w3[i], moe_w2[i])
        x_lat = x_lat + ms[5] * rmsnorm(y_lat, post_ffn_g[l])
        if not last:
            f_ctx = modulate(rmsnorm(x_ctx, pre_ffn_g[l]), mc[3], mc[4])
            i = l // 2
            if l % 2 == 0:
                y_ctx = swiglu(f_ctx, ffn_w1[i], ffn_w3[i], ffn_w2[i])
            else:
                y_ctx = moe_swiglu(f_ctx, router_w[i], moe_w1[i], moe_w3[i], moe_w2[i])
            x_ctx = x_ctx + mc[5] * rmsnorm(y_ctx, post_ffn_g[l])
    return x_lat
```

```python
import functools
import math

import jax
import jax.numpy as jnp
from jax import lax
from jax.experimental import pallas as pl
from jax.experimental.pallas import tpu as pltpu

D_MODEL = 1024
BATCH = 16
SEQ = 2048
DEPTH = 2
GRID_W = 64
CTX_LEN = 256
DA_WIDTH = 512
DA_HEADS = 4
DA_VDIM = 128
DA_HEAD_DIM = 64
SG_WIDTH = 512
SG_GROUPS = 4
CHUNK = 128
ROPE_BASE = 10000.0
D_FF = 2816
N_EXPERTS = 8
D_EXPERT = 3584
EPS = 1e-6
IN_COLS = 3 * DA_WIDTH + 2 * SG_WIDTH

N_LAT = BATCH * SEQ
N_CTX = BATCH * CTX_LEN
N_ALL = N_LAT + N_CTX
MOD_ROWS = 32

TM = 512
TQ = 512
TF_DENSE = 1408
TF_MOE = 512
MIB = 2 ** 20

BF16 = jnp.bfloat16
F32 = jnp.float32


def _rms(x):
    return x * lax.rsqrt(jnp.mean(x * x, axis=-1, keepdims=True) + EPS)


def _silu(x):
    return x * jax.nn.sigmoid(x)


def _resident(block_shape, index_map):
    return pl.BlockSpec(block_shape, index_map, pipeline_mode=pl.Buffered(1))


def _params(semantics, vmem_mib):
    return pltpu.CompilerParams(dimension_semantics=semantics, vmem_limit_bytes=vmem_mib * MIB)


def _mod_kernel(cc_ref, w_ref, b_ref, o_ref):
    a = _silu(cc_ref[...]).astype(BF16)
    o_ref[...] = jnp.dot(a, w_ref[...].astype(BF16), preferred_element_type=F32) + b_ref[...]


def _modulation(cc, ada_w, ada_b):
    tn = 1536
    out = pl.pallas_call(
        _mod_kernel,
        grid=(DEPTH, 6 * D_MODEL // tn),
        in_specs=[
            pl.BlockSpec((MOD_ROWS, D_MODEL), lambda l, j: (0, 0)),
            pl.BlockSpec((None, D_MODEL, tn), lambda l, j: (l, 0, j)),
            pl.BlockSpec((None, 1, tn), lambda l, j: (l, 0, j)),
        ],
        out_specs=pl.BlockSpec((None, MOD_ROWS, tn), lambda l, j: (l, 0, j)),
        out_shape=jax.ShapeDtypeStruct((DEPTH, MOD_ROWS, 6 * D_MODEL), F32),
        compiler_params=_params(("parallel", "parallel"), 40),
        name="modulation",
    )(cc, ada_w, ada_b.reshape(DEPTH, 1, 6 * D_MODEL))
    return out.reshape(DEPTH, MOD_ROWS, 6, D_MODEL)


def _mod_spec(layer):
    tiles_per_batch = SEQ // TM
    return pl.BlockSpec((None, None, 6, D_MODEL),
                        lambda i: (layer, jnp.minimum(i // tiles_per_batch, BATCH), 0, 0))


def _in_proj_kernel(n_lat_tiles, xa_ref, xb_ref, mod_ref, g_ref, w_ref, cos_ref, sa_ref, sb_ref,
                    lng_ref, lnb_ref, q_ref, kt_ref, v_ref, u_ref, gn_ref):
    i = pl.program_id(0)
    x = jnp.where(i < n_lat_tiles, xa_ref[...], xb_ref[...])
    h = _rms(x) * g_ref[...]
    h = h * (1.0 + mod_ref[1:2, :]) + mod_ref[0:1, :]
    p = jnp.dot(h.astype(BF16), w_ref[...], preferred_element_type=F32)

    cos, sa, sb = cos_ref[...], sa_ref[...], sb_ref[...]

    def rope(t):
        return t * cos + pltpu.roll(t, 112, 1) * sa + pltpu.roll(t, 16, 1) * sb

    for hd in range(DA_HEADS):
        lo, hi = hd * DA_VDIM, (hd + 1) * DA_VDIM
        q_ref[:, lo:hi] = (rope(p[:, lo:hi]) * (DA_HEAD_DIM ** -0.5)).astype(BF16)
        kt_ref[lo:hi, :] = rope(p[:, DA_WIDTH + lo:DA_WIDTH + hi]).T.astype(BF16)
    v_ref[...] = p[:, 2 * DA_WIDTH:3 * DA_WIDTH].astype(BF16)
    u_ref[...] = p[:, 3 * DA_WIDTH:3 * DA_WIDTH + SG_WIDTH].astype(BF16)
    gv = p[:, 3 * DA_WIDTH + SG_WIDTH:]
    mu = jnp.mean(gv, axis=-1, keepdims=True)
    var = jnp.mean(jnp.square(gv - mu), axis=-1, keepdims=True)
    gn_ref[...] = ((gv - mu) * lax.rsqrt(var + EPS) * lng_ref[...] + lnb_ref[...]).astype(BF16)


def _in_proj(layer, xa, xb, xb_tile0, mod, pre_g, w_in, rope_tabs, ln_g, ln_b):
    n_lat_tiles = N_LAT // TM
    n_tiles = N_ALL // TM
    tiles_per_batch = SEQ // TM
    tab_spec = pl.BlockSpec(
        (TM, DA_VDIM), lambda i: (jnp.where(i < n_lat_tiles, i % tiles_per_batch, tiles_per_batch), 0))
    row = lambda i: (i, 0)
    outs = pl.pallas_call(
        functools.partial(_in_proj_kernel, n_lat_tiles),
        grid=(n_tiles,),
        in_specs=[
            pl.BlockSpec((TM, D_MODEL), lambda i: (jnp.minimum(i, n_lat_tiles - 1), 0)),
            pl.BlockSpec((TM, D_MODEL), lambda i: (xb_tile0 + jnp.maximum(i - n_lat_tiles, 0), 0)),
            _mod_spec(layer),
            _resident((1, D_MODEL), lambda i: (0, 0)),
            _resident((D_MODEL, IN_COLS), lambda i: (0, 0)),
            tab_spec, tab_spec, tab_spec,
            _resident((1, SG_WIDTH), lambda i: (0, 0)),
            _resident((1, SG_WIDTH), lambda i: (0, 0)),
        ],
        out_specs=[
            pl.BlockSpec((TM, DA_WIDTH), row),
            pl.BlockSpec((DA_WIDTH, TM), lambda i: (0, i)),
            pl.BlockSpec((TM, DA_WIDTH), row),
            pl.BlockSpec((TM, SG_WIDTH), row),
            pl.BlockSpec((TM, SG_WIDTH), row),
        ],
        out_shape=[
            jax.ShapeDtypeStruct((N_ALL, DA_WIDTH), BF16),
            jax.ShapeDtypeStruct((DA_WIDTH, N_ALL), BF16),
            jax.ShapeDtypeStruct((N_ALL, DA_WIDTH), BF16),
            jax.ShapeDtypeStruct((N_ALL, SG_WIDTH), BF16),
            jax.ShapeDtypeStruct((N_ALL, SG_WIDTH), BF16),
        ],
        compiler_params=_params(("parallel",), 48),
        name=f"in_proj_l{layer}",
    )(xa, xb, mod, pre_g, w_in, *rope_tabs, ln_g, ln_b)
    return outs


def _rope_tables():
    pos = jnp.arange(SEQ, dtype=jnp.int32)
    row = (pos // GRID_W).astype(F32)
    col = (pos % GRID_W).astype(F32)
    half = DA_HEAD_DIM // 2
    inv = ROPE_BASE ** (-jnp.arange(0, half, 2, dtype=F32) / half)
    ang_r = row[:, None] * inv
    ang_c = col[:, None] * inv
    ang = jnp.concatenate([ang_r, ang_r, ang_c, ang_c], axis=-1)
    cos = jnp.tile(jnp.cos(ang), (1, 2))
    sin = jnp.tile(jnp.sin(ang), (1, 2))
    first = (jnp.arange(DA_VDIM) % 32) < 16
    sa = jnp.where(first, -sin, 0.0)
    sb = jnp.where(first, 0.0, sin)
    ident = jnp.zeros((TM, DA_VDIM), F32)
    return (jnp.concatenate([cos, ident + 1.0], axis=0),
            jnp.concatenate([sa, ident], axis=0),
            jnp.concatenate([sb, ident], axis=0))


def _attn_kernel(n_src, lambda_init, q_ref, *refs):
    kt_refs = refs[:n_src]
    v_refs = refs[n_src:2 * n_src]
    lq1_ref, lk1_ref, lq2_ref, lk2_ref, subg_ref = refs[2 * n_src:2 * n_src + 5]
    o_ref = refs[-1 - n_src]
    vext_refs = refs[-n_src:]

    @pl.when(pl.program_id(2) == 0)
    def _():
        for v_ref, ve_ref in zip(v_refs, vext_refs):
            ve_ref[:, :DA_VDIM] = v_ref[...]
            lane = lax.broadcasted_iota(jnp.int32, (v_ref.shape[0], DA_VDIM), 1)
            ve_ref[:, DA_VDIM:] = jnp.where(lane == 0, 1.0, 0.0).astype(BF16)

    q = q_ref[...]
    first = lax.broadcasted_iota(jnp.int32, q.shape, 1) < DA_HEAD_DIM
    zero = jnp.zeros_like(q)
    maps = []
    for qm in (jnp.where(first, q, zero), jnp.where(first, zero, q)):
        ss = [jnp.dot(qm, kt_ref[...], preferred_element_type=F32) for kt_ref in kt_refs]
        m = functools.reduce(jnp.maximum, [jnp.max(s, axis=-1, keepdims=True) for s in ss])
        acc = None
        for s, ve_ref in zip(ss, vext_refs):
            t = jnp.dot(jnp.exp(s - m).astype(BF16), ve_ref[...], preferred_element_type=F32)
            acc = t if acc is None else acc + t
        maps.append(acc[:, :DA_VDIM] / acc[:, DA_VDIM:DA_VDIM + 1])
    lam = (jnp.exp(jnp.sum(lq1_ref[...] * lk1_ref[...], axis=-1, keepdims=True))
           - jnp.exp(jnp.sum(lq2_ref[...] * lk2_ref[...], axis=-1, keepdims=True)) + lambda_init)
    o = maps[0] - lam * maps[1]
    o_ref[...] = (_rms(o) * subg_ref[...] * (1.0 - lambda_init)).astype(BF16)


def _attention(layer, q, kt, v, lam_params, subg, a_prev=None):
    lambda_init = 0.8 - 0.6 * math.exp(-0.3 * layer)
    ctx_blk0 = N_LAT // CTX_LEN
    small = lambda: pl.BlockSpec((1, DA_HEAD_DIM), lambda b, h, t: (0, 0))
    tail_specs = [small(), small(), small(), small(), pl.BlockSpec((1, DA_VDIM), lambda b, h, t: (0, 0))]
    kt_ctx = pl.BlockSpec((DA_VDIM, CTX_LEN), lambda b, h, t: (h, ctx_blk0 + b))
    v_ctx = pl.BlockSpec((CTX_LEN, DA_VDIM), lambda b, h, t: (ctx_blk0 + b, h))
    if a_prev is None:
        tq, n_q = TQ, SEQ // TQ
        q_map = lambda b, h, t: (b * n_q + t, h)
        kt_specs = [kt_ctx, pl.BlockSpec((DA_VDIM, SEQ), lambda b, h, t: (h, b))]
        v_specs = [v_ctx, pl.BlockSpec((SEQ, DA_VDIM), lambda b, h, t: (b, h))]
        key_lens = [CTX_LEN, SEQ]
        extra_in, extra_specs, aliases = [], [], {}
    else:
        tq, n_q = CTX_LEN, 1
        q_map = lambda b, h, t: (ctx_blk0 + b, h)
        kt_specs, v_specs, key_lens = [kt_ctx], [v_ctx], [CTX_LEN]
        extra_in, extra_specs = [a_prev], [pl.BlockSpec(memory_space=pl.ANY)]
        aliases = {1 + 2 * len(key_lens) + 5: 0}
    n_src = len(key_lens)
    kernel_fn = functools.partial(_attn_kernel, n_src, lambda_init)
    if a_prev is not None:
        kernel_fn = _drop_arg(kernel_fn, 1 + 2 * n_src + 5)
    return pl.pallas_call(
        kernel_fn,
        grid=(BATCH, DA_HEADS, n_q),
        in_specs=[pl.BlockSpec((tq, DA_VDIM), q_map)] + kt_specs + v_specs + tail_specs + extra_specs,
        out_specs=pl.BlockSpec((tq, DA_VDIM), q_map),
        out_shape=jax.ShapeDtypeStruct((N_ALL, DA_WIDTH), BF16),
        scratch_shapes=[pltpu.VMEM((n, 2 * DA_VDIM), BF16) for n in key_lens],
        input_output_aliases=aliases,
        compiler_params=_params(("parallel", "parallel", "arbitrary"), 48),
        name=f"attn_l{layer}_{'lat' if a_prev is None else 'ctx'}",
    )(q, *([kt] * n_src), *([v] * n_src), *lam_params, subg, *extra_in)


def _drop_arg(fn, pos):
    def wrapped(*refs):
        return fn(*refs[:pos], *refs[pos + 1:])
    return wrapped


def _out_proj_kernel(n_lat_tiles, a_ref, u_ref, gn_ref, sgw_ref, sgb_ref, w_ref, pg_ref, mod_ref,
                     xa_ref, xb_ref, o_ref, s_ref):
    i = pl.program_id(0)
    for c in range(TM // CHUNK):
        rows = slice(c * CHUNK, (c + 1) * CHUNK)
        for g in range(SG_GROUPS):
            cols = slice(g * CHUNK, (g + 1) * CHUNK)
            mixed = jnp.dot(sgw_ref[g], gn_ref[rows, cols], preferred_element_type=F32) + sgb_ref[:, cols]
            s_ref[rows, cols] = (u_ref[rows, cols].astype(F32) * mixed).astype(BF16)
    o = (jnp.dot(a_ref[...], w_ref[:DA_WIDTH, :], preferred_element_type=F32)
         + jnp.dot(s_ref[...], w_ref[DA_WIDTH:, :], preferred_element_type=F32))
    x = jnp.where(i < n_lat_tiles, xa_ref[...], xb_ref[...])
    o_ref[...] = x + mod_ref[2:3, :] * (_rms(o) * pg_ref[...])


def _out_proj(layer, n_tiles, a, u, gn, sg_w, sg_bias, w_out, post_g, mod, xa, xb, xb_tile0):
    n_lat_tiles = N_LAT // TM
    row = lambda i: (i, 0)
    half = lambda: pl.BlockSpec((TM, DA_WIDTH), row)
    return pl.pallas_call(
        functools.partial(_out_proj_kernel, n_lat_tiles),
        grid=(n_tiles,),
        in_specs=[
            half(), half(), half(),
            _resident((SG_GROUPS, CHUNK, CHUNK), lambda i: (0, 0, 0)),
            _resident((CHUNK, SG_WIDTH), lambda i: (0, 0)),
            _resident((D_MODEL, D_MODEL), lambda i: (0, 0)),
            _resident((1, D_MODEL), lambda i: (0, 0)),
            _mod_spec(layer),
            pl.BlockSpec((TM, D_MODEL), lambda i: (jnp.minimum(i, n_lat_tiles - 1), 0)),
            pl.BlockSpec((TM, D_MODEL), lambda i: (xb_tile0 + jnp.maximum(i - n_lat_tiles, 0), 0)),
        ],
        out_specs=pl.BlockSpec((TM, D_MODEL), row),
        out_shape=jax.ShapeDtypeStruct((n_tiles * TM, D_MODEL), F32),
        scratch_shapes=[pltpu.VMEM((TM, SG_WIDTH), BF16)],
        compiler_params=_params(("parallel",), 48),
        name=f"out_proj_l{layer}",
    )(a, u, gn, sg_w, sg_bias, w_out, post_g, mod, xa, xb)


def _ffn_kernel(x_ref, mod_ref, g_ref, w1_ref, w3_ref, w2_ref, pg_ref, o_ref):
    x = x_ref[...]
    f = ((_rms(x) * g_ref[...]) * (1.0 + mod_ref[4:5, :]) + mod_ref[3:4, :]).astype(BF16)
    y = None
    for k in range(D_FF // TF_DENSE):
        cols = slice(k * TF_DENSE, (k + 1) * TF_DENSE)
        h1 = jnp.dot(f, w1_ref[:, cols], preferred_element_type=F32)
        h3 = jnp.dot(f, w3_ref[:, cols], preferred_element_type=F32)
        t = jnp.dot((_silu(h1) * h3).astype(BF16), w2_ref[cols, :], preferred_element_type=F32)
        y = t if y is None else y + t
    o_ref[...] = x + mod_ref[5:6, :] * (_rms(y) * pg_ref[...])


def _ffn(layer, x_all, mod, pre_g, w1, w3, w2, post_g):
    row = lambda i: (i, 0)
    return pl.pallas_call(
        _ffn_kernel,
        grid=(N_ALL // TM,),
        in_specs=[
            pl.BlockSpec((TM, D_MODEL), row),
            _mod_spec(layer),
            _resident((1, D_MODEL), lambda i: (0, 0)),
            _resident((D_MODEL, D_FF), lambda i: (0, 0)),
            _resident((D_MODEL, D_FF), lambda i: (0, 0)),
            _resident((D_FF, D_MODEL), lambda i: (0, 0)),
            _resident((1, D_MODEL), lambda i: (0, 0)),
        ],
        out_specs=pl.BlockSpec((TM, D_MODEL), row),
        out_shape=jax.ShapeDtypeStruct((N_ALL, D_MODEL), F32),
        compiler_params=_params(("parallel",), 56),
        name=f"ffn_l{layer}",
    )(x_all, mod, pre_g, w1, w3, w2, post_g)


def _router_kernel(x_ref, mod_ref, g_ref, rw_ref, f_ref, gates_ref):
    x = x_ref[...]
    f = (_rms(x) * g_ref[...]) * (1.0 + mod_ref[4:5, :]) + mod_ref[3:4, :]
    f_ref[...] = f.astype(BF16)
    logits = jnp.dot(f, rw_ref[...], preferred_element_type=F32, precision=lax.Precision.HIGHEST)
    lane = lax.broadcasted_iota(jnp.int32, logits.shape, 1)
    v1 = jnp.max(logits, axis=-1, keepdims=True)
    i1 = jnp.min(jnp.where(logits == v1, lane, N_EXPERTS), axis=-1, keepdims=True)
    rest = jnp.where(lane == i1, -jnp.inf, logits)
    v2 = jnp.max(rest, axis=-1, keepdims=True)
    i2 = jnp.min(jnp.where(rest == v2, lane, N_EXPERTS), axis=-1, keepdims=True)
    e = jnp.exp(v2 - v1)
    w1 = 1.0 / (1.0 + e)
    gates_ref[...] = jnp.where(lane == i1, w1, 0.0) + jnp.where(lane == i2, e * w1, 0.0)


def _router(layer, x_lat, mod, pre_g, router_w):
    row = lambda i: (i, 0)
    return pl.pallas_call(
        _router_kernel,
        grid=(N_LAT // TM,),
        in_specs=[
            pl.BlockSpec((TM, D_MODEL), row),
            _mod_spec(layer),
            _resident((1, D_MODEL), lambda i: (0, 0)),
            _resident((D_MODEL, N_EXPERTS), lambda i: (0, 0)),
        ],
        out_specs=[pl.BlockSpec((TM, D_MODEL), row), pl.BlockSpec((TM, N_EXPERTS), row)],
        out_shape=[jax.ShapeDtypeStruct((N_LAT, D_MODEL), BF16),
                   jax.ShapeDtypeStruct((N_LAT, N_EXPERTS), F32)],
        compiler_params=_params(("parallel",), 32),
        name=f"router_l{layer}",
    )(x_lat, mod, pre_g, router_w)


def _moe_kernel(f_ref, gates_ref, w1_ref, w3_ref, w2_ref, x_ref, mod_ref, pg_ref, o_ref, acc_ref):
    e = pl.program_id(1)
    k = pl.program_id(2)

    @pl.when((e == 0) & (k == 0))
    def _():
        acc_ref[...] = jnp.zeros_like(acc_ref)

    f = f_ref[...]
    h1 = jnp.dot(f, w1_ref[...], preferred_element_type=F32)
    h3 = jnp.dot(f, w3_ref[...], preferred_element_type=F32)
    lane = lax.broadcasted_iota(jnp.int32, gates_ref.shape, 1)
    gate = jnp.sum(jnp.where(lane == e, gates_ref[...], 0.0), axis=-1, keepdims=True)
    acc_ref[...] += jnp.dot((_silu(h1) * h3 * gate).astype(BF16), w2_ref[...], preferred_element_type=F32)

    @pl.when((e == N_EXPERTS - 1) & (k == pl.num_programs(2) - 1))
    def _():
        o_ref[...] = x_ref[...] + mod_ref[5:6, :] * (_rms(acc_ref[...]) * pg_ref[...])


def _moe(layer, f, gates, w1, w3, w2, x_lat, mod, post_g):
    row = lambda i, e, k: (i, 0)
    tiles_per_batch = SEQ // TM
    return pl.pallas_call(
        _moe_kernel,
        grid=(N_LAT // TM, N_EXPERTS, D_EXPERT // TF_MOE),
        in_specs=[
            pl.BlockSpec((TM, D_MODEL), row),
            pl.BlockSpec((TM, N_EXPERTS), row),
            pl.BlockSpec((None, D_MODEL, TF_MOE), lambda i, e, k: (e, 0, k)),
            pl.BlockSpec((None, D_MODEL, TF_MOE), lambda i, e, k: (e, 0, k)),
            pl.BlockSpec((None, TF_MOE, D_MODEL), lambda i, e, k: (e, k, 0)),
            pl.BlockSpec((TM, D_MODEL), row),
            pl.BlockSpec((None, None, 6, D_MODEL), lambda i, e, k: (layer, i // tiles_per_batch, 0, 0)),
            pl.BlockSpec((1, D_MODEL), lambda i, e, k: (0, 0)),
        ],
        out_specs=pl.BlockSpec((TM, D_MODEL), row),
        out_shape=jax.ShapeDtypeStruct((N_LAT, D_MODEL), F32),
        scratch_shapes=[pltpu.VMEM((TM, D_MODEL), F32)],
        compiler_params=_params(("parallel", "arbitrary", "arbitrary"), 48),
        name=f"moe_l{layer}",
    )(f, gates, w1, w3, w2, x_lat, mod, post_g)


def kernel(x, c, ctx, c_ctx, ada_w, ada_b, pre_mix_g, post_mix_g, pre_ffn_g, post_ffn_g, w_in, w_out,
           lam_q1, lam_k1, lam_q2, lam_k2, subln_g, sg_ln_g, sg_ln_b, sg_w, sg_b,
           ffn_w1, ffn_w3, ffn_w2, router_w, moe_w1, moe_w3, moe_w2):
    x_lat = x.reshape(N_LAT, D_MODEL)
    x_ctx = ctx.reshape(N_CTX, D_MODEL)
    cc = jnp.concatenate([c, c_ctx[None, :], jnp.zeros((MOD_ROWS - BATCH - 1, D_MODEL), F32)], axis=0)
    mod = _modulation(cc, ada_w, ada_b)
    rope_tabs = _rope_tables()
    n_lat_tiles = N_LAT // TM
    x_all = None
    for l in range(DEPTH):
        last = l == DEPTH - 1
        vec = lambda a: a[l][None, :]
        if l == 0:
            xa, xb, xb_tile0 = x_lat, x_ctx, 0
        else:
            xa, xb, xb_tile0 = x_all, x_all, n_lat_tiles
        q, kt, v, u, gn = _in_proj(l, xa, xb, xb_tile0, mod, vec(pre_mix_g), w_in[l].astype(BF16),
                                   rope_tabs, vec(sg_ln_g), vec(sg_ln_b))
        lam_params = [vec(lam_q1), vec(lam_k1), vec(lam_q2), vec(lam_k2)]
        a = _attention(l, q, kt, v, lam_params, vec(subln_g))
        if not last:
            a = _attention(l, q, kt, v, lam_params, vec(subln_g), a_prev=a)
        sg_bias = jnp.repeat(sg_b[l].T, CHUNK, axis=1)
        n_tiles = n_lat_tiles if last else N_ALL // TM
        x_mid = _out_proj(l, n_tiles, a, u, gn, sg_w[l].astype(BF16), sg_bias, w_out[l].astype(BF16),
                          vec(post_mix_g), mod, xa, xb, xb_tile0)
        if l % 2 == 0:
            i = l // 2
            x_all = _ffn(l, x_mid, mod, vec(pre_ffn_g), ffn_w1[i].astype(BF16), ffn_w3[i].astype(BF16),
                         ffn_w2[i].astype(BF16), vec(post_ffn_g))
        else:
            i = l // 2
            f, gates = _router(l, x_mid, mod, vec(pre_ffn_g), router_w[i])
            x_all = _moe(l, f, gates, moe_w1[i].astype(BF16), moe_w3[i].astype(BF16),
                         moe_w2[i].astype(BF16), x_mid, mod, vec(post_ffn_g))
    return x_all.reshape(BATCH, SEQ, D_MODEL)
```

```python
import functools
import math

import jax
import jax.numpy as jnp
from jax import lax
from jax.experimental import pallas as pl
from jax.experimental.pallas import tpu as pltpu

D_MODEL = 1024
BATCH = 16
SEQ = 2048
DEPTH = 2
GRID_W = 64
CTX_LEN = 256
DA_WIDTH = 512
DA_HEADS = 4
DA_VDIM = 128
DA_HEAD_DIM = 64
SG_WIDTH = 512
SG_GROUPS = 4
CHUNK = 128
ROPE_BASE = 10000.0
D_FF = 2816
N_EXPERTS = 8
D_EXPERT = 3584
EPS = 1e-6
IN_COLS = 3 * DA_WIDTH + 2 * SG_WIDTH

N_LAT = BATCH * SEQ
N_CTX = BATCH * CTX_LEN
N_ALL = N_LAT + N_CTX
MOD_ROWS = 32

TM = 512
TQ = 512
TF_DENSE = 1408
TF_MOE = 512
XS_TILES = 2 * N_LAT // TM + N_EXPERTS
MIB = 2 ** 20

BF16 = jnp.bfloat16
F32 = jnp.float32


def _rms(x):
    return x * lax.rsqrt(jnp.mean(x * x, axis=-1, keepdims=True) + EPS)


def _silu(x):
    return x * jax.nn.sigmoid(x)


def _resident(block_shape, index_map):
    return pl.BlockSpec(block_shape, index_map, pipeline_mode=pl.Buffered(1))


def _params(semantics, vmem_mib):
    return pltpu.CompilerParams(dimension_semantics=semantics, vmem_limit_bytes=vmem_mib * MIB)


def _mod_kernel(cc_ref, w_ref, b_ref, o_ref):
    a = _silu(cc_ref[...]).astype(BF16)
    o_ref[...] = jnp.dot(a, w_ref[...].astype(BF16), preferred_element_type=F32) + b_ref[...]


def _modulation(cc, ada_w, ada_b):
    tn = 1536
    out = pl.pallas_call(
        _mod_kernel,
        grid=(DEPTH, 6 * D_MODEL // tn),
        in_specs=[
            pl.BlockSpec((MOD_ROWS, D_MODEL), lambda l, j: (0, 0)),
            pl.BlockSpec((None, D_MODEL, tn), lambda l, j: (l, 0, j)),
            pl.BlockSpec((None, 1, tn), lambda l, j: (l, 0, j)),
        ],
        out_specs=pl.BlockSpec((None, MOD_ROWS, tn), lambda l, j: (l, 0, j)),
        out_shape=jax.ShapeDtypeStruct((DEPTH, MOD_ROWS, 6 * D_MODEL), F32),
        compiler_params=_params(("parallel", "parallel"), 40),
        name="modulation",
    )(cc, ada_w, ada_b.reshape(DEPTH, 1, 6 * D_MODEL))
    return out.reshape(DEPTH, MOD_ROWS, 6, D_MODEL)


def _mod_spec(layer):
    tiles_per_batch = SEQ // TM
    return pl.BlockSpec((None, None, 6, D_MODEL),
                        lambda i: (layer, jnp.minimum(i // tiles_per_batch, BATCH), 0, 0))


def _in_proj_kernel(n_lat_tiles, xa_ref, xb_ref, mod_ref, g_ref, w_ref, cos_ref, sa_ref, sb_ref,
                    lng_ref, lnb_ref, q_ref, kt_ref, v_ref, u_ref, gn_ref):
    i = pl.program_id(0)
    x = jnp.where(i < n_lat_tiles, xa_ref[...], xb_ref[...])
    h = _rms(x) * g_ref[...]
    h = h * (1.0 + mod_ref[1:2, :]) + mod_ref[0:1, :]
    p = jnp.dot(h.astype(BF16), w_ref[...], preferred_element_type=F32)

    cos, sa, sb = cos_ref[...], sa_ref[...], sb_ref[...]

    def rope(t):
        return t * cos + pltpu.roll(t, 112, 1) * sa + pltpu.roll(t, 16, 1) * sb

    for hd in range(DA_HEADS):
        lo, hi = hd * DA_VDIM, (hd + 1) * DA_VDIM
        q_ref[:, lo:hi] = (rope(p[:, lo:hi]) * (DA_HEAD_DIM ** -0.5)).astype(BF16)
        kt_ref[lo:hi, :] = rope(p[:, DA_WIDTH + lo:DA_WIDTH + hi]).T.astype(BF16)
    v_ref[...] = p[:, 2 * DA_WIDTH:3 * DA_WIDTH].astype(BF16)
    u_ref[...] = p[:, 3 * DA_WIDTH:3 * DA_WIDTH + SG_WIDTH].astype(BF16)
    gv = p[:, 3 * DA_WIDTH + SG_WIDTH:]
    mu = jnp.mean(gv, axis=-1, keepdims=True)
    var = jnp.mean(jnp.square(gv - mu), axis=-1, keepdims=True)
    gn_ref[...] = ((gv - mu) * lax.rsqrt(var + EPS) * lng_ref[...] + lnb_ref[...]).astype(BF16)


def _in_proj(layer, xa, xb, xb_tile0, mod, pre_g, w_in, rope_tabs, ln_g, ln_b):
    n_lat_tiles = N_LAT // TM
    n_tiles = N_ALL // TM
    tiles_per_batch = SEQ // TM
    tab_spec = pl.BlockSpec(
        (TM, DA_VDIM), lambda i: (jnp.where(i < n_lat_tiles, i % tiles_per_batch, tiles_per_batch), 0))
    row = lambda i: (i, 0)
    outs = pl.pallas_call(
        functools.partial(_in_proj_kernel, n_lat_tiles),
        grid=(n_tiles,),
        in_specs=[
            pl.BlockSpec((TM, D_MODEL), lambda i: (jnp.minimum(i, n_lat_tiles - 1), 0)),
            pl.BlockSpec((TM, D_MODEL), lambda i: (xb_tile0 + jnp.maximum(i - n_lat_tiles, 0), 0)),
            _mod_spec(layer),
            _resident((1, D_MODEL), lambda i: (0, 0)),
            _resident((D_MODEL, IN_COLS), lambda i: (0, 0)),
            tab_spec, tab_spec, tab_spec,
            _resident((1, SG_WIDTH), lambda i: (0, 0)),
            _resident((1, SG_WIDTH), lambda i: (0, 0)),
        ],
        out_specs=[
            pl.BlockSpec((TM, DA_WIDTH), row),
            pl.BlockSpec((DA_WIDTH, TM), lambda i: (0, i)),
            pl.BlockSpec((TM, DA_WIDTH), row),
            pl.BlockSpec((TM, SG_WIDTH), row),
            pl.BlockSpec((TM, SG_WIDTH), row),
        ],
        out_shape=[
            jax.ShapeDtypeStruct((N_ALL, DA_WIDTH), BF16),
            jax.ShapeDtypeStruct((DA_WIDTH, N_ALL), BF16),
            jax.ShapeDtypeStruct((N_ALL, DA_WIDTH), BF16),
            jax.ShapeDtypeStruct((N_ALL, SG_WIDTH), BF16),
            jax.ShapeDtypeStruct((N_ALL, SG_WIDTH), BF16),
        ],
        compiler_params=_params(("parallel",), 48),
        name=f"in_proj_l{layer}",
    )(xa, xb, mod, pre_g, w_in, *rope_tabs, ln_g, ln_b)
    return outs


def _rope_tables():
    pos = jnp.arange(SEQ, dtype=jnp.int32)
    row = (pos // GRID_W).astype(F32)
    col = (pos % GRID_W).astype(F32)
    half = DA_HEAD_DIM // 2
    inv = ROPE_BASE ** (-jnp.arange(0, half, 2, dtype=F32) / half)
    ang_r = row[:, None] * inv
    ang_c = col[:, None] * inv
    ang = jnp.concatenate([ang_r, ang_r, ang_c, ang_c], axis=-1)
    cos = jnp.tile(jnp.cos(ang), (1, 2))
    sin = jnp.tile(jnp.sin(ang), (1, 2))
    first = (jnp.arange(DA_VDIM) % 32) < 16
    sa = jnp.where(first, -sin, 0.0)
    sb = jnp.where(first, 0.0, sin)
    ident = jnp.zeros((TM, DA_VDIM), F32)
    return (jnp.concatenate([cos, ident + 1.0], axis=0),
            jnp.concatenate([sa, ident], axis=0),
            jnp.concatenate([sb, ident], axis=0))


def _attn_kernel(n_src, lambda_init, q_ref, *refs):
    kt_refs = refs[:n_src]
    v_refs = refs[n_src:2 * n_src]
    lq1_ref, lk1_ref, lq2_ref, lk2_ref, subg_ref = refs[2 * n_src:2 * n_src + 5]
    o_ref = refs[-1 - n_src]
    vext_refs = refs[-n_src:]

    @pl.when(pl.program_id(2) == 0)
    def _():
        for v_ref, ve_ref in zip(v_refs, vext_refs):
            ve_ref[:, :DA_VDIM] = v_ref[...]
            lane = lax.broadcasted_iota(jnp.int32, (v_ref.shape[0], DA_VDIM), 1)
            ve_ref[:, DA_VDIM:] = jnp.where(lane == 0, 1.0, 0.0).astype(BF16)

    q = q_ref[...]
    first = lax.broadcasted_iota(jnp.int32, q.shape, 1) < DA_HEAD_DIM
    zero = jnp.zeros_like(q)
    maps = []
    for qm in (jnp.where(first, q, zero), jnp.where(first, zero, q)):
        ss = [jnp.dot(qm, kt_ref[...], preferred_element_type=F32) for kt_ref in kt_refs]
        m = functools.reduce(jnp.maximum, [jnp.max(s, axis=-1, keepdims=True) for s in ss])
        acc = None
        for s, ve_ref in zip(ss, vext_refs):
            t = jnp.dot(jnp.exp(s - m).astype(BF16), ve_ref[...], preferred_element_type=F32)
            acc = t if acc is None else acc + t
        maps.append(acc[:, :DA_VDIM] / acc[:, DA_VDIM:DA_VDIM + 1])
    lam = (jnp.exp(jnp.sum(lq1_ref[...] * lk1_ref[...], axis=-1, keepdims=True))
           - jnp.exp(jnp.sum(lq2_ref[...] * lk2_ref[...], axis=-1, keepdims=True)) + lambda_init)
    o = maps[0] - lam * maps[1]
    o_ref[...] = (_rms(o) * subg_ref[...] * (1.0 - lambda_init)).astype(BF16)


def _attention(layer, q, kt, v, lam_params, subg, latent):
    lambda_init = 0.8 - 0.6 * math.exp(-0.3 * layer)
    ctx_blk0 = N_LAT // CTX_LEN
    small = lambda: pl.BlockSpec((1, DA_HEAD_DIM), lambda b, h, t: (0, 0))
    tail_specs = [small(), small(), small(), small(), pl.BlockSpec((1, DA_VDIM), lambda b, h, t: (0, 0))]
    kt_ctx = pl.BlockSpec((DA_VDIM, CTX_LEN), lambda b, h, t: (h, ctx_blk0 + b))
    v_ctx = pl.BlockSpec((CTX_LEN, DA_VDIM), lambda b, h, t: (ctx_blk0 + b, h))
    if latent:
        tq, n_q, n_rows = TQ, SEQ // TQ, N_LAT
        q_map = lambda b, h, t: (b * n_q + t, h)
        o_map = q_map
        kt_specs = [kt_ctx, pl.BlockSpec((DA_VDIM, SEQ), lambda b, h, t: (h, b))]
        v_specs = [v_ctx, pl.BlockSpec((SEQ, DA_VDIM), lambda b, h, t: (b, h))]
        key_lens = [CTX_LEN, SEQ]
    else:
        tq, n_q, n_rows = CTX_LEN, 1, N_CTX
        q_map = lambda b, h, t: (ctx_blk0 + b, h)
        o_map = lambda b, h, t: (b, h)
        kt_specs, v_specs, key_lens = [kt_ctx], [v_ctx], [CTX_LEN]
    n_src = len(key_lens)
    return pl.pallas_call(
        functools.partial(_attn_kernel, n_src, lambda_init),
        grid=(BATCH, DA_HEADS, n_q),
        in_specs=[pl.BlockSpec((tq, DA_VDIM), q_map)] + kt_specs + v_specs + tail_specs,
        out_specs=pl.BlockSpec((tq, DA_VDIM), o_map),
        out_shape=jax.ShapeDtypeStruct((n_rows, DA_WIDTH), BF16),
        scratch_shapes=[pltpu.VMEM((n, 2 * DA_VDIM), BF16) for n in key_lens],
        compiler_params=_params(("parallel", "parallel", "arbitrary"), 48),
        name=f"attn_l{layer}_{'lat' if latent else 'ctx'}",
    )(q, *([kt] * n_src), *([v] * n_src), *lam_params, subg)


def _out_proj_kernel(n_lat_tiles, aa_ref, ab_ref, u_ref, gn_ref, sgw_ref, sgb_ref, w_ref, pg_ref, mod_ref,
                     xa_ref, xb_ref, o_ref, s_ref):
    i = pl.program_id(0)
    a = jnp.where(i < n_lat_tiles, aa_ref[...], ab_ref[...])
    for c in range(TM // CHUNK):
        rows = slice(c * CHUNK, (c + 1) * CHUNK)
        for g in range(SG_GROUPS):
            cols = slice(g * CHUNK, (g + 1) * CHUNK)
            mixed = jnp.dot(sgw_ref[g], gn_ref[rows, cols], preferred_element_type=F32) + sgb_ref[:, cols]
            s_ref[rows, cols] = (u_ref[rows, cols].astype(F32) * mixed).astype(BF16)
    o = (jnp.dot(a, w_ref[:DA_WIDTH, :], preferred_element_type=F32)
         + jnp.dot(s_ref[...], w_ref[DA_WIDTH:, :], preferred_element_type=F32))
    x = jnp.where(i < n_lat_tiles, xa_ref[...], xb_ref[...])
    o_ref[...] = x + mod_ref[2:3, :] * (_rms(o) * pg_ref[...])


def _out_proj(layer, n_tiles, a_lat, a_ctx, u, gn, sg_w, sg_bias, w_out, post_g, mod, xa, xb, xb_tile0):
    n_lat_tiles = N_LAT // TM
    row = lambda i: (i, 0)
    half = lambda: pl.BlockSpec((TM, DA_WIDTH), row)
    return pl.pallas_call(
        functools.partial(_out_proj_kernel, n_lat_tiles),
        grid=(n_tiles,),
        in_specs=[
            pl.BlockSpec((TM, DA_WIDTH), lambda i: (jnp.minimum(i, n_lat_tiles - 1), 0)),
            pl.BlockSpec((TM, DA_WIDTH), lambda i: (jnp.maximum(i - n_lat_tiles, 0), 0)),
            half(), half(),
            _resident((SG_GROUPS, CHUNK, CHUNK), lambda i: (0, 0, 0)),
            _resident((CHUNK, SG_WIDTH), lambda i: (0, 0)),
            _resident((D_MODEL, D_MODEL), lambda i: (0, 0)),
            _resident((1, D_MODEL), lambda i: (0, 0)),
            _mod_spec(layer),
            pl.BlockSpec((TM, D_MODEL), lambda i: (jnp.minimum(i, n_lat_tiles - 1), 0)),
            pl.BlockSpec((TM, D_MODEL), lambda i: (xb_tile0 + jnp.maximum(i - n_lat_tiles, 0), 0)),
        ],
        out_specs=pl.BlockSpec((TM, D_MODEL), row),
        out_shape=jax.ShapeDtypeStruct((n_tiles * TM, D_MODEL), F32),
        scratch_shapes=[pltpu.VMEM((TM, SG_WIDTH), BF16)],
        compiler_params=_params(("parallel",), 48),
        name=f"out_proj_l{layer}",
    )(a_lat, a_ctx, u, gn, sg_w, sg_bias, w_out, post_g, mod, xa, xb)


def _ffn_kernel(x_ref, mod_ref, g_ref, w1_ref, w3_ref, w2_ref, pg_ref, o_ref):
    x = x_ref[...]
    f = ((_rms(x) * g_ref[...]) * (1.0 + mod_ref[4:5, :]) + mod_ref[3:4, :]).astype(BF16)
    y = None
    for k in range(D_FF // TF_DENSE):
        cols = slice(k * TF_DENSE, (k + 1) * TF_DENSE)
        h1 = jnp.dot(f, w1_ref[:, cols], preferred_element_type=F32)
        h3 = jnp.dot(f, w3_ref[:, cols], preferred_element_type=F32)
        t = jnp.dot((_silu(h1) * h3).astype(BF16), w2_ref[cols, :], preferred_element_type=F32)
        y = t if y is None else y + t
    o_ref[...] = x + mod_ref[5:6, :] * (_rms(y) * pg_ref[...])


def _ffn(layer, x_all, mod, pre_g, w1, w3, w2, post_g):
    row = lambda i: (i, 0)
    return pl.pallas_call(
        _ffn_kernel,
        grid=(N_ALL // TM,),
        in_specs=[
            pl.BlockSpec((TM, D_MODEL), row),
            _mod_spec(layer),
            _resident((1, D_MODEL), lambda i: (0, 0)),
            _resident((D_MODEL, D_FF), lambda i: (0, 0)),
            _resident((D_MODEL, D_FF), lambda i: (0, 0)),
            _resident((D_FF, D_MODEL), lambda i: (0, 0)),
            _resident((1, D_MODEL), lambda i: (0, 0)),
        ],
        out_specs=pl.BlockSpec((TM, D_MODEL), row),
        out_shape=jax.ShapeDtypeStruct((N_ALL, D_MODEL), F32),
        compiler_params=_params(("parallel",), 56),
        name=f"ffn_l{layer}",
    )(x_all, mod, pre_g, w1, w3, w2, post_g)


def _ffn_input(x, mod_ref, g_ref):
    return (_rms(x) * g_ref[...]) * (1.0 + mod_ref[4:5, :]) + mod_ref[3:4, :]


def _router_kernel(x_ref, mod_ref, g_ref, rw_ref, ids_ref, rank_ref, gate_ref, cnt_ref, carry_ref):
    @pl.when(pl.program_id(0) == 0)
    def _():
        carry_ref[...] = jnp.zeros_like(carry_ref)

    f = _ffn_input(x_ref[...], mod_ref, g_ref)
    logits = jnp.dot(f, rw_ref[...], preferred_element_type=F32, precision=lax.Precision.HIGHEST)
    lane = lax.broadcasted_iota(jnp.int32, logits.shape, 1)
    v1 = jnp.max(logits, axis=-1, keepdims=True)
    i1 = jnp.min(jnp.where(logits == v1, lane, N_EXPERTS), axis=-1, keepdims=True)
    rest = jnp.where(lane == i1, -jnp.inf, logits)
    v2 = jnp.max(rest, axis=-1, keepdims=True)
    i2 = jnp.min(jnp.where(rest == v2, lane, N_EXPERTS), axis=-1, keepdims=True)
    e = jnp.exp(v2 - v1)
    w1 = 1.0 / (1.0 + e)

    sel1, sel2 = lane == i1, lane == i2
    onehot = jnp.where(sel1 | sel2, 1.0, 0.0)
    r = lax.broadcasted_iota(jnp.int32, (TM, TM), 0)
    c = lax.broadcasted_iota(jnp.int32, (TM, TM), 1)
    tri = jnp.where(c < r, 1.0, 0.0).astype(BF16)
    prefix = jnp.dot(tri, onehot.astype(BF16), preferred_element_type=F32) + carry_ref[...]
    rank1 = jnp.sum(jnp.where(sel1, prefix, 0.0), axis=-1, keepdims=True)
    rank2 = jnp.sum(jnp.where(sel2, prefix, 0.0), axis=-1, keepdims=True)
    carry_ref[...] += jnp.sum(onehot, axis=0, keepdims=True)

    slot0 = lax.broadcasted_iota(jnp.int32, (TM, 2), 1) == 0
    ids_ref[...] = jnp.where(slot0, i1, i2)
    rank_ref[...] = jnp.where(slot0, rank1, rank2).astype(jnp.int32)
    gate_ref[...] = jnp.where(slot0, w1, e * w1)
    cnt_ref[...] = carry_ref[...].astype(jnp.int32)


def _router(layer, x_lat, mod, pre_g, router_w):
    row = lambda i: (i, 0)
    pair = lambda: pl.BlockSpec((TM, 2), row)
    return pl.pallas_call(
        _router_kernel,
        grid=(N_LAT // TM,),
        in_specs=[
            pl.BlockSpec((TM, D_MODEL), row),
            _mod_spec(layer),
            _resident((1, D_MODEL), lambda i: (0, 0)),
            _resident((D_MODEL, N_EXPERTS), lambda i: (0, 0)),
        ],
        out_specs=[pair(), pair(), pair(), pl.BlockSpec((1, N_EXPERTS), lambda i: (0, 0))],
        out_shape=[jax.ShapeDtypeStruct((N_LAT, 2), jnp.int32),
                   jax.ShapeDtypeStruct((N_LAT, 2), jnp.int32),
                   jax.ShapeDtypeStruct((N_LAT, 2), F32),
                   jax.ShapeDtypeStruct((1, N_EXPERTS), jnp.int32)],
        scratch_shapes=[pltpu.VMEM((1, N_EXPERTS), F32)],
        compiler_params=_params(("arbitrary",), 32),
        name=f"router_l{layer}",
    )(x_lat, mod, pre_g, router_w)


def _routing_tables(ids, rank, cnt):
    cnt = cnt.reshape(N_EXPERTS)
    tiles = (cnt + TM - 1) // TM
    tile_end = jnp.cumsum(tiles)
    off = (tile_end - tiles) * TM
    dest = (off[ids] + rank).reshape(N_LAT // TM, 1, 2 * TM)
    j = jnp.arange(XS_TILES, dtype=jnp.int32)
    n_used = tile_end[-1]
    valid = (j < n_used).astype(jnp.int32)
    jc = jnp.minimum(j, n_used - 1)
    texp = jnp.minimum(jnp.sum(jc[:, None] >= tile_end[None, :], axis=1), N_EXPERTS - 1).astype(jnp.int32)
    pad_lo = (off + cnt).astype(jnp.int32)
    pad_hi = (tile_end * TM).astype(jnp.int32).at[N_EXPERTS - 1].set(XS_TILES * TM)
    return dest.astype(jnp.int32), texp, valid, pad_lo, pad_hi


def _row_copy(src_ref, src_row, dst_ref, dst_row, sem):
    return pltpu.make_async_copy(src_ref.at[pl.ds(src_row, 1)], dst_ref.at[pl.ds(dst_row, 1)], sem)


def _dispatch_kernel(pad_lo_ref, pad_hi_ref, dest_ref, x_ref, mod_ref, g_ref, xs_ref, f_ref, z_ref, sem):
    @pl.when(pl.program_id(0) == 0)
    def _():
        z_ref[...] = jnp.zeros_like(z_ref)
        for e in range(N_EXPERTS):
            lo, hi = pad_lo_ref[e], pad_hi_ref[e]
            lax.fori_loop(lo, hi, lambda r, _: _row_copy(z_ref, 0, xs_ref, r, sem).start(), None)
            lax.fori_loop(lo, hi, lambda r, _: _row_copy(z_ref, 0, xs_ref, r, sem).wait(), None)

    f_ref[...] = _ffn_input(x_ref[...], mod_ref, g_ref)

    def issue(t, _):
        _row_copy(f_ref, t, xs_ref, dest_ref[0, 2 * t], sem).start()
        _row_copy(f_ref, t, xs_ref, dest_ref[0, 2 * t + 1], sem).start()

    lax.fori_loop(0, TM, issue, None)
    for _ in range(2):
        pltpu.make_async_copy(f_ref, xs_ref.at[pl.ds(0, TM)], sem).wait()


def _dispatch(layer, x_lat, mod, pre_g, dest, pad_lo, pad_hi):
    smem = lambda: pl.BlockSpec(memory_space=pltpu.SMEM)
    return pl.pallas_call(
        _dispatch_kernel,
        grid=(N_LAT // TM,),
        in_specs=[
            smem(), smem(),
            pl.BlockSpec((None, 1, 2 * TM), lambda i: (i, 0, 0), memory_space=pltpu.SMEM),
            pl.BlockSpec((TM, D_MODEL), lambda i: (i, 0)),
            _mod_spec(layer),
            _resident((1, D_MODEL), lambda i: (0, 0)),
        ],
        out_specs=pl.BlockSpec(memory_space=pl.ANY),
        out_shape=jax.ShapeDtypeStruct((XS_TILES * TM, D_MODEL), F32),
        scratch_shapes=[pltpu.VMEM((TM, D_MODEL), F32), pltpu.VMEM((8, D_MODEL), F32),
                        pltpu.SemaphoreType.DMA(())],
        compiler_params=_params(("arbitrary",), 32),
        name=f"moe_dispatch_l{layer}",
    )(pad_lo, pad_hi, dest, x_lat, mod, pre_g)


def _expert_kernel(texp_ref, tvalid_ref, xs_ref, w1_ref, w3_ref, w2_ref, ys_ref, xb_ref, acc_ref):
    j = pl.program_id(0)
    k = pl.program_id(1)

    @pl.when((tvalid_ref[j] == 0) & (k == 0))
    def _():
        ys_ref[...] = jnp.zeros_like(ys_ref)

    @pl.when(tvalid_ref[j] == 1)
    def _():
        @pl.when(k == 0)
        def _():
            xb_ref[...] = xs_ref[...].astype(BF16)
            acc_ref[...] = jnp.zeros_like(acc_ref)

        xb = xb_ref[...]
        h1 = jnp.dot(xb, w1_ref[...], preferred_element_type=F32)
        h3 = jnp.dot(xb, w3_ref[...], preferred_element_type=F32)
        acc_ref[...] += jnp.dot((_silu(h1) * h3).astype(BF16), w2_ref[...], preferred_element_type=F32)

        @pl.when(k == pl.num_programs(1) - 1)
        def _():
            ys_ref[...] = acc_ref[...]


def _experts(layer, xs, texp, tvalid, w1, w3, w2):
    n_k = D_EXPERT // TF_MOE
    row = lambda j, k, texp, tvalid: (j, 0)
    kk = lambda j, k, tvalid: jnp.where(tvalid[j] == 1, k, n_k - 1)
    return pl.pallas_call(
        _expert_kernel,
        grid_spec=pltpu.PrefetchScalarGridSpec(
            num_scalar_prefetch=2,
            grid=(XS_TILES, n_k),
            in_specs=[
                pl.BlockSpec((TM, D_MODEL), row),
                pl.BlockSpec((None, D_MODEL, TF_MOE), lambda j, k, texp, tvalid: (texp[j], 0, kk(j, k, tvalid))),
                pl.BlockSpec((None, D_MODEL, TF_MOE), lambda j, k, texp, tvalid: (texp[j], 0, kk(j, k, tvalid))),
                pl.BlockSpec((None, TF_MOE, D_MODEL), lambda j, k, texp, tvalid: (texp[j], kk(j, k, tvalid), 0)),
            ],
            out_specs=pl.BlockSpec((TM, D_MODEL), row),
            scratch_shapes=[pltpu.VMEM((TM, D_MODEL), BF16), pltpu.VMEM((TM, D_MODEL), F32)],
        ),
        out_shape=jax.ShapeDtypeStruct((XS_TILES * TM, D_MODEL), F32),
        compiler_params=_params(("arbitrary", "arbitrary"), 48),
        name=f"moe_experts_l{layer}",
    )(texp, tvalid, xs, w1, w3, w2)


def _combine_kernel(dest_ref, ys_ref, gate_ref, x_ref, mod_ref, pg_ref, o_ref, buf_ref, sem):
    def issue(t, _):
        _row_copy(ys_ref, dest_ref[0, 2 * t], buf_ref.at[0], t, sem).start()
        _row_copy(ys_ref, dest_ref[0, 2 * t + 1], buf_ref.at[1], t, sem).start()

    lax.fori_loop(0, TM, issue, None)
    for s in range(2):
        pltpu.make_async_copy(ys_ref.at[pl.ds(0, TM)], buf_ref.at[s], sem).wait()
    y = gate_ref[:, 0:1] * buf_ref[0] + gate_ref[:, 1:2] * buf_ref[1]
    o_ref[...] = x_ref[...] + mod_ref[5:6, :] * (_rms(y) * pg_ref[...])


def _combine(layer, ys, dest, gate, x_lat, mod, post_g):
    row = lambda i: (i, 0)
    return pl.pallas_call(
        _combine_kernel,
        grid=(N_LAT // TM,),
        in_specs=[
            pl.BlockSpec((None, 1, 2 * TM), lambda i: (i, 0, 0), memory_space=pltpu.SMEM),
            pl.BlockSpec(memory_space=pl.ANY),
            pl.BlockSpec((TM, 2), row),
            pl.BlockSpec((TM, D_MODEL), row),
            _mod_spec(layer),
            _resident((1, D_MODEL), lambda i: (0, 0)),
        ],
        out_specs=pl.BlockSpec((TM, D_MODEL), row),
        out_shape=jax.ShapeDtypeStruct((N_LAT, D_MODEL), F32),
        scratch_shapes=[pltpu.VMEM((2, TM, D_MODEL), F32), pltpu.SemaphoreType.DMA(())],
        compiler_params=_params(("arbitrary",), 32),
        name=f"moe_combine_l{layer}",
    )(dest, ys, gate, x_lat, mod, post_g)


def kernel(x, c, ctx, c_ctx, ada_w, ada_b, pre_mix_g, post_mix_g, pre_ffn_g, post_ffn_g, w_in, w_out,
           lam_q1, lam_k1, lam_q2, lam_k2, subln_g, sg_ln_g, sg_ln_b, sg_w, sg_b,
           ffn_w1, ffn_w3, ffn_w2, router_w, moe_w1, moe_w3, moe_w2):
    x_lat = x.reshape(N_LAT, D_MODEL)
    x_ctx = ctx.reshape(N_CTX, D_MODEL)
    cc = jnp.concatenate([c, c_ctx[None, :], jnp.zeros((MOD_ROWS - BATCH - 1, D_MODEL), F32)], axis=0)
    mod = _modulation(cc, ada_w, ada_b)
    rope_tabs = _rope_tables()
    n_lat_tiles = N_LAT // TM
    x_all = None
    for l in range(DEPTH):
        last = l == DEPTH - 1
        vec = lambda a: a[l][None, :]
        if l == 0:
            xa, xb, xb_tile0 = x_lat, x_ctx, 0
        else:
            xa, xb, xb_tile0 = x_all, x_all, n_lat_tiles
        q, kt, v, u, gn = _in_proj(l, xa, xb, xb_tile0, mod, vec(pre_mix_g), w_in[l].astype(BF16),
                                   rope_tabs, vec(sg_ln_g), vec(sg_ln_b))
        lam_params = [vec(lam_q1), vec(lam_k1), vec(lam_q2), vec(lam_k2)]
        a_lat = _attention(l, q, kt, v, lam_params, vec(subln_g), latent=True)
        a_ctx = a_lat if last else _attention(l, q, kt, v, lam_params, vec(subln_g), latent=False)
        sg_bias = jnp.repeat(sg_b[l].T, CHUNK, axis=1)
        n_tiles = n_lat_tiles if last else N_ALL // TM
        x_mid = _out_proj(l, n_tiles, a_lat, a_ctx, u, gn, sg_w[l].astype(BF16), sg_bias,
                          w_out[l].astype(BF16), vec(post_mix_g), mod, xa, xb, xb_tile0)
        if l % 2 == 0:
            i = l // 2
            x_all = _ffn(l, x_mid, mod, vec(pre_ffn_g), ffn_w1[i].astype(BF16), ffn_w3[i].astype(BF16),
                         ffn_w2[i].astype(BF16), vec(post_ffn_g))
        else:
            i = l // 2
            ids, rank, gate, cnt = _router(l, x_mid, mod, vec(pre_ffn_g), router_w[i])
            dest, texp, tvalid, pad_lo, pad_hi = _routing_tables(ids, rank, cnt)
            xs = _dispatch(l, x_mid, mod, vec(pre_ffn_g), dest, pad_lo, pad_hi)
            ys = _experts(l, xs, texp, tvalid, moe_w1[i].astype(BF16), moe_w3[i].astype(BF16),
                          moe_w2[i].astype(BF16))
            x_all = _combine(l, ys, dest, gate, x_mid, mod, vec(post_ffn_g))
    return x_all.reshape(BATCH, SEQ, D_MODEL)
```

```python
import functools
import math

import jax
import jax.numpy as jnp
from jax import lax
from jax.experimental import pallas as pl
from jax.experimental.pallas import tpu as pltpu

D_MODEL = 1024
BATCH = 16
SEQ = 2048
DEPTH = 2
GRID_W = 64
CTX_LEN = 256
DA_WIDTH = 512
DA_HEADS = 4
DA_VDIM = 128
DA_HEAD_DIM = 64
SG_WIDTH = 512
SG_GROUPS = 4
CHUNK = 128
ROPE_BASE = 10000.0
D_FF = 2816
N_EXPERTS = 8
D_EXPERT = 3584
EPS = 1e-6
IN_COLS = 3 * DA_WIDTH + 2 * SG_WIDTH

N_LAT = BATCH * SEQ
N_CTX = BATCH * CTX_LEN
N_ALL = N_LAT + N_CTX
MOD_ROWS = 32

TM = 512
TQ = 1024
TQ_SUB = 512
KEY_SEG = 2048
TF_DENSE = 1408
TF_MOE = 1792
XS_TILES = 2 * N_LAT // TM + N_EXPERTS
MIB = 2 ** 20

BF16 = jnp.bfloat16
F32 = jnp.float32


def _rms(x):
    return x * lax.rsqrt(jnp.mean(x * x, axis=-1, keepdims=True) + EPS)


def _silu(x):
    return x * jax.nn.sigmoid(x)


def _resident(block_shape, index_map):
    return pl.BlockSpec(block_shape, index_map, pipeline_mode=pl.Buffered(1))


def _params(semantics, vmem_mib):
    return pltpu.CompilerParams(dimension_semantics=semantics, vmem_limit_bytes=vmem_mib * MIB)


def _mod_kernel(cc_ref, w_ref, b_ref, o_ref):
    a = _silu(cc_ref[...]).astype(BF16)
    o_ref[...] = jnp.dot(a, w_ref[...].astype(BF16), preferred_element_type=F32) + b_ref[...]


def _modulation(cc, ada_w, ada_b):
    tn = 1536
    out = pl.pallas_call(
        _mod_kernel,
        grid=(DEPTH, 6 * D_MODEL // tn),
        in_specs=[
            pl.BlockSpec((MOD_ROWS, D_MODEL), lambda l, j: (0, 0)),
            pl.BlockSpec((None, D_MODEL, tn), lambda l, j: (l, 0, j)),
            pl.BlockSpec((None, 1, tn), lambda l, j: (l, 0, j)),
        ],
        out_specs=pl.BlockSpec((None, MOD_ROWS, tn), lambda l, j: (l, 0, j)),
        out_shape=jax.ShapeDtypeStruct((DEPTH, MOD_ROWS, 6 * D_MODEL), F32),
        compiler_params=_params(("parallel", "parallel"), 40),
        name="modulation",
    )(cc, ada_w, ada_b.reshape(DEPTH, 1, 6 * D_MODEL))
    return out.reshape(DEPTH, MOD_ROWS, 6, D_MODEL)


def _mod_spec(layer):
    tiles_per_batch = SEQ // TM
    return pl.BlockSpec((None, None, 6, D_MODEL),
                        lambda i: (layer, jnp.minimum(i // tiles_per_batch, BATCH), 0, 0))


def _in_proj_kernel(n_lat_tiles, xa_ref, xb_ref, mod_ref, g_ref, w_ref, cos_ref, sa_ref, sb_ref,
                    lng_ref, lnb_ref, q_ref, kt_ref, v_ref, u_ref, gn_ref):
    i = pl.program_id(0)
    x = jnp.where(i < n_lat_tiles, xa_ref[...], xb_ref[...])
    h = _rms(x) * g_ref[...]
    h = h * (1.0 + mod_ref[1:2, :]) + mod_ref[0:1, :]
    p = jnp.dot(h.astype(BF16), w_ref[...], preferred_element_type=F32)

    cos, sa, sb = cos_ref[...], sa_ref[...], sb_ref[...]

    def rope(t):
        return t * cos + pltpu.roll(t, 112, 1) * sa + pltpu.roll(t, 16, 1) * sb

    for hd in range(DA_HEADS):
        lo, hi = hd * DA_VDIM, (hd + 1) * DA_VDIM
        q_ref[:, lo:hi] = (rope(p[:, lo:hi]) * (DA_HEAD_DIM ** -0.5)).astype(BF16)
        kt_ref[lo:hi, :] = rope(p[:, DA_WIDTH + lo:DA_WIDTH + hi]).T.astype(BF16)
    v_ref[...] = p[:, 2 * DA_WIDTH:3 * DA_WIDTH].astype(BF16)
    u_ref[...] = p[:, 3 * DA_WIDTH:3 * DA_WIDTH + SG_WIDTH].astype(BF16)
    gv = p[:, 3 * DA_WIDTH + SG_WIDTH:]
    mu = jnp.mean(gv, axis=-1, keepdims=True)
    var = jnp.mean(jnp.square(gv - mu), axis=-1, keepdims=True)
    gn_ref[...] = ((gv - mu) * lax.rsqrt(var + EPS) * lng_ref[...] + lnb_ref[...]).astype(BF16)


def _in_proj(layer, xa, xb, xb_tile0, mod, pre_g, w_in, rope_tabs, ln_g, ln_b):
    n_lat_tiles = N_LAT // TM
    n_tiles = N_ALL // TM
    tiles_per_batch = SEQ // TM
    tab_spec = pl.BlockSpec(
        (TM, DA_VDIM), lambda i: (jnp.where(i < n_lat_tiles, i % tiles_per_batch, tiles_per_batch), 0))
    row = lambda i: (i, 0)
    outs = pl.pallas_call(
        functools.partial(_in_proj_kernel, n_lat_tiles),
        grid=(n_tiles,),
        in_specs=[
            pl.BlockSpec((TM, D_MODEL), lambda i: (jnp.minimum(i, n_lat_tiles - 1), 0)),
            pl.BlockSpec((TM, D_MODEL), lambda i: (xb_tile0 + jnp.maximum(i - n_lat_tiles, 0), 0)),
            _mod_spec(layer),
            _resident((1, D_MODEL), lambda i: (0, 0)),
            _resident((D_MODEL, IN_COLS), lambda i: (0, 0)),
            tab_spec, tab_spec, tab_spec,
            _resident((1, SG_WIDTH), lambda i: (0, 0)),
            _resident((1, SG_WIDTH), lambda i: (0, 0)),
        ],
        out_specs=[
            pl.BlockSpec((TM, DA_WIDTH), row),
            pl.BlockSpec((DA_WIDTH, TM), lambda i: (0, i)),
            pl.BlockSpec((TM, DA_WIDTH), row),
            pl.BlockSpec((TM, SG_WIDTH), row),
            pl.BlockSpec((TM, SG_WIDTH), row),
        ],
        out_shape=[
            jax.ShapeDtypeStruct((N_ALL, DA_WIDTH), BF16),
            jax.ShapeDtypeStruct((DA_WIDTH, N_ALL), BF16),
            jax.ShapeDtypeStruct((N_ALL, DA_WIDTH), BF16),
            jax.ShapeDtypeStruct((N_ALL, SG_WIDTH), BF16),
            jax.ShapeDtypeStruct((N_ALL, SG_WIDTH), BF16),
        ],
        compiler_params=_params(("parallel",), 48),
        name=f"in_proj_l{layer}",
    )(xa, xb, mod, pre_g, w_in, *rope_tabs, ln_g, ln_b)
    return outs


def _rope_tables():
    pos = jnp.arange(SEQ, dtype=jnp.int32)
    row = (pos // GRID_W).astype(F32)
    col = (pos % GRID_W).astype(F32)
    half = DA_HEAD_DIM // 2
    inv = ROPE_BASE ** (-jnp.arange(0, half, 2, dtype=F32) / half)
    ang_r = row[:, None] * inv
    ang_c = col[:, None] * inv
    ang = jnp.concatenate([ang_r, ang_r, ang_c, ang_c], axis=-1)
    cos = jnp.tile(jnp.cos(ang), (1, 2))
    sin = jnp.tile(jnp.sin(ang), (1, 2))
    first = (jnp.arange(DA_VDIM) % 32) < 16
    sa = jnp.where(first, -sin, 0.0)
    sb = jnp.where(first, 0.0, sin)
    ident = jnp.zeros((TM, DA_VDIM), F32)
    return (jnp.concatenate([cos, ident + 1.0], axis=0),
            jnp.concatenate([sa, ident], axis=0),
            jnp.concatenate([sb, ident], axis=0))


def _attn_kernel(n_src, lambda_init, q_ref, *refs):
    kt_refs = refs[:n_src]
    v_refs = refs[n_src:2 * n_src]
    lq1_ref, lk1_ref, lq2_ref, lk2_ref, subg_ref = refs[2 * n_src:2 * n_src + 5]
    o_ref = refs[-1 - n_src]
    vext_refs = refs[-n_src:]

    @pl.when(pl.program_id(2) == 0)
    def _():
        for v_ref, ve_ref in zip(v_refs, vext_refs):
            ve_ref[:, :DA_VDIM] = v_ref[...]
            ve_ref[:, DA_VDIM:] = jnp.ones((v_ref.shape[0], DA_VDIM), BF16)

    segs = []
    for kt_ref, ve_ref in zip(kt_refs, vext_refs):
        n_keys = ve_ref.shape[0]
        for lo in range(0, n_keys, KEY_SEG):
            segs.append((kt_ref, ve_ref, lo, min(lo + KEY_SEG, n_keys)))

    lam = (jnp.exp(jnp.sum(lq1_ref[...] * lk1_ref[...], axis=-1, keepdims=True))
           - jnp.exp(jnp.sum(lq2_ref[...] * lk2_ref[...], axis=-1, keepdims=True)) + lambda_init)
    tq_sub = min(TQ_SUB, q_ref.shape[0])

    def scores(r0):
        q = q_ref[r0:r0 + tq_sub, :]
        first = lax.broadcasted_iota(jnp.int32, q.shape, 1) < DA_HEAD_DIM
        zero = jnp.zeros_like(q)
        qms = (jnp.where(first, q, zero), jnp.where(first, zero, q))
        return [[jnp.dot(qm, kt_ref[:, lo:hi], preferred_element_type=F32) for kt_ref, _, lo, hi in segs]
                for qm in qms]

    def finish(r0, ss_maps):
        maps = []
        for ss in ss_maps:
            m = functools.reduce(jnp.maximum, [jnp.max(s, axis=-1, keepdims=True) for s in ss])
            ts = [jnp.dot(jnp.exp(s - m).astype(BF16), ve_ref[lo:hi, :], preferred_element_type=F32)
                  for s, (_, ve_ref, lo, hi) in zip(ss, segs)]
            acc = functools.reduce(jnp.add, ts)
            maps.append(acc[:, :DA_VDIM] / acc[:, DA_VDIM:])
        o = maps[0] - lam * maps[1]
        o_ref[r0:r0 + tq_sub, :] = (_rms(o) * subg_ref[...] * (1.0 - lambda_init)).astype(BF16)

    starts = list(range(0, q_ref.shape[0], tq_sub))
    pending = scores(starts[0])
    for idx, r0 in enumerate(starts):
        nxt = scores(starts[idx + 1]) if idx + 1 < len(starts) else None
        finish(r0, pending)
        pending = nxt


def _attention(layer, q, kt, v, lam_params, subg, latent):
    lambda_init = 0.8 - 0.6 * math.exp(-0.3 * layer)
    ctx_blk0 = N_LAT // CTX_LEN
    small = lambda: pl.BlockSpec((1, DA_HEAD_DIM), lambda b, h, t: (0, 0))
    tail_specs = [small(), small(), small(), small(), pl.BlockSpec((1, DA_VDIM), lambda b, h, t: (0, 0))]
    kt_ctx = pl.BlockSpec((DA_VDIM, CTX_LEN), lambda b, h, t: (h, ctx_blk0 + b))
    v_ctx = pl.BlockSpec((CTX_LEN, DA_VDIM), lambda b, h, t: (ctx_blk0 + b, h))
    if latent:
        tq, n_q, n_rows = TQ, SEQ // TQ, N_LAT
        q_map = lambda b, h, t: (b * n_q + t, h)
        o_map = q_map
        kt_specs = [kt_ctx, pl.BlockSpec((DA_VDIM, SEQ), lambda b, h, t: (h, b))]
        v_specs = [v_ctx, pl.BlockSpec((SEQ, DA_VDIM), lambda b, h, t: (b, h))]
        key_lens = [CTX_LEN, SEQ]
    else:
        tq, n_q, n_rows = CTX_LEN, 1, N_CTX
        q_map = lambda b, h, t: (ctx_blk0 + b, h)
        o_map = lambda b, h, t: (b, h)
        kt_specs, v_specs, key_lens = [kt_ctx], [v_ctx], [CTX_LEN]
    n_src = len(key_lens)
    return pl.pallas_call(
        functools.partial(_attn_kernel, n_src, lambda_init),
        grid=(BATCH, DA_HEADS, n_q),
        in_specs=[pl.BlockSpec((tq, DA_VDIM), q_map)] + kt_specs + v_specs + tail_specs,
        out_specs=pl.BlockSpec((tq, DA_VDIM), o_map),
        out_shape=jax.ShapeDtypeStruct((n_rows, DA_WIDTH), BF16),
        scratch_shapes=[pltpu.VMEM((n, 2 * DA_VDIM), BF16) for n in key_lens],
        compiler_params=_params(("parallel", "parallel", "arbitrary"), 48),
        name=f"attn_l{layer}_{'lat' if latent else 'ctx'}",
    )(q, *([kt] * n_src), *([v] * n_src), *lam_params, subg)


def _out_proj_kernel(n_lat_tiles, aa_ref, ab_ref, u_ref, gn_ref, sgw_ref, sgb_ref, w_ref, pg_ref, mod_ref,
                     xa_ref, xb_ref, o_ref, s_ref):
    i = pl.program_id(0)
    a = jnp.where(i < n_lat_tiles, aa_ref[...], ab_ref[...])
    for c in range(TM // CHUNK):
        rows = slice(c * CHUNK, (c + 1) * CHUNK)
        for g in range(SG_GROUPS):
            cols = slice(g * CHUNK, (g + 1) * CHUNK)
            mixed = jnp.dot(sgw_ref[g], gn_ref[rows, cols], preferred_element_type=F32) + sgb_ref[:, cols]
            s_ref[rows, cols] = (u_ref[rows, cols].astype(F32) * mixed).astype(BF16)
    o = (jnp.dot(a, w_ref[:DA_WIDTH, :], preferred_element_type=F32)
         + jnp.dot(s_ref[...], w_ref[DA_WIDTH:, :], preferred_element_type=F32))
    x = jnp.where(i < n_lat_tiles, xa_ref[...], xb_ref[...])
    o_ref[...] = x + mod_ref[2:3, :] * (_rms(o) * pg_ref[...])


def _out_proj(layer, n_tiles, a_lat, a_ctx, u, gn, sg_w, sg_bias, w_out, post_g, mod, xa, xb, xb_tile0):
    n_lat_tiles = N_LAT // TM
    row = lambda i: (i, 0)
    half = lambda: pl.BlockSpec((TM, DA_WIDTH), row)
    return pl.pallas_call(
        functools.partial(_out_proj_kernel, n_lat_tiles),
        grid=(n_tiles,),
        in_specs=[
            pl.BlockSpec((TM, DA_WIDTH), lambda i: (jnp.minimum(i, n_lat_tiles - 1), 0)),
            pl.BlockSpec((TM, DA_WIDTH), lambda i: (jnp.maximum(i - n_lat_tiles, 0), 0)),
            half(), half(),
            _resident((SG_GROUPS, CHUNK, CHUNK), lambda i: (0, 0, 0)),
            _resident((CHUNK, SG_WIDTH), lambda i: (0, 0)),
            _resident((D_MODEL, D_MODEL), lambda i: (0, 0)),
            _resident((1, D_MODEL), lambda i: (0, 0)),
            _mod_spec(layer),
            pl.BlockSpec((TM, D_MODEL), lambda i: (jnp.minimum(i, n_lat_tiles - 1), 0)),
            pl.BlockSpec((TM, D_MODEL), lambda i: (xb_tile0 + jnp.maximum(i - n_lat_tiles, 0), 0)),
        ],
        out_specs=pl.BlockSpec((TM, D_MODEL), row),
        out_shape=jax.ShapeDtypeStruct((n_tiles * TM, D_MODEL), F32),
        scratch_shapes=[pltpu.VMEM((TM, SG_WIDTH), BF16)],
        compiler_params=_params(("parallel",), 48),
        name=f"out_proj_l{layer}",
    )(a_lat, a_ctx, u, gn, sg_w, sg_bias, w_out, post_g, mod, xa, xb)


def _ffn_kernel(x_ref, mod_ref, g_ref, w1_ref, w3_ref, w2_ref, pg_ref, o_ref):
    x = x_ref[...]
    f = ((_rms(x) * g_ref[...]) * (1.0 + mod_ref[4:5, :]) + mod_ref[3:4, :]).astype(BF16)
    y = None
    for k in range(D_FF // TF_DENSE):
        cols = slice(k * TF_DENSE, (k + 1) * TF_DENSE)
        h1 = jnp.dot(f, w1_ref[:, cols], preferred_element_type=F32)
        h3 = jnp.dot(f, w3_ref[:, cols], preferred_element_type=F32)
        t = jnp.dot((_silu(h1) * h3).astype(BF16), w2_ref[cols, :], preferred_element_type=F32)
        y = t if y is None else y + t
    o_ref[...] = x + mod_ref[5:6, :] * (_rms(y) * pg_ref[...])


def _ffn(layer, x_all, mod, pre_g, w1, w3, w2, post_g):
    row = lambda i: (i, 0)
    return pl.pallas_call(
        _ffn_kernel,
        grid=(N_ALL // TM,),
        in_specs=[
            pl.BlockSpec((TM, D_MODEL), row),
            _mod_spec(layer),
            _resident((1, D_MODEL), lambda i: (0, 0)),
            _resident((D_MODEL, D_FF), lambda i: (0, 0)),
            _resident((D_MODEL, D_FF), lambda i: (0, 0)),
            _resident((D_FF, D_MODEL), lambda i: (0, 0)),
            _resident((1, D_MODEL), lambda i: (0, 0)),
        ],
        out_specs=pl.BlockSpec((TM, D_MODEL), row),
        out_shape=jax.ShapeDtypeStruct((N_ALL, D_MODEL), F32),
        compiler_params=_params(("parallel",), 56),
        name=f"ffn_l{layer}",
    )(x_all, mod, pre_g, w1, w3, w2, post_g)


def _ffn_input(x, mod_ref, g_ref):
    return (_rms(x) * g_ref[...]) * (1.0 + mod_ref[4:5, :]) + mod_ref[3:4, :]


def _router_kernel(x_ref, mod_ref, g_ref, rw_ref, ids_ref, rank_ref, gate_ref, cnt_ref, carry_ref):
    @pl.when(pl.program_id(0) == 0)
    def _():
        carry_ref[...] = jnp.zeros_like(carry_ref)

    f = _ffn_input(x_ref[...], mod_ref, g_ref)
    rw = rw_ref[...]
    f_hi, rw_hi = f.astype(BF16), rw.astype(BF16)
    f_lo, rw_lo = (f - f_hi.astype(F32)).astype(BF16), (rw - rw_hi.astype(F32)).astype(BF16)
    logits = (jnp.dot(f_hi, rw_hi, preferred_element_type=F32) + jnp.dot(f_lo, rw_hi, preferred_element_type=F32)
              + jnp.dot(f_hi, rw_lo, preferred_element_type=F32))
    lane = lax.broadcasted_iota(jnp.int32, logits.shape, 1)
    v1 = jnp.max(logits, axis=-1, keepdims=True)
    i1 = jnp.min(jnp.where(logits == v1, lane, N_EXPERTS), axis=-1, keepdims=True)
    rest = jnp.where(lane == i1, -jnp.inf, logits)
    v2 = jnp.max(rest, axis=-1, keepdims=True)
    i2 = jnp.min(jnp.where(rest == v2, lane, N_EXPERTS), axis=-1, keepdims=True)
    e = jnp.exp(v2 - v1)
    w1 = 1.0 / (1.0 + e)

    sel1, sel2 = lane == i1, lane == i2
    onehot = jnp.where(sel1 | sel2, 1.0, 0.0)
    r = lax.broadcasted_iota(jnp.int32, (TM, TM), 0)
    c = lax.broadcasted_iota(jnp.int32, (TM, TM), 1)
    tri = jnp.where(c < r, 1.0, 0.0).astype(BF16)
    prefix = jnp.dot(tri, onehot.astype(BF16), preferred_element_type=F32) + carry_ref[...]
    rank1 = jnp.sum(jnp.where(sel1, prefix, 0.0), axis=-1, keepdims=True)
    rank2 = jnp.sum(jnp.where(sel2, prefix, 0.0), axis=-1, keepdims=True)
    carry_ref[...] += jnp.sum(onehot, axis=0, keepdims=True)

    slot0 = lax.broadcasted_iota(jnp.int32, (TM, 2), 1) == 0
    ids_ref[...] = jnp.where(slot0, i1, i2)
    rank_ref[...] = jnp.where(slot0, rank1, rank2).astype(jnp.int32)
    gate_ref[...] = jnp.where(slot0, w1, e * w1)
    cnt_ref[...] = carry_ref[...].astype(jnp.int32)


def _router(layer, x_lat, mod, pre_g, router_w):
    row = lambda i: (i, 0)
    pair = lambda: pl.BlockSpec((TM, 2), row)
    return pl.pallas_call(
        _router_kernel,
        grid=(N_LAT // TM,),
        in_specs=[
            pl.BlockSpec((TM, D_MODEL), row),
            _mod_spec(layer),
            _resident((1, D_MODEL), lambda i: (0, 0)),
            _resident((D_MODEL, N_EXPERTS), lambda i: (0, 0)),
        ],
        out_specs=[pair(), pair(), pair(), pl.BlockSpec((1, N_EXPERTS), lambda i: (0, 0))],
        out_shape=[jax.ShapeDtypeStruct((N_LAT, 2), jnp.int32),
                   jax.ShapeDtypeStruct((N_LAT, 2), jnp.int32),
                   jax.ShapeDtypeStruct((N_LAT, 2), F32),
                   jax.ShapeDtypeStruct((1, N_EXPERTS), jnp.int32)],
        scratch_shapes=[pltpu.VMEM((1, N_EXPERTS), F32)],
        compiler_params=_params(("arbitrary",), 32),
        name=f"router_l{layer}",
    )(x_lat, mod, pre_g, router_w)


def _routing_tables(ids, rank, cnt):
    cnt = cnt.reshape(N_EXPERTS)
    tiles = (cnt + TM - 1) // TM
    tile_end = jnp.cumsum(tiles)
    off = (tile_end - tiles) * TM
    dest = (off[ids] + rank).reshape(N_LAT // TM, 1, 2 * TM)
    j = jnp.arange(XS_TILES, dtype=jnp.int32)
    n_used = tile_end[-1]
    valid = (j < n_used).astype(jnp.int32)
    jc = jnp.minimum(j, n_used - 1)
    texp = jnp.minimum(jnp.sum(jc[:, None] >= tile_end[None, :], axis=1), N_EXPERTS - 1).astype(jnp.int32)
    pad_lo = (off + cnt).astype(jnp.int32)
    pad_hi = (tile_end * TM).astype(jnp.int32).at[N_EXPERTS - 1].set(XS_TILES * TM)
    return dest.astype(jnp.int32), texp, valid, pad_lo, pad_hi


def _row_copy(src_ref, src_row, dst_ref, dst_row, sem):
    return pltpu.make_async_copy(src_ref.at[pl.ds(src_row, 1)], dst_ref.at[pl.ds(dst_row, 1)], sem)


def _dispatch_kernel(pad_lo_ref, pad_hi_ref, dest_ref, x_ref, mod_ref, g_ref, xs_ref, f_ref, z_ref, sem):
    @pl.when(pl.program_id(0) == 0)
    def _():
        z_ref[...] = jnp.zeros_like(z_ref)
        for e in range(N_EXPERTS):
            lo, hi = pad_lo_ref[e], pad_hi_ref[e]
            lax.fori_loop(lo, hi, lambda r, _: _row_copy(z_ref, 0, xs_ref, r, sem).start(), None)
            lax.fori_loop(lo, hi, lambda r, _: _row_copy(z_ref, 0, xs_ref, r, sem).wait(), None)

    f_ref[...] = _ffn_input(x_ref[...], mod_ref, g_ref)

    def issue(t, _):
        _row_copy(f_ref, t, xs_ref, dest_ref[0, 2 * t], sem).start()
        _row_copy(f_ref, t, xs_ref, dest_ref[0, 2 * t + 1], sem).start()

    lax.fori_loop(0, TM, issue, None, unroll=8)
    for _ in range(2):
        pltpu.make_async_copy(f_ref, xs_ref.at[pl.ds(0, TM)], sem).wait()


def _dispatch(layer, x_lat, mod, pre_g, dest, pad_lo, pad_hi):
    smem = lambda: pl.BlockSpec(memory_space=pltpu.SMEM)
    return pl.pallas_call(
        _dispatch_kernel,
        grid=(N_LAT // TM,),
        in_specs=[
            smem(), smem(),
            pl.BlockSpec((None, 1, 2 * TM), lambda i: (i, 0, 0), memory_space=pltpu.SMEM),
            pl.BlockSpec((TM, D_MODEL), lambda i: (i, 0)),
            _mod_spec(layer),
            _resident((1, D_MODEL), lambda i: (0, 0)),
        ],
        out_specs=pl.BlockSpec(memory_space=pl.ANY),
        out_shape=jax.ShapeDtypeStruct((XS_TILES * TM, D_MODEL), F32),
        scratch_shapes=[pltpu.VMEM((TM, D_MODEL), F32), pltpu.VMEM((8, D_MODEL), F32),
                        pltpu.SemaphoreType.DMA(())],
        compiler_params=_params(("arbitrary",), 32),
        name=f"moe_dispatch_l{layer}",
    )(pad_lo, pad_hi, dest, x_lat, mod, pre_g)


def _expert_kernel(texp_ref, tvalid_ref, xs_ref, w1_ref, w3_ref, w2_ref, ys_ref, xb_ref, acc_ref):
    j = pl.program_id(0)
    k = pl.program_id(1)

    @pl.when((tvalid_ref[j] == 0) & (k == 0))
    def _():
        ys_ref[...] = jnp.zeros_like(ys_ref)

    @pl.when(tvalid_ref[j] == 1)
    def _():
        @pl.when(k == 0)
        def _():
            xb_ref[...] = xs_ref[...].astype(BF16)
            acc_ref[...] = jnp.zeros_like(acc_ref)

        xb = xb_ref[...]
        h1 = jnp.dot(xb, w1_ref[...], preferred_element_type=F32)
        h3 = jnp.dot(xb, w3_ref[...], preferred_element_type=F32)
        acc_ref[...] += jnp.dot((_silu(h1) * h3).astype(BF16), w2_ref[...], preferred_element_type=F32)

        @pl.when(k == pl.num_programs(1) - 1)
        def _():
            ys_ref[...] = acc_ref[...]


def _experts(layer, xs, texp, tvalid, w1, w3, w2):
    n_k = D_EXPERT // TF_MOE
    row = lambda j, k, texp, tvalid: (j, 0)
    kk = lambda j, k, tvalid: jnp.where(tvalid[j] == 1, k, n_k - 1)
    return pl.pallas_call(
        _expert_kernel,
        grid_spec=pltpu.PrefetchScalarGridSpec(
            num_scalar_prefetch=2,
            grid=(XS_TILES, n_k),
            in_specs=[
                pl.BlockSpec((TM, D_MODEL), row),
                pl.BlockSpec((None, D_MODEL, TF_MOE), lambda j, k, texp, tvalid: (texp[j], 0, kk(j, k, tvalid))),
                pl.BlockSpec((None, D_MODEL, TF_MOE), lambda j, k, texp, tvalid: (texp[j], 0, kk(j, k, tvalid))),
                pl.BlockSpec((None, TF_MOE, D_MODEL), lambda j, k, texp, tvalid: (texp[j], kk(j, k, tvalid), 0)),
            ],
            out_specs=pl.BlockSpec((TM, D_MODEL), row),
            scratch_shapes=[pltpu.VMEM((TM, D_MODEL), BF16), pltpu.VMEM((TM, D_MODEL), F32)],
        ),
        out_shape=jax.ShapeDtypeStruct((XS_TILES * TM, D_MODEL), F32),
        compiler_params=_params(("arbitrary", "arbitrary"), 56),
        name=f"moe_experts_l{layer}",
    )(texp, tvalid, xs, w1, w3, w2)


def _combine_kernel(dest_ref, ys_ref, gate_ref, x_ref, mod_ref, pg_ref, o_ref, buf_ref, sem):
    def issue(t, _):
        _row_copy(ys_ref, dest_ref[0, 2 * t], buf_ref.at[0], t, sem).start()
        _row_copy(ys_ref, dest_ref[0, 2 * t + 1], buf_ref.at[1], t, sem).start()

    lax.fori_loop(0, TM, issue, None, unroll=8)
    for s in range(2):
        pltpu.make_async_copy(ys_ref.at[pl.ds(0, TM)], buf_ref.at[s], sem).wait()
    y = gate_ref[:, 0:1] * buf_ref[0] + gate_ref[:, 1:2] * buf_ref[1]
    o_ref[...] = x_ref[...] + mod_ref[5:6, :] * (_rms(y) * pg_ref[...])


def _combine(layer, ys, dest, gate, x_lat, mod, post_g):
    row = lambda i: (i, 0)
    return pl.pallas_call(
        _combine_kernel,
        grid=(N_LAT // TM,),
        in_specs=[
            pl.BlockSpec((None, 1, 2 * TM), lambda i: (i, 0, 0), memory_space=pltpu.SMEM),
            pl.BlockSpec(memory_space=pl.ANY),
            pl.BlockSpec((TM, 2), row),
            pl.BlockSpec((TM, D_MODEL), row),
            _mod_spec(layer),
            _resident((1, D_MODEL), lambda i: (0, 0)),
        ],
        out_specs=pl.BlockSpec((TM, D_MODEL), row),
        out_shape=jax.ShapeDtypeStruct((N_LAT, D_MODEL), F32),
        scratch_shapes=[pltpu.VMEM((2, TM, D_MODEL), F32), pltpu.SemaphoreType.DMA(())],
        compiler_params=_params(("arbitrary",), 32),
        name=f"moe_combine_l{layer}",
    )(dest, ys, gate, x_lat, mod, post_g)


def kernel(x, c, ctx, c_ctx, ada_w, ada_b, pre_mix_g, post_mix_g, pre_ffn_g, post_ffn_g, w_in, w_out,
           lam_q1, lam_k1, lam_q2, lam_k2, subln_g, sg_ln_g, sg_ln_b, sg_w, sg_b,
           ffn_w1, ffn_w3, ffn_w2, router_w, moe_w1, moe_w3, moe_w2):
    x_lat = x.reshape(N_LAT, D_MODEL)
    x_ctx = ctx.reshape(N_CTX, D_MODEL)
    cc = jnp.concatenate([c, c_ctx[None, :], jnp.zeros((MOD_ROWS - BATCH - 1, D_MODEL), F32)], axis=0)
    mod = _modulation(cc, ada_w, ada_b)
    rope_tabs = _rope_tables()
    n_lat_tiles = N_LAT // TM
    x_all = None
    for l in range(DEPTH):
        last = l == DEPTH - 1
        vec = lambda a: a[l][None, :]
        if l == 0:
            xa, xb, xb_tile0 = x_lat, x_ctx, 0
        else:
            xa, xb, xb_tile0 = x_all, x_all, n_lat_tiles
        q, kt, v, u, gn = _in_proj(l, xa, xb, xb_tile0, mod, vec(pre_mix_g), w_in[l].astype(BF16),
                                   rope_tabs, vec(sg_ln_g), vec(sg_ln_b))
        lam_params = [vec(lam_q1), vec(lam_k1), vec(lam_q2), vec(lam_k2)]
        a_lat = _attention(l, q, kt, v, lam_params, vec(subln_g), latent=True)
        a_ctx = a_lat if last else _attention(l, q, kt, v, lam_params, vec(subln_g), latent=False)
        sg_bias = jnp.repeat(sg_b[l].T, CHUNK, axis=1)
        n_tiles = n_lat_tiles if last else N_ALL // TM
        x_mid = _out_proj(l, n_tiles, a_lat, a_ctx, u, gn, sg_w[l].astype(BF16), sg_bias,
                          w_out[l].astype(BF16), vec(post_mix_g), mod, xa, xb, xb_tile0)
        if l % 2 == 0:
            i = l // 2
            x_all = _ffn(l, x_mid, mod, vec(pre_ffn_g), ffn_w1[i].astype(BF16), ffn_w3[i].astype(BF16),
                         ffn_w2[i].astype(BF16), vec(post_ffn_g))
        else:
            i = l // 2
            ids, rank, gate, cnt = _router(l, x_mid, mod, vec(pre_ffn_g), router_w[i])
            dest, texp, tvalid, pad_lo, pad_hi = _routing_tables(ids, rank, cnt)
            xs = _dispatch(l, x_mid, mod, vec(pre_ffn_g), dest, pad_lo, pad_hi)
            ys = _experts(l, xs, texp, tvalid, moe_w1[i].astype(BF16), moe_w3[i].astype(BF16),
                          moe_w2[i].astype(BF16))
            x_all = _combine(l, ys, dest, gate, x_mid, mod, vec(post_ffn_g))
    return x_all.reshape(BATCH, SEQ, D_MODEL)
```

```python
import functools
import math

import jax
import jax.numpy as jnp
from jax import lax
from jax.experimental import pallas as pl
from jax.experimental.pallas import tpu as pltpu

D_MODEL = 1024
BATCH = 16
SEQ = 2048
DEPTH = 2
GRID_W = 64
CTX_LEN = 256
DA_WIDTH = 512
DA_HEADS = 4
DA_VDIM = 128
DA_HEAD_DIM = 64
SG_WIDTH = 512
SG_GROUPS = 4
CHUNK = 128
ROPE_BASE = 10000.0
D_FF = 2816
N_EXPERTS = 8
D_EXPERT = 3584
EPS = 1e-6
IN_COLS = 3 * DA_WIDTH + 2 * SG_WIDTH

N_LAT = BATCH * SEQ
N_CTX = BATCH * CTX_LEN
N_ALL = N_LAT + N_CTX
MOD_ROWS = 32

TM = 512
TQ = 1024
TQ_SUB = 512
KEY_SEG = 2048
TF_DENSE = 1408
TF_MOE = 1792
XS_TILES = 2 * N_LAT // TM + N_EXPERTS
MIB = 2 ** 20

BF16 = jnp.bfloat16
F32 = jnp.float32


def _rms(x):
    return x * lax.rsqrt(jnp.mean(x * x, axis=-1, keepdims=True) + EPS)


def _silu(x):
    return x * jax.nn.sigmoid(x)


def _resident(block_shape, index_map):
    return pl.BlockSpec(block_shape, index_map, pipeline_mode=pl.Buffered(1))


def _params(semantics, vmem_mib):
    return pltpu.CompilerParams(dimension_semantics=semantics, vmem_limit_bytes=vmem_mib * MIB)


def _mod_kernel(cc_ref, w_ref, b_ref, o_ref):
    a = _silu(cc_ref[...]).astype(BF16)
    o_ref[...] = jnp.dot(a, w_ref[...].astype(BF16), preferred_element_type=F32) + b_ref[...]


def _modulation(cc, ada_w, ada_b):
    tn = 1536
    out = pl.pallas_call(
        _mod_kernel,
        grid=(DEPTH, 6 * D_MODEL // tn),
        in_specs=[
            pl.BlockSpec((MOD_ROWS, D_MODEL), lambda l, j: (0, 0)),
            pl.BlockSpec((None, D_MODEL, tn), lambda l, j: (l, 0, j)),
            pl.BlockSpec((None, 1, tn), lambda l, j: (l, 0, j)),
        ],
        out_specs=pl.BlockSpec((None, MOD_ROWS, tn), lambda l, j: (l, 0, j)),
        out_shape=jax.ShapeDtypeStruct((DEPTH, MOD_ROWS, 6 * D_MODEL), F32),
        compiler_params=_params(("parallel", "parallel"), 40),
        name="modulation",
    )(cc, ada_w, ada_b.reshape(DEPTH, 1, 6 * D_MODEL))
    return out.reshape(DEPTH, MOD_ROWS, 6, D_MODEL)


def _mod_spec(layer):
    tiles_per_batch = SEQ // TM
    return pl.BlockSpec((None, None, 6, D_MODEL),
                        lambda i: (layer, jnp.minimum(i // tiles_per_batch, BATCH), 0, 0))


def _in_proj_kernel(n_lat_tiles, xa_ref, xb_ref, mod_ref, g_ref, w_ref, cos_ref, sa_ref, sb_ref,
                    lng_ref, lnb_ref, q_ref, kt_ref, v_ref, u_ref, gn_ref):
    i = pl.program_id(0)
    x = jnp.where(i < n_lat_tiles, xa_ref[...], xb_ref[...])
    h = _rms(x) * g_ref[...]
    h = h * (1.0 + mod_ref[1:2, :]) + mod_ref[0:1, :]
    p = jnp.dot(h.astype(BF16), w_ref[...], preferred_element_type=F32)

    cos, sa, sb = cos_ref[...], sa_ref[...], sb_ref[...]

    def rope(t):
        return t * cos + pltpu.roll(t, 112, 1) * sa + pltpu.roll(t, 16, 1) * sb

    for hd in range(DA_HEADS):
        lo, hi = hd * DA_VDIM, (hd + 1) * DA_VDIM
        q_ref[:, lo:hi] = (rope(p[:, lo:hi]) * (DA_HEAD_DIM ** -0.5)).astype(BF16)
        kt_ref[lo:hi, :] = rope(p[:, DA_WIDTH + lo:DA_WIDTH + hi]).T.astype(BF16)
    v_ref[...] = p[:, 2 * DA_WIDTH:3 * DA_WIDTH].astype(BF16)
    u_ref[...] = p[:, 3 * DA_WIDTH:3 * DA_WIDTH + SG_WIDTH].astype(BF16)
    gv = p[:, 3 * DA_WIDTH + SG_WIDTH:]
    mu = jnp.mean(gv, axis=-1, keepdims=True)
    var = jnp.mean(jnp.square(gv - mu), axis=-1, keepdims=True)
    gn_ref[...] = ((gv - mu) * lax.rsqrt(var + EPS) * lng_ref[...] + lnb_ref[...]).astype(BF16)


def _in_proj(layer, xa, xb, xb_tile0, mod, pre_g, w_in, rope_tabs, ln_g, ln_b):
    n_lat_tiles = N_LAT // TM
    n_tiles = N_ALL // TM
    tiles_per_batch = SEQ // TM
    tab_spec = pl.BlockSpec(
        (TM, DA_VDIM), lambda i: (jnp.where(i < n_lat_tiles, i % tiles_per_batch, tiles_per_batch), 0))
    row = lambda i: (i, 0)
    outs = pl.pallas_call(
        functools.partial(_in_proj_kernel, n_lat_tiles),
        grid=(n_tiles,),
        in_specs=[
            pl.BlockSpec((TM, D_MODEL), lambda i: (jnp.minimum(i, n_lat_tiles - 1), 0)),
            pl.BlockSpec((TM, D_MODEL), lambda i: (xb_tile0 + jnp.maximum(i - n_lat_tiles, 0), 0)),
            _mod_spec(layer),
            _resident((1, D_MODEL), lambda i: (0, 0)),
            _resident((D_MODEL, IN_COLS), lambda i: (0, 0)),
            tab_spec, tab_spec, tab_spec,
            _resident((1, SG_WIDTH), lambda i: (0, 0)),
            _resident((1, SG_WIDTH), lambda i: (0, 0)),
        ],
        out_specs=[
            pl.BlockSpec((TM, DA_WIDTH), row),
            pl.BlockSpec((DA_WIDTH, TM), lambda i: (0, i)),
            pl.BlockSpec((TM, DA_WIDTH), row),
            pl.BlockSpec((TM, SG_WIDTH), row),
            pl.BlockSpec((TM, SG_WIDTH), row),
        ],
        out_shape=[
            jax.ShapeDtypeStruct((N_ALL, DA_WIDTH), BF16),
            jax.ShapeDtypeStruct((DA_WIDTH, N_ALL), BF16),
            jax.ShapeDtypeStruct((N_ALL, DA_WIDTH), BF16),
            jax.ShapeDtypeStruct((N_ALL, SG_WIDTH), BF16),
            jax.ShapeDtypeStruct((N_ALL, SG_WIDTH), BF16),
        ],
        compiler_params=_params(("parallel",), 48),
        name=f"in_proj_l{layer}",
    )(xa, xb, mod, pre_g, w_in, *rope_tabs, ln_g, ln_b)
    return outs


def _rope_tables():
    pos = jnp.arange(SEQ, dtype=jnp.int32)
    row = (pos // GRID_W).astype(F32)
    col = (pos % GRID_W).astype(F32)
    half = DA_HEAD_DIM // 2
    inv = ROPE_BASE ** (-jnp.arange(0, half, 2, dtype=F32) / half)
    ang_r = row[:, None] * inv
    ang_c = col[:, None] * inv
    ang = jnp.concatenate([ang_r, ang_r, ang_c, ang_c], axis=-1)
    cos = jnp.tile(jnp.cos(ang), (1, 2))
    sin = jnp.tile(jnp.sin(ang), (1, 2))
    first = (jnp.arange(DA_VDIM) % 32) < 16
    sa = jnp.where(first, -sin, 0.0)
    sb = jnp.where(first, 0.0, sin)
    ident = jnp.zeros((TM, DA_VDIM), F32)
    return (jnp.concatenate([cos, ident + 1.0], axis=0),
            jnp.concatenate([sa, ident], axis=0),
            jnp.concatenate([sb, ident], axis=0))


def _attn_kernel(n_src, lambda_init, q_ref, *refs):
    kt_refs = refs[:n_src]
    v_refs = refs[n_src:2 * n_src]
    lq1_ref, lk1_ref, lq2_ref, lk2_ref, subg_ref = refs[2 * n_src:2 * n_src + 5]
    o_ref = refs[-1 - n_src]
    vext_refs = refs[-n_src:]

    @pl.when(pl.program_id(2) == 0)
    def _():
        for v_ref, ve_ref in zip(v_refs, vext_refs):
            ve_ref[:, :DA_VDIM] = v_ref[...]
            ve_ref[:, DA_VDIM:] = jnp.ones((v_ref.shape[0], DA_VDIM), BF16)

    segs = []
    for kt_ref, ve_ref in zip(kt_refs, vext_refs):
        n_keys = ve_ref.shape[0]
        for lo in range(0, n_keys, KEY_SEG):
            segs.append((kt_ref, ve_ref, lo, min(lo + KEY_SEG, n_keys)))

    lam = (jnp.exp(jnp.sum(lq1_ref[...] * lk1_ref[...], axis=-1, keepdims=True))
           - jnp.exp(jnp.sum(lq2_ref[...] * lk2_ref[...], axis=-1, keepdims=True)) + lambda_init)
    tq_sub = min(TQ_SUB, q_ref.shape[0])

    def scores(r0):
        q = q_ref[r0:r0 + tq_sub, :]
        first = lax.broadcasted_iota(jnp.int32, q.shape, 1) < DA_HEAD_DIM
        zero = jnp.zeros_like(q)
        qms = (jnp.where(first, q, zero), jnp.where(first, zero, q))
        return [[jnp.dot(qm, kt_ref[:, lo:hi], preferred_element_type=F32) for kt_ref, _, lo, hi in segs]
                for qm in qms]

    def finish(r0, ss_maps):
        maps = []
        for ss in ss_maps:
            m = functools.reduce(jnp.maximum, [jnp.max(s, axis=-1, keepdims=True) for s in ss])
            ts = [jnp.dot(jnp.exp(s - m).astype(BF16), ve_ref[lo:hi, :], preferred_element_type=F32)
                  for s, (_, ve_ref, lo, hi) in zip(ss, segs)]
            acc = functools.reduce(jnp.add, ts)
            maps.append(acc[:, :DA_VDIM] / acc[:, DA_VDIM:])
        o = maps[0] - lam * maps[1]
        o_ref[r0:r0 + tq_sub, :] = (_rms(o) * subg_ref[...] * (1.0 - lambda_init)).astype(BF16)

    starts = list(range(0, q_ref.shape[0], tq_sub))
    pending = scores(starts[0])
    for idx, r0 in enumerate(starts):
        nxt = scores(starts[idx + 1]) if idx + 1 < len(starts) else None
        finish(r0, pending)
        pending = nxt


def _attention(layer, q, kt, v, lam_params, subg, latent):
    lambda_init = 0.8 - 0.6 * math.exp(-0.3 * layer)
    ctx_blk0 = N_LAT // CTX_LEN
    small = lambda: pl.BlockSpec((1, DA_HEAD_DIM), lambda b, h, t: (0, 0))
    tail_specs = [small(), small(), small(), small(), pl.BlockSpec((1, DA_VDIM), lambda b, h, t: (0, 0))]
    kt_ctx = pl.BlockSpec((DA_VDIM, CTX_LEN), lambda b, h, t: (h, ctx_blk0 + b))
    v_ctx = pl.BlockSpec((CTX_LEN, DA_VDIM), lambda b, h, t: (ctx_blk0 + b, h))
    if latent:
        tq, n_q, n_rows = TQ, SEQ // TQ, N_LAT
        q_map = lambda b, h, t: (b * n_q + t, h)
        o_map = q_map
        kt_specs = [kt_ctx, pl.BlockSpec((DA_VDIM, SEQ), lambda b, h, t: (h, b))]
        v_specs = [v_ctx, pl.BlockSpec((SEQ, DA_VDIM), lambda b, h, t: (b, h))]
        key_lens = [CTX_LEN, SEQ]
    else:
        tq, n_q, n_rows = CTX_LEN, 1, N_CTX
        q_map = lambda b, h, t: (ctx_blk0 + b, h)
        o_map = lambda b, h, t: (b, h)
        kt_specs, v_specs, key_lens = [kt_ctx], [v_ctx], [CTX_LEN]
    n_src = len(key_lens)
    return pl.pallas_call(
        functools.partial(_attn_kernel, n_src, lambda_init),
        grid=(BATCH, DA_HEADS, n_q),
        in_specs=[pl.BlockSpec((tq, DA_VDIM), q_map)] + kt_specs + v_specs + tail_specs,
        out_specs=pl.BlockSpec((tq, DA_VDIM), o_map),
        out_shape=jax.ShapeDtypeStruct((n_rows, DA_WIDTH), BF16),
        scratch_shapes=[pltpu.VMEM((n, 2 * DA_VDIM), BF16) for n in key_lens],
        compiler_params=_params(("parallel", "parallel", "arbitrary"), 48),
        name=f"attn_l{layer}_{'lat' if latent else 'ctx'}",
    )(q, *([kt] * n_src), *([v] * n_src), *lam_params, subg)


def _out_proj_kernel(n_lat_tiles, aa_ref, ab_ref, u_ref, gn_ref, sgw_ref, sgb_ref, w_ref, pg_ref, mod_ref,
                     xa_ref, xb_ref, o_ref, s_ref):
    i = pl.program_id(0)
    a = jnp.where(i < n_lat_tiles, aa_ref[...], ab_ref[...])
    for c in range(TM // CHUNK):
        rows = slice(c * CHUNK, (c + 1) * CHUNK)
        for g in range(SG_GROUPS):
            cols = slice(g * CHUNK, (g + 1) * CHUNK)
            mixed = jnp.dot(sgw_ref[g], gn_ref[rows, cols], preferred_element_type=F32) + sgb_ref[:, cols]
            s_ref[rows, cols] = (u_ref[rows, cols].astype(F32) * mixed).astype(BF16)
    o = (jnp.dot(a, w_ref[:DA_WIDTH, :], preferred_element_type=F32)
         + jnp.dot(s_ref[...], w_ref[DA_WIDTH:, :], preferred_element_type=F32))
    x = jnp.where(i < n_lat_tiles, xa_ref[...], xb_ref[...])
    o_ref[...] = x + mod_ref[2:3, :] * (_rms(o) * pg_ref[...])


def _out_proj(layer, n_tiles, a_lat, a_ctx, u, gn, sg_w, sg_bias, w_out, post_g, mod, xa, xb, xb_tile0):
    n_lat_tiles = N_LAT // TM
    row = lambda i: (i, 0)
    half = lambda: pl.BlockSpec((TM, DA_WIDTH), row)
    return pl.pallas_call(
        functools.partial(_out_proj_kernel, n_lat_tiles),
        grid=(n_tiles,),
        in_specs=[
            pl.BlockSpec((TM, DA_WIDTH), lambda i: (jnp.minimum(i, n_lat_tiles - 1), 0)),
            pl.BlockSpec((TM, DA_WIDTH), lambda i: (jnp.maximum(i - n_lat_tiles, 0), 0)),
            half(), half(),
            _resident((SG_GROUPS, CHUNK, CHUNK), lambda i: (0, 0, 0)),
            _resident((CHUNK, SG_WIDTH), lambda i: (0, 0)),
            _resident((D_MODEL, D_MODEL), lambda i: (0, 0)),
            _resident((1, D_MODEL), lambda i: (0, 0)),
            _mod_spec(layer),
            pl.BlockSpec((TM, D_MODEL), lambda i: (jnp.minimum(i, n_lat_tiles - 1), 0)),
            pl.BlockSpec((TM, D_MODEL), lambda i: (xb_tile0 + jnp.maximum(i - n_lat_tiles, 0), 0)),
        ],
        out_specs=pl.BlockSpec((TM, D_MODEL), row),
        out_shape=jax.ShapeDtypeStruct((n_tiles * TM, D_MODEL), F32),
        scratch_shapes=[pltpu.VMEM((TM, SG_WIDTH), BF16)],
        compiler_params=_params(("parallel",), 48),
        name=f"out_proj_l{layer}",
    )(a_lat, a_ctx, u, gn, sg_w, sg_bias, w_out, post_g, mod, xa, xb)


def _ffn_kernel(x_ref, mod_ref, g_ref, w1_ref, w3_ref, w2_ref, pg_ref, o_ref):
    x = x_ref[...]
    f = ((_rms(x) * g_ref[...]) * (1.0 + mod_ref[4:5, :]) + mod_ref[3:4, :]).astype(BF16)
    y = None
    for k in range(D_FF // TF_DENSE):
        cols = slice(k * TF_DENSE, (k + 1) * TF_DENSE)
        h1 = jnp.dot(f, w1_ref[:, cols], preferred_element_type=F32)
        h3 = jnp.dot(f, w3_ref[:, cols], preferred_element_type=F32)
        t = jnp.dot((_silu(h1) * h3).astype(BF16), w2_ref[cols, :], preferred_element_type=F32)
        y = t if y is None else y + t
    o_ref[...] = x + mod_ref[5:6, :] * (_rms(y) * pg_ref[...])


def _ffn(layer, x_all, mod, pre_g, w1, w3, w2, post_g):
    row = lambda i: (i, 0)
    return pl.pallas_call(
        _ffn_kernel,
        grid=(N_ALL // TM,),
        in_specs=[
            pl.BlockSpec((TM, D_MODEL), row),
            _mod_spec(layer),
            _resident((1, D_MODEL), lambda i: (0, 0)),
            _resident((D_MODEL, D_FF), lambda i: (0, 0)),
            _resident((D_MODEL, D_FF), lambda i: (0, 0)),
            _resident((D_FF, D_MODEL), lambda i: (0, 0)),
            _resident((1, D_MODEL), lambda i: (0, 0)),
        ],
        out_specs=pl.BlockSpec((TM, D_MODEL), row),
        out_shape=jax.ShapeDtypeStruct((N_ALL, D_MODEL), F32),
        compiler_params=_params(("parallel",), 56),
        name=f"ffn_l{layer}",
    )(x_all, mod, pre_g, w1, w3, w2, post_g)


def _ffn_input(x, mod_ref, g_ref):
    return (_rms(x) * g_ref[...]) * (1.0 + mod_ref[4:5, :]) + mod_ref[3:4, :]


def _router_kernel(x_ref, mod_ref, g_ref, rw_ref, ids_ref, rank_ref, gate_ref, cnt_ref, carry_ref):
    @pl.when(pl.program_id(0) == 0)
    def _():
        carry_ref[...] = jnp.zeros_like(carry_ref)

    f = _ffn_input(x_ref[...], mod_ref, g_ref)
    rw = rw_ref[...]
    f_hi, rw_hi = f.astype(BF16), rw.astype(BF16)
    f_lo, rw_lo = (f - f_hi.astype(F32)).astype(BF16), (rw - rw_hi.astype(F32)).astype(BF16)
    logits = (jnp.dot(f_hi, rw_hi, preferred_element_type=F32) + jnp.dot(f_lo, rw_hi, preferred_element_type=F32)
              + jnp.dot(f_hi, rw_lo, preferred_element_type=F32))
    lane = lax.broadcasted_iota(jnp.int32, logits.shape, 1)
    v1 = jnp.max(logits, axis=-1, keepdims=True)
    i1 = jnp.min(jnp.where(logits == v1, lane, N_EXPERTS), axis=-1, keepdims=True)
    rest = jnp.where(lane == i1, -jnp.inf, logits)
    v2 = jnp.max(rest, axis=-1, keepdims=True)
    i2 = jnp.min(jnp.where(rest == v2, lane, N_EXPERTS), axis=-1, keepdims=True)
    e = jnp.exp(v2 - v1)
    w1 = 1.0 / (1.0 + e)

    sel1, sel2 = lane == i1, lane == i2
    onehot = jnp.where(sel1 | sel2, 1.0, 0.0)
    r = lax.broadcasted_iota(jnp.int32, (TM, TM), 0)
    c = lax.broadcasted_iota(jnp.int32, (TM, TM), 1)
    tri = jnp.where(c < r, 1.0, 0.0).astype(BF16)
    prefix = jnp.dot(tri, onehot.astype(BF16), preferred_element_type=F32) + carry_ref[...]
    rank1 = jnp.sum(jnp.where(sel1, prefix, 0.0), axis=-1, keepdims=True)
    rank2 = jnp.sum(jnp.where(sel2, prefix, 0.0), axis=-1, keepdims=True)
    carry_ref[...] += jnp.sum(onehot, axis=0, keepdims=True)

    slot0 = lax.broadcasted_iota(jnp.int32, (TM, 2), 1) == 0
    ids_ref[...] = jnp.where(slot0, i1, i2)
    rank_ref[...] = jnp.where(slot0, rank1, rank2).astype(jnp.int32)
    gate_ref[...] = jnp.where(slot0, w1, e * w1)
    cnt_ref[...] = carry_ref[...].astype(jnp.int32)


def _router(layer, x_lat, mod, pre_g, router_w):
    row = lambda i: (i, 0)
    pair = lambda: pl.BlockSpec((TM, 2), row)
    return pl.pallas_call(
        _router_kernel,
        grid=(N_LAT // TM,),
        in_specs=[
            pl.BlockSpec((TM, D_MODEL), row),
            _mod_spec(layer),
            _resident((1, D_MODEL), lambda i: (0, 0)),
            _resident((D_MODEL, N_EXPERTS), lambda i: (0, 0)),
        ],
        out_specs=[pair(), pair(), pair(), pl.BlockSpec((1, N_EXPERTS), lambda i: (0, 0))],
        out_shape=[jax.ShapeDtypeStruct((N_LAT, 2), jnp.int32),
                   jax.ShapeDtypeStruct((N_LAT, 2), jnp.int32),
                   jax.ShapeDtypeStruct((N_LAT, 2), F32),
                   jax.ShapeDtypeStruct((1, N_EXPERTS), jnp.int32)],
        scratch_shapes=[pltpu.VMEM((1, N_EXPERTS), F32)],
        compiler_params=_params(("arbitrary",), 32),
        name=f"router_l{layer}",
    )(x_lat, mod, pre_g, router_w)


def _routing_tables(ids, rank, cnt):
    cnt = cnt.reshape(N_EXPERTS)
    tiles = (cnt + TM - 1) // TM
    tile_end = jnp.cumsum(tiles)
    off = (tile_end - tiles) * TM
    dest = (off[ids] + rank).reshape(N_LAT // TM, 1, 2 * TM)
    j = jnp.arange(XS_TILES, dtype=jnp.int32)
    n_used = tile_end[-1]
    valid = (j < n_used).astype(jnp.int32)
    jc = jnp.minimum(j, n_used - 1)
    texp = jnp.minimum(jnp.sum(jc[:, None] >= tile_end[None, :], axis=1), N_EXPERTS - 1).astype(jnp.int32)
    pad_lo = (off + cnt).astype(jnp.int32)
    pad_hi = (tile_end * TM).astype(jnp.int32).at[N_EXPERTS - 1].set(XS_TILES * TM)
    return dest.astype(jnp.int32), texp, valid, pad_lo, pad_hi


def _row_copy(src_ref, src_row, dst_ref, dst_row, sem):
    return pltpu.make_async_copy(src_ref.at[pl.ds(src_row, 1)], dst_ref.at[pl.ds(dst_row, 1)], sem)


def _dispatch_kernel(pad_lo_ref, pad_hi_ref, dest_ref, x_ref, mod_ref, g_ref, xs_ref, f_ref, z_ref, sems, zsem):
    i = pl.program_id(0)
    n = pl.num_programs(0)

    @pl.when(i == 0)
    def _():
        z_ref[...] = jnp.zeros_like(z_ref)
        for e in range(N_EXPERTS):
            lo, hi = pad_lo_ref[e], pad_hi_ref[e]
            lax.fori_loop(lo, hi, lambda r, _: _row_copy(z_ref, 0, xs_ref, r, zsem).start(), None)
            lax.fori_loop(lo, hi, lambda r, _: _row_copy(z_ref, 0, xs_ref, r, zsem).wait(), None)

    def drain(slot):
        for _ in range(2):
            pltpu.make_async_copy(f_ref.at[slot], xs_ref.at[pl.ds(0, TM)], sems.at[slot]).wait()

    def step(slot):
        @pl.when(i >= 2)
        def _():
            drain(slot)

        f_ref[slot] = _ffn_input(x_ref[...], mod_ref, g_ref)
        for t in range(TM):
            _row_copy(f_ref.at[slot], t, xs_ref, dest_ref[0, 2 * t], sems.at[slot]).start()
            _row_copy(f_ref.at[slot], t, xs_ref, dest_ref[0, 2 * t + 1], sems.at[slot]).start()

        @pl.when(i == n - 1)
        def _():
            drain(1 - slot)
            drain(slot)

    for slot in range(2):
        pl.when(i % 2 == slot)(functools.partial(step, slot))


def _dispatch(layer, x_lat, mod, pre_g, dest, pad_lo, pad_hi):
    smem = lambda: pl.BlockSpec(memory_space=pltpu.SMEM)
    return pl.pallas_call(
        _dispatch_kernel,
        grid=(N_LAT // TM,),
        in_specs=[
            smem(), smem(),
            pl.BlockSpec((None, 1, 2 * TM), lambda i: (i, 0, 0), memory_space=pltpu.SMEM),
            pl.BlockSpec((TM, D_MODEL), lambda i: (i, 0)),
            _mod_spec(layer),
            _resident((1, D_MODEL), lambda i: (0, 0)),
        ],
        out_specs=pl.BlockSpec(memory_space=pl.ANY),
        out_shape=jax.ShapeDtypeStruct((XS_TILES * TM, D_MODEL), F32),
        scratch_shapes=[pltpu.VMEM((2, TM, D_MODEL), F32), pltpu.VMEM((8, D_MODEL), F32),
                        pltpu.SemaphoreType.DMA((2,)), pltpu.SemaphoreType.DMA(())],
        compiler_params=_params(("arbitrary",), 32),
        name=f"moe_dispatch_l{layer}",
    )(pad_lo, pad_hi, dest, x_lat, mod, pre_g)


def _expert_kernel(texp_ref, tvalid_ref, xs_ref, w1_ref, w3_ref, w2_ref, ys_ref, xb_ref, acc_ref):
    j = pl.program_id(0)
    k = pl.program_id(1)

    @pl.when((tvalid_ref[j] == 0) & (k == 0))
    def _():
        ys_ref[...] = jnp.zeros_like(ys_ref)

    @pl.when(tvalid_ref[j] == 1)
    def _():
        @pl.when(k == 0)
        def _():
            xb_ref[...] = xs_ref[...].astype(BF16)
            acc_ref[...] = jnp.zeros_like(acc_ref)

        xb = xb_ref[...]
        h1 = jnp.dot(xb, w1_ref[...], preferred_element_type=F32)
        h3 = jnp.dot(xb, w3_ref[...], preferred_element_type=F32)
        acc_ref[...] += jnp.dot((_silu(h1) * h3).astype(BF16), w2_ref[...], preferred_element_type=F32)

        @pl.when(k == pl.num_programs(1) - 1)
        def _():
            ys_ref[...] = acc_ref[...]


def _experts(layer, xs, texp, tvalid, w1, w3, w2):
    n_k = D_EXPERT // TF_MOE
    row = lambda j, k, texp, tvalid: (j, 0)
    kk = lambda j, k, tvalid: jnp.where(tvalid[j] == 1, k, n_k - 1)
    return pl.pallas_call(
        _expert_kernel,
        grid_spec=pltpu.PrefetchScalarGridSpec(
            num_scalar_prefetch=2,
            grid=(XS_TILES, n_k),
            in_specs=[
                pl.BlockSpec((TM, D_MODEL), row),
                pl.BlockSpec((None, D_MODEL, TF_MOE), lambda j, k, texp, tvalid: (texp[j], 0, kk(j, k, tvalid))),
                pl.BlockSpec((None, D_MODEL, TF_MOE), lambda j, k, texp, tvalid: (texp[j], 0, kk(j, k, tvalid))),
                pl.BlockSpec((None, TF_MOE, D_MODEL), lambda j, k, texp, tvalid: (texp[j], kk(j, k, tvalid), 0)),
            ],
            out_specs=pl.BlockSpec((TM, D_MODEL), row),
            scratch_shapes=[pltpu.VMEM((TM, D_MODEL), BF16), pltpu.VMEM((TM, D_MODEL), F32)],
        ),
        out_shape=jax.ShapeDtypeStruct((XS_TILES * TM, D_MODEL), F32),
        compiler_params=_params(("arbitrary", "arbitrary"), 56),
        name=f"moe_experts_l{layer}",
    )(texp, tvalid, xs, w1, w3, w2)


def _combine_kernel(dest_ref, dest_next_ref, ys_ref, gate_ref, x_ref, mod_ref, pg_ref, o_ref, buf_ref, sems):
    i = pl.program_id(0)
    n = pl.num_programs(0)

    def gather(d_ref, slot):
        for t in range(TM):
            _row_copy(ys_ref, d_ref[0, 2 * t], buf_ref.at[slot, 0], t, sems.at[slot]).start()
            _row_copy(ys_ref, d_ref[0, 2 * t + 1], buf_ref.at[slot, 1], t, sems.at[slot]).start()

    def step(slot):
        @pl.when(i == 0)
        def _():
            gather(dest_ref, slot)

        @pl.when(i + 1 < n)
        def _():
            gather(dest_next_ref, 1 - slot)

        for s in range(2):
            pltpu.make_async_copy(ys_ref.at[pl.ds(0, TM)], buf_ref.at[slot, s], sems.at[slot]).wait()
        y = gate_ref[:, 0:1] * buf_ref[slot, 0] + gate_ref[:, 1:2] * buf_ref[slot, 1]
        o_ref[...] = x_ref[...] + mod_ref[5:6, :] * (_rms(y) * pg_ref[...])

    for slot in range(2):
        pl.when(i % 2 == slot)(functools.partial(step, slot))


def _combine(layer, ys, dest, gate, x_lat, mod, post_g):
    row = lambda i: (i, 0)
    return pl.pallas_call(
        _combine_kernel,
        grid=(N_LAT // TM,),
        in_specs=[
            pl.BlockSpec((None, 1, 2 * TM), lambda i: (i, 0, 0), memory_space=pltpu.SMEM),
            pl.BlockSpec((None, 1, 2 * TM), lambda i: (jnp.minimum(i + 1, N_LAT // TM - 1), 0, 0),
                         memory_space=pltpu.SMEM),
            pl.BlockSpec(memory_space=pl.ANY),
            pl.BlockSpec((TM, 2), row),
            pl.BlockSpec((TM, D_MODEL), row),
            _mod_spec(layer),
            _resident((1, D_MODEL), lambda i: (0, 0)),
        ],
        out_specs=pl.BlockSpec((TM, D_MODEL), row),
        out_shape=jax.ShapeDtypeStruct((N_LAT, D_MODEL), F32),
        scratch_shapes=[pltpu.VMEM((2, 2, TM, D_MODEL), F32), pltpu.SemaphoreType.DMA((2,))],
        compiler_params=_params(("arbitrary",), 40),
        name=f"moe_combine_l{layer}",
    )(dest, dest, ys, gate, x_lat, mod, post_g)


def kernel(x, c, ctx, c_ctx, ada_w, ada_b, pre_mix_g, post_mix_g, pre_ffn_g, post_ffn_g, w_in, w_out,
           lam_q1, lam_k1, lam_q2, lam_k2, subln_g, sg_ln_g, sg_ln_b, sg_w, sg_b,
           ffn_w1, ffn_w3, ffn_w2, router_w, moe_w1, moe_w3, moe_w2):
    x_lat = x.reshape(N_LAT, D_MODEL)
    x_ctx = ctx.reshape(N_CTX, D_MODEL)
    cc = jnp.concatenate([c, c_ctx[None, :], jnp.zeros((MOD_ROWS - BATCH - 1, D_MODEL), F32)], axis=0)
    mod = _modulation(cc, ada_w, ada_b)
    rope_tabs = _rope_tables()
    n_lat_tiles = N_LAT // TM
    x_all = None
    for l in range(DEPTH):
        last = l == DEPTH - 1
        vec = lambda a: a[l][None, :]
        if l == 0:
            xa, xb, xb_tile0 = x_lat, x_ctx, 0
        else:
            xa, xb, xb_tile0 = x_all, x_all, n_lat_tiles
        q, kt, v, u, gn = _in_proj(l, xa, xb, xb_tile0, mod, vec(pre_mix_g), w_in[l].astype(BF16),
                                   rope_tabs, vec(sg_ln_g), vec(sg_ln_b))
        lam_params = [vec(lam_q1), vec(lam_k1), vec(lam_q2), vec(lam_k2)]
        a_lat = _attention(l, q, kt, v, lam_params, vec(subln_g), latent=True)
        a_ctx = a_lat if last else _attention(l, q, kt, v, lam_params, vec(subln_g), latent=False)
        sg_bias = jnp.repeat(sg_b[l].T, CHUNK, axis=1)
        n_tiles = n_lat_tiles if last else N_ALL // TM
        x_mid = _out_proj(l, n_tiles, a_lat, a_ctx, u, gn, sg_w[l].astype(BF16), sg_bias,
                          w_out[l].astype(BF16), vec(post_mix_g), mod, xa, xb, xb_tile0)
        if l % 2 == 0:
            i = l // 2
            x_all = _ffn(l, x_mid, mod, vec(pre_ffn_g), ffn_w1[i].astype(BF16), ffn_w3[i].astype(BF16),
                         ffn_w2[i].astype(BF16), vec(post_ffn_g))
        else:
            i = l // 2
            ids, rank, gate, cnt = _router(l, x_mid, mod, vec(pre_ffn_g), router_w[i])
            dest, texp, tvalid, pad_lo, pad_hi = _routing_tables(ids, rank, cnt)
            xs = _dispatch(l, x_mid, mod, vec(pre_ffn_g), dest, pad_lo, pad_hi)
            ys = _experts(l, xs, texp, tvalid, moe_w1[i].astype(BF16), moe_w3[i].astype(BF16),
                          moe_w2[i].astype(BF16))
            x_all = _combine(l, ys, dest, gate, x_mid, mod, vec(post_ffn_g))
    return x_all.reshape(BATCH, SEQ, D_MODEL)
```

```python
import functools
import math

import jax
import jax.numpy as jnp
from jax import lax
from jax.experimental import pallas as pl
from jax.experimental.pallas import tpu as pltpu

D_MODEL = 1024
BATCH = 16
SEQ = 2048
DEPTH = 2
GRID_W = 64
CTX_LEN = 256
DA_WIDTH = 512
DA_HEADS = 4
DA_VDIM = 128
DA_HEAD_DIM = 64
SG_WIDTH = 512
SG_GROUPS = 4
CHUNK = 128
ROPE_BASE = 10000.0
D_FF = 2816
N_EXPERTS = 8
D_EXPERT = 3584
EPS = 1e-6
IN_COLS = 3 * DA_WIDTH + 2 * SG_WIDTH

N_LAT = BATCH * SEQ
N_CTX = BATCH * CTX_LEN
N_ALL = N_LAT + N_CTX
MOD_ROWS = 32

TM = 512
TQ = 1024
TQ_SUB = 512
KEY_SEG = 2048
TF_DENSE = 1408
TF_MOE = 1792
N_TOK_TILES = N_LAT // TM
ROW_ALIGN = 8
RUN_SIZES = (512, 256, 128, 64, 32, 16, 8)
Z_ROWS = 2 * TM + N_EXPERTS * ROW_ALIGN
D_XS = D_MODEL + 128
XS_TILES = (2 * N_LAT + N_TOK_TILES * N_EXPERTS * (ROW_ALIGN - 1)) // TM + 1 + N_EXPERTS
MIB = 2 ** 20

BF16 = jnp.bfloat16
F32 = jnp.float32


def _rms(x):
    return x * lax.rsqrt(jnp.mean(x * x, axis=-1, keepdims=True) + EPS)


def _silu(x):
    return x * jax.nn.sigmoid(x)


def _resident(block_shape, index_map):
    return pl.BlockSpec(block_shape, index_map, pipeline_mode=pl.Buffered(1))


def _params(semantics, vmem_mib):
    return pltpu.CompilerParams(dimension_semantics=semantics, vmem_limit_bytes=vmem_mib * MIB)


def _mod_kernel(cc_ref, w_ref, b_ref, o_ref):
    a = _silu(cc_ref[...]).astype(BF16)
    o_ref[...] = jnp.dot(a, w_ref[...].astype(BF16), preferred_element_type=F32) + b_ref[...]


def _modulation(cc, ada_w, ada_b):
    tn = 1536
    out = pl.pallas_call(
        _mod_kernel,
        grid=(DEPTH, 6 * D_MODEL // tn),
        in_specs=[
            pl.BlockSpec((MOD_ROWS, D_MODEL), lambda l, j: (0, 0)),
            pl.BlockSpec((None, D_MODEL, tn), lambda l, j: (l, 0, j)),
            pl.BlockSpec((None, 1, tn), lambda l, j: (l, 0, j)),
        ],
        out_specs=pl.BlockSpec((None, MOD_ROWS, tn), lambda l, j: (l, 0, j)),
        out_shape=jax.ShapeDtypeStruct((DEPTH, MOD_ROWS, 6 * D_MODEL), F32),
        compiler_params=_params(("parallel", "parallel"), 40),
        name="modulation",
    )(cc, ada_w, ada_b.reshape(DEPTH, 1, 6 * D_MODEL))
    return out.reshape(DEPTH, MOD_ROWS, 6, D_MODEL)


def _mod_spec(layer):
    tiles_per_batch = SEQ // TM
    return pl.BlockSpec((None, None, 6, D_MODEL),
                        lambda i: (layer, jnp.minimum(i // tiles_per_batch, BATCH), 0, 0))


def _in_proj_kernel(n_lat_tiles, xa_ref, xb_ref, mod_ref, g_ref, w_ref, cos_ref, sa_ref, sb_ref,
                    lng_ref, lnb_ref, q_ref, kt_ref, v_ref, u_ref, gn_ref):
    i = pl.program_id(0)
    x = jnp.where(i < n_lat_tiles, xa_ref[...], xb_ref[...])
    h = _rms(x) * g_ref[...]
    h = h * (1.0 + mod_ref[1:2, :]) + mod_ref[0:1, :]
    p = jnp.dot(h.astype(BF16), w_ref[...], preferred_element_type=F32)

    cos, sa, sb = cos_ref[...], sa_ref[...], sb_ref[...]

    def rope(t):
        return t * cos + pltpu.roll(t, 112, 1) * sa + pltpu.roll(t, 16, 1) * sb

    for hd in range(DA_HEADS):
        lo, hi = hd * DA_VDIM, (hd + 1) * DA_VDIM
        q_ref[:, lo:hi] = (rope(p[:, lo:hi]) * (DA_HEAD_DIM ** -0.5)).astype(BF16)
        kt_ref[lo:hi, :] = rope(p[:, DA_WIDTH + lo:DA_WIDTH + hi]).T.astype(BF16)
    v_ref[...] = p[:, 2 * DA_WIDTH:3 * DA_WIDTH].astype(BF16)
    u_ref[...] = p[:, 3 * DA_WIDTH:3 * DA_WIDTH + SG_WIDTH].astype(BF16)
    gv = p[:, 3 * DA_WIDTH + SG_WIDTH:]
    mu = jnp.mean(gv, axis=-1, keepdims=True)
    var = jnp.mean(jnp.square(gv - mu), axis=-1, keepdims=True)
    gn_ref[...] = ((gv - mu) * lax.rsqrt(var + EPS) * lng_ref[...] + lnb_ref[...]).astype(BF16)


def _in_proj(layer, xa, xb, xb_tile0, mod, pre_g, w_in, rope_tabs, ln_g, ln_b):
    n_lat_tiles = N_LAT // TM
    n_tiles = N_ALL // TM
    tiles_per_batch = SEQ // TM
    tab_spec = pl.BlockSpec(
        (TM, DA_VDIM), lambda i: (jnp.where(i < n_lat_tiles, i % tiles_per_batch, tiles_per_batch), 0))
    row = lambda i: (i, 0)
    outs = pl.pallas_call(
        functools.partial(_in_proj_kernel, n_lat_tiles),
        grid=(n_tiles,),
        in_specs=[
            pl.BlockSpec((TM, D_MODEL), lambda i: (jnp.minimum(i, n_lat_tiles - 1), 0)),
            pl.BlockSpec((TM, D_MODEL), lambda i: (xb_tile0 + jnp.maximum(i - n_lat_tiles, 0), 0)),
            _mod_spec(layer),
            _resident((1, D_MODEL), lambda i: (0, 0)),
            _resident((D_MODEL, IN_COLS), lambda i: (0, 0)),
            tab_spec, tab_spec, tab_spec,
            _resident((1, SG_WIDTH), lambda i: (0, 0)),
            _resident((1, SG_WIDTH), lambda i: (0, 0)),
        ],
        out_specs=[
            pl.BlockSpec((TM, DA_WIDTH), row),
            pl.BlockSpec((DA_WIDTH, TM), lambda i: (0, i)),
            pl.BlockSpec((TM, DA_WIDTH), row),
            pl.BlockSpec((TM, SG_WIDTH), row),
            pl.BlockSpec((TM, SG_WIDTH), row),
        ],
        out_shape=[
            jax.ShapeDtypeStruct((N_ALL, DA_WIDTH), BF16),
            jax.ShapeDtypeStruct((DA_WIDTH, N_ALL), BF16),
            jax.ShapeDtypeStruct((N_ALL, DA_WIDTH), BF16),
            jax.ShapeDtypeStruct((N_ALL, SG_WIDTH), BF16),
            jax.ShapeDtypeStruct((N_ALL, SG_WIDTH), BF16),
        ],
        compiler_params=_params(("parallel",), 48),
        name=f"in_proj_l{layer}",
    )(xa, xb, mod, pre_g, w_in, *rope_tabs, ln_g, ln_b)
    return outs


def _rope_tables():
    pos = jnp.arange(SEQ, dtype=jnp.int32)
    row = (pos // GRID_W).astype(F32)
    col = (pos % GRID_W).astype(F32)
    half = DA_HEAD_DIM // 2
    inv = ROPE_BASE ** (-jnp.arange(0, half, 2, dtype=F32) / half)
    ang_r = row[:, None] * inv
    ang_c = col[:, None] * inv
    ang = jnp.concatenate([ang_r, ang_r, ang_c, ang_c], axis=-1)
    cos = jnp.tile(jnp.cos(ang), (1, 2))
    sin = jnp.tile(jnp.sin(ang), (1, 2))
    first = (jnp.arange(DA_VDIM) % 32) < 16
    sa = jnp.where(first, -sin, 0.0)
    sb = jnp.where(first, 0.0, sin)
    ident = jnp.zeros((TM, DA_VDIM), F32)
    return (jnp.concatenate([cos, ident + 1.0], axis=0),
            jnp.concatenate([sa, ident], axis=0),
            jnp.concatenate([sb, ident], axis=0))


def _attn_kernel(n_src, lambda_init, q_ref, *refs):
    kt_refs = refs[:n_src]
    v_refs = refs[n_src:2 * n_src]
    lq1_ref, lk1_ref, lq2_ref, lk2_ref, subg_ref = refs[2 * n_src:2 * n_src + 5]
    o_ref = refs[-1 - n_src]
    vext_refs = refs[-n_src:]

    @pl.when(pl.program_id(2) == 0)
    def _():
        for v_ref, ve_ref in zip(v_refs, vext_refs):
            ve_ref[:, :DA_VDIM] = v_ref[...]
            ve_ref[:, DA_VDIM:] = jnp.ones((v_ref.shape[0], DA_VDIM), BF16)

    segs = []
    for kt_ref, ve_ref in zip(kt_refs, vext_refs):
        n_keys = ve_ref.shape[0]
        for lo in range(0, n_keys, KEY_SEG):
            segs.append((kt_ref, ve_ref, lo, min(lo + KEY_SEG, n_keys)))

    lam = (jnp.exp(jnp.sum(lq1_ref[...] * lk1_ref[...], axis=-1, keepdims=True))
           - jnp.exp(jnp.sum(lq2_ref[...] * lk2_ref[...], axis=-1, keepdims=True)) + lambda_init)
    tq_sub = min(TQ_SUB, q_ref.shape[0])

    def scores(r0):
        q = q_ref[r0:r0 + tq_sub, :]
        first = lax.broadcasted_iota(jnp.int32, q.shape, 1) < DA_HEAD_DIM
        zero = jnp.zeros_like(q)
        qms = (jnp.where(first, q, zero), jnp.where(first, zero, q))
        return [[jnp.dot(qm, kt_ref[:, lo:hi], preferred_element_type=F32) for kt_ref, _, lo, hi in segs]
                for qm in qms]

    def finish(r0, ss_maps):
        maps = []
        for ss in ss_maps:
            m = functools.reduce(jnp.maximum, [jnp.max(s, axis=-1, keepdims=True) for s in ss])
            ts = [jnp.dot(jnp.exp(s - m).astype(BF16), ve_ref[lo:hi, :], preferred_element_type=F32)
                  for s, (_, ve_ref, lo, hi) in zip(ss, segs)]
            acc = functools.reduce(jnp.add, ts)
            maps.append(acc[:, :DA_VDIM] / acc[:, DA_VDIM:])
        o = maps[0] - lam * maps[1]
        o_ref[r0:r0 + tq_sub, :] = (_rms(o) * subg_ref[...] * (1.0 - lambda_init)).astype(BF16)

    starts = list(range(0, q_ref.shape[0], tq_sub))
    pending = scores(starts[0])
    for idx, r0 in enumerate(starts):
        nxt = scores(starts[idx + 1]) if idx + 1 < len(starts) else None
        finish(r0, pending)
        pending = nxt


def _attention(layer, q, kt, v, lam_params, subg, latent):
    lambda_init = 0.8 - 0.6 * math.exp(-0.3 * layer)
    ctx_blk0 = N_LAT // CTX_LEN
    small = lambda: pl.BlockSpec((1, DA_HEAD_DIM), lambda b, h, t: (0, 0))
    tail_specs = [small(), small(), small(), small(), pl.BlockSpec((1, DA_VDIM), lambda b, h, t: (0, 0))]
    kt_ctx = pl.BlockSpec((DA_VDIM, CTX_LEN), lambda b, h, t: (h, ctx_blk0 + b))
    v_ctx = pl.BlockSpec((CTX_LEN, DA_VDIM), lambda b, h, t: (ctx_blk0 + b, h))
    if latent:
        tq, n_q, n_rows = TQ, SEQ // TQ, N_LAT
        q_map = lambda b, h, t: (b * n_q + t, h)
        o_map = q_map
        kt_specs = [kt_ctx, pl.BlockSpec((DA_VDIM, SEQ), lambda b, h, t: (h, b))]
        v_specs = [v_ctx, pl.BlockSpec((SEQ, DA_VDIM), lambda b, h, t: (b, h))]
        key_lens = [CTX_LEN, SEQ]
    else:
        tq, n_q, n_rows = CTX_LEN, 1, N_CTX
        q_map = lambda b, h, t: (ctx_blk0 + b, h)
        o_map = lambda b, h, t: (b, h)
        kt_specs, v_specs, key_lens = [kt_ctx], [v_ctx], [CTX_LEN]
    n_src = len(key_lens)
    return pl.pallas_call(
        functools.partial(_attn_kernel, n_src, lambda_init),
        grid=(BATCH, DA_HEADS, n_q),
        in_specs=[pl.BlockSpec((tq, DA_VDIM), q_map)] + kt_specs + v_specs + tail_specs,
        out_specs=pl.BlockSpec((tq, DA_VDIM), o_map),
        out_shape=jax.ShapeDtypeStruct((n_rows, DA_WIDTH), BF16),
        scratch_shapes=[pltpu.VMEM((n, 2 * DA_VDIM), BF16) for n in key_lens],
        compiler_params=_params(("parallel", "parallel", "arbitrary"), 48),
        name=f"attn_l{layer}_{'lat' if latent else 'ctx'}",
    )(q, *([kt] * n_src), *([v] * n_src), *lam_params, subg)


def _out_proj_kernel(n_lat_tiles, aa_ref, ab_ref, u_ref, gn_ref, sgw_ref, sgb_ref, w_ref, pg_ref, mod_ref,
                     xa_ref, xb_ref, o_ref, s_ref):
    i = pl.program_id(0)
    a = jnp.where(i < n_lat_tiles, aa_ref[...], ab_ref[...])
    for c in range(TM // CHUNK):
        rows = slice(c * CHUNK, (c + 1) * CHUNK)
        for g in range(SG_GROUPS):
            cols = slice(g * CHUNK, (g + 1) * CHUNK)
            mixed = jnp.dot(sgw_ref[g], gn_ref[rows, cols], preferred_element_type=F32) + sgb_ref[:, cols]
            s_ref[rows, cols] = (u_ref[rows, cols].astype(F32) * mixed).astype(BF16)
    o = (jnp.dot(a, w_ref[:DA_WIDTH, :], preferred_element_type=F32)
         + jnp.dot(s_ref[...], w_ref[DA_WIDTH:, :], preferred_element_type=F32))
    x = jnp.where(i < n_lat_tiles, xa_ref[...], xb_ref[...])
    o_ref[...] = x + mod_ref[2:3, :] * (_rms(o) * pg_ref[...])


def _out_proj(layer, n_tiles, a_lat, a_ctx, u, gn, sg_w, sg_bias, w_out, post_g, mod, xa, xb, xb_tile0):
    n_lat_tiles = N_LAT // TM
    row = lambda i: (i, 0)
    half = lambda: pl.BlockSpec((TM, DA_WIDTH), row)
    return pl.pallas_call(
        functools.partial(_out_proj_kernel, n_lat_tiles),
        grid=(n_tiles,),
        in_specs=[
            pl.BlockSpec((TM, DA_WIDTH), lambda i: (jnp.minimum(i, n_lat_tiles - 1), 0)),
            pl.BlockSpec((TM, DA_WIDTH), lambda i: (jnp.maximum(i - n_lat_tiles, 0), 0)),
            half(), half(),
            _resident((SG_GROUPS, CHUNK, CHUNK), lambda i: (0, 0, 0)),
            _resident((CHUNK, SG_WIDTH), lambda i: (0, 0)),
            _resident((D_MODEL, D_MODEL), lambda i: (0, 0)),
            _resident((1, D_MODEL), lambda i: (0, 0)),
            _mod_spec(layer),
            pl.BlockSpec((TM, D_MODEL), lambda i: (jnp.minimum(i, n_lat_tiles - 1), 0)),
            pl.BlockSpec((TM, D_MODEL), lambda i: (xb_tile0 + jnp.maximum(i - n_lat_tiles, 0), 0)),
        ],
        out_specs=pl.BlockSpec((TM, D_MODEL), row),
        out_shape=jax.ShapeDtypeStruct((n_tiles * TM, D_MODEL), F32),
        scratch_shapes=[pltpu.VMEM((TM, SG_WIDTH), BF16)],
        compiler_params=_params(("parallel",), 48),
        name=f"out_proj_l{layer}",
    )(a_lat, a_ctx, u, gn, sg_w, sg_bias, w_out, post_g, mod, xa, xb)


def _ffn_kernel(x_ref, mod_ref, g_ref, w1_ref, w3_ref, w2_ref, pg_ref, o_ref):
    x = x_ref[...]
    f = ((_rms(x) * g_ref[...]) * (1.0 + mod_ref[4:5, :]) + mod_ref[3:4, :]).astype(BF16)
    y = None
    for k in range(D_FF // TF_DENSE):
        cols = slice(k * TF_DENSE, (k + 1) * TF_DENSE)
        h1 = jnp.dot(f, w1_ref[:, cols], preferred_element_type=F32)
        h3 = jnp.dot(f, w3_ref[:, cols], preferred_element_type=F32)
        t = jnp.dot((_silu(h1) * h3).astype(BF16), w2_ref[cols, :], preferred_element_type=F32)
        y = t if y is None else y + t
    o_ref[...] = x + mod_ref[5:6, :] * (_rms(y) * pg_ref[...])


def _ffn(layer, x_all, mod, pre_g, w1, w3, w2, post_g):
    row = lambda i: (i, 0)
    return pl.pallas_call(
        _ffn_kernel,
        grid=(N_ALL // TM,),
        in_specs=[
            pl.BlockSpec((TM, D_MODEL), row),
            _mod_spec(layer),
            _resident((1, D_MODEL), lambda i: (0, 0)),
            _resident((D_MODEL, D_FF), lambda i: (0, 0)),
            _resident((D_MODEL, D_FF), lambda i: (0, 0)),
            _resident((D_FF, D_MODEL), lambda i: (0, 0)),
            _resident((1, D_MODEL), lambda i: (0, 0)),
        ],
        out_specs=pl.BlockSpec((TM, D_MODEL), row),
        out_shape=jax.ShapeDtypeStruct((N_ALL, D_MODEL), F32),
        compiler_params=_params(("parallel",), 56),
        name=f"ffn_l{layer}",
    )(x_all, mod, pre_g, w1, w3, w2, post_g)


def _ffn_input(x, mod_ref, g_ref):
    return (_rms(x) * g_ref[...]) * (1.0 + mod_ref[4:5, :]) + mod_ref[3:4, :]


def _router_kernel(x_ref, mod_ref, g_ref, rw_ref, ids_ref, rank_ref, gate_ref, cnt_ref):
    f = _ffn_input(x_ref[...], mod_ref, g_ref)
    rw = rw_ref[...]
    f_hi, rw_hi = f.astype(BF16), rw.astype(BF16)
    f_lo, rw_lo = (f - f_hi.astype(F32)).astype(BF16), (rw - rw_hi.astype(F32)).astype(BF16)
    logits = (jnp.dot(f_hi, rw_hi, preferred_element_type=F32) + jnp.dot(f_lo, rw_hi, preferred_element_type=F32)
              + jnp.dot(f_hi, rw_lo, preferred_element_type=F32))
    lane = lax.broadcasted_iota(jnp.int32, logits.shape, 1)
    v1 = jnp.max(logits, axis=-1, keepdims=True)
    i1 = jnp.min(jnp.where(logits == v1, lane, N_EXPERTS), axis=-1, keepdims=True)
    rest = jnp.where(lane == i1, -jnp.inf, logits)
    v2 = jnp.max(rest, axis=-1, keepdims=True)
    i2 = jnp.min(jnp.where(rest == v2, lane, N_EXPERTS), axis=-1, keepdims=True)
    e = jnp.exp(v2 - v1)
    w1 = 1.0 / (1.0 + e)

    sel1, sel2 = lane == i1, lane == i2
    onehot = jnp.where(sel1 | sel2, 1.0, 0.0)
    r = lax.broadcasted_iota(jnp.int32, (TM, TM), 0)
    c = lax.broadcasted_iota(jnp.int32, (TM, TM), 1)
    tri = jnp.where(c < r, 1.0, 0.0).astype(BF16)
    prefix = jnp.dot(tri, onehot.astype(BF16), preferred_element_type=F32)
    rank1 = jnp.sum(jnp.where(sel1, prefix, 0.0), axis=-1, keepdims=True)
    rank2 = jnp.sum(jnp.where(sel2, prefix, 0.0), axis=-1, keepdims=True)

    slot0 = lax.broadcasted_iota(jnp.int32, (TM, 2), 1) == 0
    ids_ref[...] = jnp.where(slot0, i1, i2)
    rank_ref[...] = jnp.where(slot0, rank1, rank2).astype(jnp.int32)
    gate_ref[...] = jnp.where(slot0, w1, e * w1)
    cnt_ref[...] = jnp.sum(onehot, axis=0, keepdims=True).astype(jnp.int32)


def _router(layer, x_lat, mod, pre_g, router_w):
    row = lambda i: (i, 0)
    pair = lambda: pl.BlockSpec((TM, 2), row)
    return pl.pallas_call(
        _router_kernel,
        grid=(N_TOK_TILES,),
        in_specs=[
            pl.BlockSpec((TM, D_MODEL), row),
            _mod_spec(layer),
            _resident((1, D_MODEL), lambda i: (0, 0)),
            _resident((D_MODEL, N_EXPERTS), lambda i: (0, 0)),
        ],
        out_specs=[pair(), pair(), pair(), pl.BlockSpec((None, 1, N_EXPERTS), lambda i: (i, 0, 0))],
        out_shape=[jax.ShapeDtypeStruct((N_LAT, 2), jnp.int32),
                   jax.ShapeDtypeStruct((N_LAT, 2), jnp.int32),
                   jax.ShapeDtypeStruct((N_LAT, 2), F32),
                   jax.ShapeDtypeStruct((N_TOK_TILES, 1, N_EXPERTS), jnp.int32)],
        compiler_params=_params(("parallel",), 32),
        name=f"router_l{layer}",
    )(x_lat, mod, pre_g, router_w)


def _routing_tables(ids, rank, gate, cnt):
    cnt = cnt.reshape(N_TOK_TILES, N_EXPERTS)
    run = (cnt + ROW_ALIGN - 1) // ROW_ALIGN * ROW_ALIGN
    zoff = jnp.cumsum(run, axis=1) - run
    grp = jnp.sum(run, axis=0)
    tiles = (grp + TM - 1) // TM
    tile_end = jnp.cumsum(tiles)
    off = (tile_end - tiles) * TM
    gstart = off[None, :] + jnp.cumsum(run, axis=0) - run
    runs = jnp.concatenate([run, zoff, gstart], axis=1).astype(jnp.int32).reshape(N_TOK_TILES, 1, 3 * N_EXPERTS)

    ids3 = ids.reshape(N_TOK_TILES, TM, 2)
    hit = ids3[..., None] == jnp.arange(N_EXPERTS, dtype=jnp.int32)
    lp = jnp.sum(jnp.where(hit, zoff[:, None, None, :], 0), axis=-1) + rank.reshape(N_TOK_TILES, TM, 2)
    lp = lp.astype(jnp.int32)
    lp_col = lp.reshape(N_LAT, 2)
    lp_row = jnp.transpose(lp, (0, 2, 1))
    gate_row = jnp.transpose(gate.reshape(N_TOK_TILES, TM, 2), (0, 2, 1))

    j = jnp.arange(XS_TILES, dtype=jnp.int32)
    n_used = tile_end[-1]
    valid = (j < n_used).astype(jnp.int32)
    jc = jnp.minimum(j, n_used - 1)
    texp = jnp.minimum(jnp.sum(jc[:, None] >= tile_end[None, :], axis=1), N_EXPERTS - 1).astype(jnp.int32)
    pad_lo = (off + grp).astype(jnp.int32)
    pad_hi = (tile_end * TM).astype(jnp.int32).at[N_EXPERTS - 1].set(XS_TILES * TM)
    return runs, lp_col, lp_row, gate_row, texp, valid, pad_lo, pad_hi


def _for_each_run_piece(runs_ref, fn):
    for e in range(N_EXPERTS):
        n = runs_ref[0, e]
        zo = runs_ref[0, N_EXPERTS + e]
        go = runs_ref[0, 2 * N_EXPERTS + e]
        for size in RUN_SIZES:
            done = n & ~(2 * size - 1)
            pl.when((n & size) != 0)(functools.partial(
                fn, pl.multiple_of(zo + done, ROW_ALIGN), pl.multiple_of(go + done, ROW_ALIGN), size))


def _dispatch_kernel(pad_lo_ref, pad_hi_ref, runs_ref, runs_prev_ref, lp_ref, gate_ref, x_ref, mod_ref, g_ref,
                     xs_ref, z_ref, zero_ref, sems, zsem):
    i = pl.program_id(0)
    n = pl.num_programs(0)

    @pl.when(i == 0)
    def _():
        zero_ref[...] = jnp.zeros_like(zero_ref)
        for e in range(N_EXPERTS):
            lo, hi = pad_lo_ref[e] // ROW_ALIGN, pad_hi_ref[e] // ROW_ALIGN
            blk = lambda r: pltpu.make_async_copy(
                zero_ref, xs_ref.at[pl.ds(pl.multiple_of(r * ROW_ALIGN, ROW_ALIGN), ROW_ALIGN)], zsem)
            lax.fori_loop(lo, hi, lambda r, _: blk(r).start(), None)
            lax.fori_loop(lo, hi, lambda r, _: blk(r).wait(), None)

    def piece(slot, zrow, grow, size):
        return pltpu.make_async_copy(z_ref.at[slot, pl.ds(zrow, size)], xs_ref.at[pl.ds(grow, size)], sems.at[slot])

    def step(slot):
        f = _ffn_input(x_ref[...], mod_ref, g_ref).astype(BF16)
        rows = lax.broadcasted_iota(jnp.int32, (Z_ROWS, TM), 0)
        hit0, hit1 = lp_ref[0:1, :] == rows, lp_ref[1:2, :] == rows
        sel = jnp.where(hit0 | hit1, 1.0, 0.0).astype(BF16)
        z_ref[slot, :, :D_MODEL] = jnp.dot(sel, f, preferred_element_type=F32)
        gate = jnp.sum(jnp.where(hit0, gate_ref[0:1, :], 0.0) + jnp.where(hit1, gate_ref[1:2, :], 0.0),
                       axis=-1, keepdims=True)
        z_ref[slot, :, D_MODEL:] = jnp.broadcast_to(gate, (Z_ROWS, D_XS - D_MODEL))
        _for_each_run_piece(runs_ref, lambda zrow, grow, size: piece(slot, zrow, grow, size).start())

        @pl.when(i >= 1)
        def _():
            _for_each_run_piece(runs_prev_ref, lambda zrow, grow, size: piece(1 - slot, zrow, grow, size).wait())

        @pl.when(i == n - 1)
        def _():
            _for_each_run_piece(runs_ref, lambda zrow, grow, size: piece(slot, zrow, grow, size).wait())

    for slot in range(2):
        pl.when(i % 2 == slot)(functools.partial(step, slot))


def _dispatch(layer, x_lat, mod, pre_g, runs, lp_row, gate_row, pad_lo, pad_hi):
    smem = lambda: pl.BlockSpec(memory_space=pltpu.SMEM)
    runs_spec = lambda back: pl.BlockSpec((None, 1, 3 * N_EXPERTS), lambda i: (jnp.maximum(i - back, 0), 0, 0),
                                          memory_space=pltpu.SMEM)
    return pl.pallas_call(
        _dispatch_kernel,
        grid=(N_TOK_TILES,),
        in_specs=[
            smem(), smem(), runs_spec(0), runs_spec(1),
            pl.BlockSpec((None, 2, TM), lambda i: (i, 0, 0)),
            pl.BlockSpec((None, 2, TM), lambda i: (i, 0, 0)),
            pl.BlockSpec((TM, D_MODEL), lambda i: (i, 0)),
            _mod_spec(layer),
            _resident((1, D_MODEL), lambda i: (0, 0)),
        ],
        out_specs=pl.BlockSpec(memory_space=pl.ANY),
        out_shape=jax.ShapeDtypeStruct((XS_TILES * TM, D_XS), F32),
        scratch_shapes=[pltpu.VMEM((2, Z_ROWS, D_XS), F32), pltpu.VMEM((ROW_ALIGN, D_XS), F32),
                        pltpu.SemaphoreType.DMA((2,)), pltpu.SemaphoreType.DMA(())],
        compiler_params=_params(("arbitrary",), 48),
        name=f"moe_dispatch_l{layer}",
    )(pad_lo, pad_hi, runs, runs, lp_row, gate_row, x_lat, mod, pre_g)


def _expert_kernel(texp_ref, tvalid_ref, xs_ref, w1_ref, w3_ref, w2_ref, ys_ref, xb_ref, acc_ref):
    j = pl.program_id(0)
    k = pl.program_id(1)

    @pl.when((tvalid_ref[j] == 0) & (k == 0))
    def _():
        ys_ref[...] = jnp.zeros_like(ys_ref)

    @pl.when(tvalid_ref[j] == 1)
    def _():
        @pl.when(k == 0)
        def _():
            xb_ref[...] = xs_ref[:, :D_MODEL].astype(BF16)
            acc_ref[...] = jnp.zeros_like(acc_ref)

        xb = xb_ref[...]
        h1 = jnp.dot(xb, w1_ref[...], preferred_element_type=F32)
        h3 = jnp.dot(xb, w3_ref[...], preferred_element_type=F32)
        acc_ref[...] += jnp.dot((_silu(h1) * h3).astype(BF16), w2_ref[...], preferred_element_type=F32)

        @pl.when(k == pl.num_programs(1) - 1)
        def _():
            ys_ref[...] = acc_ref[...] * xs_ref[:, D_MODEL:D_MODEL + 1]


def _experts(layer, xs, texp, tvalid, w1, w3, w2):
    n_k = D_EXPERT // TF_MOE
    row = lambda j, k, texp, tvalid: (j, 0)
    kk = lambda j, k, tvalid: jnp.where(tvalid[j] == 1, k, n_k - 1)
    return pl.pallas_call(
        _expert_kernel,
        grid_spec=pltpu.PrefetchScalarGridSpec(
            num_scalar_prefetch=2,
            grid=(XS_TILES, n_k),
            in_specs=[
                pl.BlockSpec((TM, D_XS), row),
                pl.BlockSpec((None, D_MODEL, TF_MOE), lambda j, k, texp, tvalid: (texp[j], 0, kk(j, k, tvalid))),
                pl.BlockSpec((None, D_MODEL, TF_MOE), lambda j, k, texp, tvalid: (texp[j], 0, kk(j, k, tvalid))),
                pl.BlockSpec((None, TF_MOE, D_MODEL), lambda j, k, texp, tvalid: (texp[j], kk(j, k, tvalid), 0)),
            ],
            out_specs=pl.BlockSpec((TM, D_MODEL), row),
            scratch_shapes=[pltpu.VMEM((TM, D_MODEL), BF16), pltpu.VMEM((TM, D_MODEL), F32)],
        ),
        out_shape=jax.ShapeDtypeStruct((XS_TILES * TM, D_MODEL), F32),
        compiler_params=_params(("arbitrary", "arbitrary"), 56),
        name=f"moe_experts_l{layer}",
    )(texp, tvalid, xs, w1, w3, w2)


def _combine_kernel(runs_ref, runs_next_ref, ys_ref, lp_ref, x_ref, mod_ref, pg_ref, o_ref, buf_ref, sems):
    i = pl.program_id(0)
    n = pl.num_programs(0)

    def piece(slot, zrow, grow, size):
        return pltpu.make_async_copy(ys_ref.at[pl.ds(grow, size)], buf_ref.at[slot, pl.ds(zrow, size)], sems.at[slot])

    def step(slot):
        @pl.when(i == 0)
        def _():
            buf_ref[...] = jnp.zeros_like(buf_ref)
            _for_each_run_piece(runs_ref, lambda zrow, grow, size: piece(slot, zrow, grow, size).start())

        @pl.when(i + 1 < n)
        def _():
            _for_each_run_piece(runs_next_ref, lambda zrow, grow, size: piece(1 - slot, zrow, grow, size).start())

        _for_each_run_piece(runs_ref, lambda zrow, grow, size: piece(slot, zrow, grow, size).wait())
        cols = lax.broadcasted_iota(jnp.int32, (TM, Z_ROWS), 1)
        pick = jnp.where((lp_ref[:, 0:1] == cols) | (lp_ref[:, 1:2] == cols), 1.0, 0.0).astype(BF16)
        y = jnp.dot(pick, buf_ref[slot].astype(BF16), preferred_element_type=F32)
        o_ref[...] = x_ref[...] + mod_ref[5:6, :] * (_rms(y) * pg_ref[...])

    for slot in range(2):
        pl.when(i % 2 == slot)(functools.partial(step, slot))


def _combine(layer, ys, runs, lp_col, x_lat, mod, post_g):
    row = lambda i: (i, 0)
    runs_spec = lambda fwd: pl.BlockSpec(
        (None, 1, 3 * N_EXPERTS), lambda i: (jnp.minimum(i + fwd, N_TOK_TILES - 1), 0, 0), memory_space=pltpu.SMEM)
    return pl.pallas_call(
        _combine_kernel,
        grid=(N_TOK_TILES,),
        in_specs=[
            runs_spec(0), runs_spec(1),
            pl.BlockSpec(memory_space=pl.ANY),
            pl.BlockSpec((TM, 2), row),
            pl.BlockSpec((TM, D_MODEL), row),
            _mod_spec(layer),
            _resident((1, D_MODEL), lambda i: (0, 0)),
        ],
        out_specs=pl.BlockSpec((TM, D_MODEL), row),
        out_shape=jax.ShapeDtypeStruct((N_LAT, D_MODEL), F32),
        scratch_shapes=[pltpu.VMEM((2, Z_ROWS, D_MODEL), F32), pltpu.SemaphoreType.DMA((2,))],
        compiler_params=_params(("arbitrary",), 40),
        name=f"moe_combine_l{layer}",
    )(runs, runs, ys, lp_col, x_lat, mod, post_g)


def kernel(x, c, ctx, c_ctx, ada_w, ada_b, pre_mix_g, post_mix_g, pre_ffn_g, post_ffn_g, w_in, w_out,
           lam_q1, lam_k1, lam_q2, lam_k2, subln_g, sg_ln_g, sg_ln_b, sg_w, sg_b,
           ffn_w1, ffn_w3, ffn_w2, router_w, moe_w1, moe_w3, moe_w2):
    x_lat = x.reshape(N_LAT, D_MODEL)
    x_ctx = ctx.reshape(N_CTX, D_MODEL)
    cc = jnp.concatenate([c, c_ctx[None, :], jnp.zeros((MOD_ROWS - BATCH - 1, D_MODEL), F32)], axis=0)
    mod = _modulation(cc, ada_w, ada_b)
    rope_tabs = _rope_tables()
    n_lat_tiles = N_LAT // TM
    x_all = None
    for l in range(DEPTH):
        last = l == DEPTH - 1
        vec = lambda a: a[l][None, :]
        if l == 0:
            xa, xb, xb_tile0 = x_lat, x_ctx, 0
        else:
            xa, xb, xb_tile0 = x_all, x_all, n_lat_tiles
        q, kt, v, u, gn = _in_proj(l, xa, xb, xb_tile0, mod, vec(pre_mix_g), w_in[l].astype(BF16),
                                   rope_tabs, vec(sg_ln_g), vec(sg_ln_b))
        lam_params = [vec(lam_q1), vec(lam_k1), vec(lam_q2), vec(lam_k2)]
        a_lat = _attention(l, q, kt, v, lam_params, vec(subln_g), latent=True)
        a_ctx = a_lat if last else _attention(l, q, kt, v, lam_params, vec(subln_g), latent=False)
        sg_bias = jnp.repeat(sg_b[l].T, CHUNK, axis=1)
        n_tiles = n_lat_tiles if last else N_ALL // TM
        x_mid = _out_proj(l, n_tiles, a_lat, a_ctx, u, gn, sg_w[l].astype(BF16), sg_bias,
                          w_out[l].astype(BF16), vec(post_mix_g), mod, xa, xb, xb_tile0)
        if l % 2 == 0:
            i = l // 2
            x_all = _ffn(l, x_mid, mod, vec(pre_ffn_g), ffn_w1[i].astype(BF16), ffn_w3[i].astype(BF16),
                         ffn_w2[i].astype(BF16), vec(post_ffn_g))
        else:
            i = l // 2
            ids, rank, gate, cnt = _router(l, x_mid, mod, vec(pre_ffn_g), router_w[i])
            runs, lp_col, lp_row, gate_row, texp, tvalid, pad_lo, pad_hi = _routing_tables(ids, rank, gate, cnt)
            xs = _dispatch(l, x_mid, mod, vec(pre_ffn_g), runs, lp_row, gate_row, pad_lo, pad_hi)
            ys = _experts(l, xs, texp, tvalid, moe_w1[i].astype(BF16), moe_w3[i].astype(BF16),
                          moe_w2[i].astype(BF16))
            x_all = _combine(l, ys, runs, lp_col, x_mid, mod, vec(post_ffn_g))
    return x_all.reshape(BATCH, SEQ, D_MODEL)
```

```python
import functools
import math

import jax
import jax.numpy as jnp
from jax import lax
from jax.experimental import pallas as pl
from jax.experimental.pallas import tpu as pltpu

D_MODEL = 1024
BATCH = 16
SEQ = 2048
DEPTH = 2
GRID_W = 64
CTX_LEN = 256
DA_WIDTH = 512
DA_HEADS = 4
DA_VDIM = 128
DA_HEAD_DIM = 64
SG_WIDTH = 512
SG_GROUPS = 4
CHUNK = 128
ROPE_BASE = 10000.0
D_FF = 2816
N_EXPERTS = 8
D_EXPERT = 3584
EPS = 1e-6
IN_COLS = 3 * DA_WIDTH + 2 * SG_WIDTH

N_LAT = BATCH * SEQ
N_CTX = BATCH * CTX_LEN
N_ALL = N_LAT + N_CTX
MOD_ROWS = 32

TP = 1024
TM = 512
TQ = 1024
TQ_SUB = 512
KEY_SEG = 2048
TF_DENSE = 256
TF_MOE = 1792
N_TOK_TILES = N_LAT // TM
ROW_ALIGN = 8
RUN_SIZES = (512, 256, 128, 64, 32, 16, 8)
Z_ROWS = 2 * TM + N_EXPERTS * ROW_ALIGN
D_XS = D_MODEL + 128
XS_TILES = (2 * N_LAT + N_TOK_TILES * N_EXPERTS * (ROW_ALIGN - 1)) // TM + 1 + N_EXPERTS
MIB = 2 ** 20

BF16 = jnp.bfloat16
F32 = jnp.float32


def _rms(x):
    return x * lax.rsqrt(jnp.mean(x * x, axis=-1, keepdims=True) + EPS)


def _silu(x):
    return x * jax.nn.sigmoid(x)


def _resident(block_shape, index_map):
    return pl.BlockSpec(block_shape, index_map, pipeline_mode=pl.Buffered(1))


def _params(semantics, vmem_mib):
    return pltpu.CompilerParams(dimension_semantics=semantics, vmem_limit_bytes=vmem_mib * MIB)


def _mod_kernel(cc_ref, w_ref, b_ref, o_ref):
    a = _silu(cc_ref[...]).astype(BF16)
    o_ref[...] = jnp.dot(a, w_ref[...].astype(BF16), preferred_element_type=F32) + b_ref[...]


def _modulation(cc, ada_w, ada_b):
    tn = 1536
    out = pl.pallas_call(
        _mod_kernel,
        grid=(DEPTH, 6 * D_MODEL // tn),
        in_specs=[
            pl.BlockSpec((MOD_ROWS, D_MODEL), lambda l, j: (0, 0)),
            pl.BlockSpec((None, D_MODEL, tn), lambda l, j: (l, 0, j)),
            pl.BlockSpec((None, 1, tn), lambda l, j: (l, 0, j)),
        ],
        out_specs=pl.BlockSpec((None, MOD_ROWS, tn), lambda l, j: (l, 0, j)),
        out_shape=jax.ShapeDtypeStruct((DEPTH, MOD_ROWS, 6 * D_MODEL), F32),
        compiler_params=_params(("parallel", "parallel"), 40),
        name="modulation",
    )(cc, ada_w, ada_b.reshape(DEPTH, 1, 6 * D_MODEL))
    return out.reshape(DEPTH, MOD_ROWS, 6, D_MODEL)


def _mod_spec(layer, tile):
    tiles_per_batch = SEQ // tile
    return pl.BlockSpec((None, None, 6, D_MODEL),
                        lambda i: (layer, jnp.minimum(i // tiles_per_batch, BATCH), 0, 0))


def _in_proj_kernel(n_lat_tiles, xa_ref, xb_ref, mod_ref, g_ref, w_ref, cos_ref, sa_ref, sb_ref,
                    lng_ref, lnb_ref, q_ref, kt_ref, v_ref, u_ref, gn_ref):
    i = pl.program_id(0)
    x = jnp.where(i < n_lat_tiles, xa_ref[...], xb_ref[...])
    h = _rms(x) * g_ref[...]
    h = h * (1.0 + mod_ref[1:2, :]) + mod_ref[0:1, :]
    p = jnp.dot(h.astype(BF16), w_ref[...], preferred_element_type=F32)

    cos, sa, sb = cos_ref[...], sa_ref[...], sb_ref[...]

    def rope(t):
        return t * cos + pltpu.roll(t, 112, 1) * sa + pltpu.roll(t, 16, 1) * sb

    for hd in range(DA_HEADS):
        lo, hi = hd * DA_VDIM, (hd + 1) * DA_VDIM
        q_ref[:, lo:hi] = (rope(p[:, lo:hi]) * (DA_HEAD_DIM ** -0.5)).astype(BF16)
        kt_ref[lo:hi, :] = rope(p[:, DA_WIDTH + lo:DA_WIDTH + hi]).T.astype(BF16)
    v_ref[...] = p[:, 2 * DA_WIDTH:3 * DA_WIDTH].astype(BF16)
    u_ref[...] = p[:, 3 * DA_WIDTH:3 * DA_WIDTH + SG_WIDTH].astype(BF16)
    gv = p[:, 3 * DA_WIDTH + SG_WIDTH:]
    mu = jnp.mean(gv, axis=-1, keepdims=True)
    var = jnp.mean(jnp.square(gv - mu), axis=-1, keepdims=True)
    gn_ref[...] = ((gv - mu) * lax.rsqrt(var + EPS) * lng_ref[...] + lnb_ref[...]).astype(BF16)


def _in_proj(layer, xa, xb, xb_tile0, mod, pre_g, w_in, rope_tabs, ln_g, ln_b):
    n_lat_tiles = N_LAT // TP
    n_tiles = N_ALL // TP
    tiles_per_batch = SEQ // TP
    tab_spec = pl.BlockSpec(
        (TP, DA_VDIM), lambda i: (jnp.where(i < n_lat_tiles, i % tiles_per_batch, tiles_per_batch), 0))
    row = lambda i: (i, 0)
    outs = pl.pallas_call(
        functools.partial(_in_proj_kernel, n_lat_tiles),
        grid=(n_tiles,),
        in_specs=[
            pl.BlockSpec((TP, D_MODEL), lambda i: (jnp.minimum(i, n_lat_tiles - 1), 0)),
            pl.BlockSpec((TP, D_MODEL), lambda i: (xb_tile0 + jnp.maximum(i - n_lat_tiles, 0), 0)),
            _mod_spec(layer, TP),
            _resident((1, D_MODEL), lambda i: (0, 0)),
            _resident((D_MODEL, IN_COLS), lambda i: (0, 0)),
            tab_spec, tab_spec, tab_spec,
            _resident((1, SG_WIDTH), lambda i: (0, 0)),
            _resident((1, SG_WIDTH), lambda i: (0, 0)),
        ],
        out_specs=[
            pl.BlockSpec((TP, DA_WIDTH), row),
            pl.BlockSpec((DA_WIDTH, TP), lambda i: (0, i)),
            pl.BlockSpec((TP, DA_WIDTH), row),
            pl.BlockSpec((TP, SG_WIDTH), row),
            pl.BlockSpec((TP, SG_WIDTH), row),
        ],
        out_shape=[
            jax.ShapeDtypeStruct((N_ALL, DA_WIDTH), BF16),
            jax.ShapeDtypeStruct((DA_WIDTH, N_ALL), BF16),
            jax.ShapeDtypeStruct((N_ALL, DA_WIDTH), BF16),
            jax.ShapeDtypeStruct((N_ALL, SG_WIDTH), BF16),
            jax.ShapeDtypeStruct((N_ALL, SG_WIDTH), BF16),
        ],
        compiler_params=_params(("parallel",), 48),
        name=f"in_proj_l{layer}",
    )(xa, xb, mod, pre_g, w_in, *rope_tabs, ln_g, ln_b)
    return outs


def _rope_tables():
    pos = jnp.arange(SEQ, dtype=jnp.int32)
    row = (pos // GRID_W).astype(F32)
    col = (pos % GRID_W).astype(F32)
    half = DA_HEAD_DIM // 2
    inv = ROPE_BASE ** (-jnp.arange(0, half, 2, dtype=F32) / half)
    ang_r = row[:, None] * inv
    ang_c = col[:, None] * inv
    ang = jnp.concatenate([ang_r, ang_r, ang_c, ang_c], axis=-1)
    cos = jnp.tile(jnp.cos(ang), (1, 2))
    sin = jnp.tile(jnp.sin(ang), (1, 2))
    first = (jnp.arange(DA_VDIM) % 32) < 16
    sa = jnp.where(first, -sin, 0.0)
    sb = jnp.where(first, 0.0, sin)
    ident = jnp.zeros((TP, DA_VDIM), F32)
    return (jnp.concatenate([cos, ident + 1.0], axis=0),
            jnp.concatenate([sa, ident], axis=0),
            jnp.concatenate([sb, ident], axis=0))


def _attn_kernel(n_src, lambda_init, q_ref, *refs):
    kt_refs = refs[:n_src]
    v_refs = refs[n_src:2 * n_src]
    lq1_ref, lk1_ref, lq2_ref, lk2_ref, subg_ref = refs[2 * n_src:2 * n_src + 5]
    o_ref = refs[-1 - n_src]
    vext_refs = refs[-n_src:]

    @pl.when(pl.program_id(2) == 0)
    def _():
        for v_ref, ve_ref in zip(v_refs, vext_refs):
            ve_ref[:, :DA_VDIM] = v_ref[...]
            ve_ref[:, DA_VDIM:] = jnp.ones((v_ref.shape[0], DA_VDIM), BF16)

    segs = []
    for kt_ref, ve_ref in zip(kt_refs, vext_refs):
        n_keys = ve_ref.shape[0]
        for lo in range(0, n_keys, KEY_SEG):
            segs.append((kt_ref, ve_ref, lo, min(lo + KEY_SEG, n_keys)))

    lam = (jnp.exp(jnp.sum(lq1_ref[...] * lk1_ref[...], axis=-1, keepdims=True))
           - jnp.exp(jnp.sum(lq2_ref[...] * lk2_ref[...], axis=-1, keepdims=True)) + lambda_init)
    tq_sub = min(TQ_SUB, q_ref.shape[0])

    def scores(r0):
        q = q_ref[r0:r0 + tq_sub, :]
        first = lax.broadcasted_iota(jnp.int32, q.shape, 1) < DA_HEAD_DIM
        zero = jnp.zeros_like(q)
        qms = (jnp.where(first, q, zero), jnp.where(first, zero, q))
        return [[jnp.dot(qm, kt_ref[:, lo:hi], preferred_element_type=F32) for kt_ref, _, lo, hi in segs]
                for qm in qms]

    def finish(r0, ss_maps):
        maps = []
        for ss in ss_maps:
            m = functools.reduce(jnp.maximum, [jnp.max(s, axis=-1, keepdims=True) for s in ss])
            ts = [jnp.dot(jnp.exp(s - m).astype(BF16), ve_ref[lo:hi, :], preferred_element_type=F32)
                  for s, (_, ve_ref, lo, hi) in zip(ss, segs)]
            acc = functools.reduce(jnp.add, ts)
            maps.append(acc[:, :DA_VDIM] / acc[:, DA_VDIM:])
        o = maps[0] - lam * maps[1]
        o_ref[r0:r0 + tq_sub, :] = (_rms(o) * subg_ref[...] * (1.0 - lambda_init)).astype(BF16)

    starts = list(range(0, q_ref.shape[0], tq_sub))
    pending = scores(starts[0])
    for idx, r0 in enumerate(starts):
        nxt = scores(starts[idx + 1]) if idx + 1 < len(starts) else None
        finish(r0, pending)
        pending = nxt


def _attention(layer, q, kt, v, lam_params, subg, latent):
    lambda_init = 0.8 - 0.6 * math.exp(-0.3 * layer)
    ctx_blk0 = N_LAT // CTX_LEN
    small = lambda: pl.BlockSpec((1, DA_HEAD_DIM), lambda b, h, t: (0, 0))
    tail_specs = [small(), small(), small(), small(), pl.BlockSpec((1, DA_VDIM), lambda b, h, t: (0, 0))]
    kt_ctx = pl.BlockSpec((DA_VDIM, CTX_LEN), lambda b, h, t: (h, ctx_blk0 + b))
    v_ctx = pl.BlockSpec((CTX_LEN, DA_VDIM), lambda b, h, t: (ctx_blk0 + b, h))
    if latent:
        tq, n_q, n_rows = TQ, SEQ // TQ, N_LAT
        q_map = lambda b, h, t: (b * n_q + t, h)
        o_map = q_map
        kt_specs = [kt_ctx, pl.BlockSpec((DA_VDIM, SEQ), lambda b, h, t: (h, b))]
        v_specs = [v_ctx, pl.BlockSpec((SEQ, DA_VDIM), lambda b, h, t: (b, h))]
        key_lens = [CTX_LEN, SEQ]
    else:
        tq, n_q, n_rows = CTX_LEN, 1, N_CTX
        q_map = lambda b, h, t: (ctx_blk0 + b, h)
        o_map = lambda b, h, t: (b, h)
        kt_specs, v_specs, key_lens = [kt_ctx], [v_ctx], [CTX_LEN]
    n_src = len(key_lens)
    return pl.pallas_call(
        functools.partial(_attn_kernel, n_src, lambda_init),
        grid=(BATCH, DA_HEADS, n_q),
        in_specs=[pl.BlockSpec((tq, DA_VDIM), q_map)] + kt_specs + v_specs + tail_specs,
        out_specs=pl.BlockSpec((tq, DA_VDIM), o_map),
        out_shape=jax.ShapeDtypeStruct((n_rows, DA_WIDTH), BF16),
        scratch_shapes=[pltpu.VMEM((n, 2 * DA_VDIM), BF16) for n in key_lens],
        compiler_params=_params(("parallel", "parallel", "arbitrary"), 48),
        name=f"attn_l{layer}_{'lat' if latent else 'ctx'}",
    )(q, *([kt] * n_src), *([v] * n_src), *lam_params, subg)


def _out_proj_kernel(n_lat_tiles, aa_ref, ab_ref, u_ref, gn_ref, sgw_ref, sgb_ref, w_ref, pg_ref, mod_ref,
                     xa_ref, xb_ref, o_ref, s_ref):
    i = pl.program_id(0)
    a = jnp.where(i < n_lat_tiles, aa_ref[...], ab_ref[...])
    for c in range(TP // CHUNK):
        rows = slice(c * CHUNK, (c + 1) * CHUNK)
        for g in range(SG_GROUPS):
            cols = slice(g * CHUNK, (g + 1) * CHUNK)
            mixed = jnp.dot(sgw_ref[g], gn_ref[rows, cols], preferred_element_type=F32) + sgb_ref[:, cols]
            s_ref[rows, cols] = (u_ref[rows, cols].astype(F32) * mixed).astype(BF16)
    o = (jnp.dot(a, w_ref[:DA_WIDTH, :], preferred_element_type=F32)
         + jnp.dot(s_ref[...], w_ref[DA_WIDTH:, :], preferred_element_type=F32))
    x = jnp.where(i < n_lat_tiles, xa_ref[...], xb_ref[...])
    o_ref[...] = x + mod_ref[2:3, :] * (_rms(o) * pg_ref[...])


def _out_proj(layer, n_tiles, a_lat, a_ctx, u, gn, sg_w, sg_bias, w_out, post_g, mod, xa, xb, xb_tile0):
    n_lat_tiles = N_LAT // TP
    row = lambda i: (i, 0)
    half = lambda: pl.BlockSpec((TP, DA_WIDTH), row)
    return pl.pallas_call(
        functools.partial(_out_proj_kernel, n_lat_tiles),
        grid=(n_tiles,),
        in_specs=[
            pl.BlockSpec((TP, DA_WIDTH), lambda i: (jnp.minimum(i, n_lat_tiles - 1), 0)),
            pl.BlockSpec((TP, DA_WIDTH), lambda i: (jnp.maximum(i - n_lat_tiles, 0), 0)),
            half(), half(),
            _resident((SG_GROUPS, CHUNK, CHUNK), lambda i: (0, 0, 0)),
            _resident((CHUNK, SG_WIDTH), lambda i: (0, 0)),
            _resident((D_MODEL, D_MODEL), lambda i: (0, 0)),
            _resident((1, D_MODEL), lambda i: (0, 0)),
            _mod_spec(layer, TP),
            pl.BlockSpec((TP, D_MODEL), lambda i: (jnp.minimum(i, n_lat_tiles - 1), 0)),
            pl.BlockSpec((TP, D_MODEL), lambda i: (xb_tile0 + jnp.maximum(i - n_lat_tiles, 0), 0)),
        ],
        out_specs=pl.BlockSpec((TP, D_MODEL), row),
        out_shape=jax.ShapeDtypeStruct((n_tiles * TP, D_MODEL), F32),
        scratch_shapes=[pltpu.VMEM((TP, SG_WIDTH), BF16)],
        compiler_params=_params(("parallel",), 48),
        name=f"out_proj_l{layer}",
    )(a_lat, a_ctx, u, gn, sg_w, sg_bias, w_out, post_g, mod, xa, xb)


def _ffn_kernel(x_ref, mod_ref, g_ref, w1_ref, w3_ref, w2_ref, pg_ref, o_ref):
    x = x_ref[...]
    f = ((_rms(x) * g_ref[...]) * (1.0 + mod_ref[4:5, :]) + mod_ref[3:4, :]).astype(BF16)
    y = None
    for k in range(D_FF // TF_DENSE):
        cols = slice(k * TF_DENSE, (k + 1) * TF_DENSE)
        h1 = jnp.dot(f, w1_ref[:, cols], preferred_element_type=F32)
        h3 = jnp.dot(f, w3_ref[:, cols], preferred_element_type=F32)
        t = jnp.dot((_silu(h1) * h3).astype(BF16), w2_ref[cols, :], preferred_element_type=F32)
        y = t if y is None else y + t
    o_ref[...] = x + mod_ref[5:6, :] * (_rms(y) * pg_ref[...])


def _ffn(layer, x_all, mod, pre_g, w1, w3, w2, post_g):
    row = lambda i: (i, 0)
    return pl.pallas_call(
        _ffn_kernel,
        grid=(N_ALL // TP,),
        in_specs=[
            pl.BlockSpec((TP, D_MODEL), row),
            _mod_spec(layer, TP),
            _resident((1, D_MODEL), lambda i: (0, 0)),
            _resident((D_MODEL, D_FF), lambda i: (0, 0)),
            _resident((D_MODEL, D_FF), lambda i: (0, 0)),
            _resident((D_FF, D_MODEL), lambda i: (0, 0)),
            _resident((1, D_MODEL), lambda i: (0, 0)),
        ],
        out_specs=pl.BlockSpec((TP, D_MODEL), row),
        out_shape=jax.ShapeDtypeStruct((N_ALL, D_MODEL), F32),
        compiler_params=_params(("parallel",), 56),
        name=f"ffn_l{layer}",
    )(x_all, mod, pre_g, w1, w3, w2, post_g)


def _ffn_input(x, mod_ref, g_ref):
    return (_rms(x) * g_ref[...]) * (1.0 + mod_ref[4:5, :]) + mod_ref[3:4, :]


def _router_kernel(x_ref, mod_ref, g_ref, rw_ref, ids_ref, rank_ref, gate_ref, cnt_ref):
    f = _ffn_input(x_ref[...], mod_ref, g_ref)
    rw = rw_ref[...]
    f_hi, rw_hi = f.astype(BF16), rw.astype(BF16)
    f_lo, rw_lo = (f - f_hi.astype(F32)).astype(BF16), (rw - rw_hi.astype(F32)).astype(BF16)
    logits = (jnp.dot(f_hi, rw_hi, preferred_element_type=F32) + jnp.dot(f_lo, rw_hi, preferred_element_type=F32)
              + jnp.dot(f_hi, rw_lo, preferred_element_type=F32))
    lane = lax.broadcasted_iota(jnp.int32, logits.shape, 1)
    v1 = jnp.max(logits, axis=-1, keepdims=True)
    i1 = jnp.min(jnp.where(logits == v1, lane, N_EXPERTS), axis=-1, keepdims=True)
    rest = jnp.where(lane == i1, -jnp.inf, logits)
    v2 = jnp.max(rest, axis=-1, keepdims=True)
    i2 = jnp.min(jnp.where(rest == v2, lane, N_EXPERTS), axis=-1, keepdims=True)
    e = jnp.exp(v2 - v1)
    w1 = 1.0 / (1.0 + e)

    sel1, sel2 = lane == i1, lane == i2
    onehot = jnp.where(sel1 | sel2, 1.0, 0.0)
    r = lax.broadcasted_iota(jnp.int32, (TM, TM), 0)
    c = lax.broadcasted_iota(jnp.int32, (TM, TM), 1)
    tri = jnp.where(c < r, 1.0, 0.0).astype(BF16)
    prefix = jnp.dot(tri, onehot.astype(BF16), preferred_element_type=F32)
    rank1 = jnp.sum(jnp.where(sel1, prefix, 0.0), axis=-1, keepdims=True)
    rank2 = jnp.sum(jnp.where(sel2, prefix, 0.0), axis=-1, keepdims=True)

    slot0 = lax.broadcasted_iota(jnp.int32, (TM, 2), 1) == 0
    ids_ref[...] = jnp.where(slot0, i1, i2)
    rank_ref[...] = jnp.where(slot0, rank1, rank2).astype(jnp.int32)
    gate_ref[...] = jnp.where(slot0, w1, e * w1)
    cnt_ref[...] = jnp.sum(onehot, axis=0, keepdims=True).astype(jnp.int32)


def _router(layer, x_lat, mod, pre_g, router_w):
    row = lambda i: (i, 0)
    pair = lambda: pl.BlockSpec((TM, 2), row)
    return pl.pallas_call(
        _router_kernel,
        grid=(N_TOK_TILES,),
        in_specs=[
            pl.BlockSpec((TM, D_MODEL), row),
            _mod_spec(layer, TM),
            _resident((1, D_MODEL), lambda i: (0, 0)),
            _resident((D_MODEL, N_EXPERTS), lambda i: (0, 0)),
        ],
        out_specs=[pair(), pair(), pair(), pl.BlockSpec((None, 1, N_EXPERTS), lambda i: (i, 0, 0))],
        out_shape=[jax.ShapeDtypeStruct((N_LAT, 2), jnp.int32),
                   jax.ShapeDtypeStruct((N_LAT, 2), jnp.int32),
                   jax.ShapeDtypeStruct((N_LAT, 2), F32),
                   jax.ShapeDtypeStruct((N_TOK_TILES, 1, N_EXPERTS), jnp.int32)],
        compiler_params=_params(("parallel",), 32),
        name=f"router_l{layer}",
    )(x_lat, mod, pre_g, router_w)


def _routing_tables(ids, rank, gate, cnt):
    cnt = cnt.reshape(N_TOK_TILES, N_EXPERTS)
    run = (cnt + ROW_ALIGN - 1) // ROW_ALIGN * ROW_ALIGN
    zoff = jnp.cumsum(run, axis=1) - run
    grp = jnp.sum(run, axis=0)
    tiles = (grp + TM - 1) // TM
    tile_end = jnp.cumsum(tiles)
    off = (tile_end - tiles) * TM
    gstart = off[None, :] + jnp.cumsum(run, axis=0) - run
    runs = jnp.concatenate([run, zoff, gstart], axis=1).astype(jnp.int32).reshape(N_TOK_TILES, 1, 3 * N_EXPERTS)

    ids3 = ids.reshape(N_TOK_TILES, TM, 2)
    hit = ids3[..., None] == jnp.arange(N_EXPERTS, dtype=jnp.int32)
    lp = jnp.sum(jnp.where(hit, zoff[:, None, None, :], 0), axis=-1) + rank.reshape(N_TOK_TILES, TM, 2)
    lp = lp.astype(jnp.int32)
    lp_col = lp.reshape(N_LAT, 2)
    lp_row = jnp.transpose(lp, (0, 2, 1))
    gate_row = jnp.transpose(gate.reshape(N_TOK_TILES, TM, 2), (0, 2, 1))

    j = jnp.arange(XS_TILES, dtype=jnp.int32)
    n_used = tile_end[-1]
    valid = (j < n_used).astype(jnp.int32)
    jc = jnp.minimum(j, n_used - 1)
    texp = jnp.minimum(jnp.sum(jc[:, None] >= tile_end[None, :], axis=1), N_EXPERTS - 1).astype(jnp.int32)
    pad_lo = (off + grp).astype(jnp.int32)
    pad_hi = (tile_end * TM).astype(jnp.int32).at[N_EXPERTS - 1].set(XS_TILES * TM)
    return runs, lp_col, lp_row, gate_row, texp, valid, pad_lo, pad_hi


def _for_each_run_piece(runs_ref, fn):
    for e in range(N_EXPERTS):
        n = runs_ref[0, e]
        zo = runs_ref[0, N_EXPERTS + e]
        go = runs_ref[0, 2 * N_EXPERTS + e]
        for size in RUN_SIZES:
            done = n & ~(2 * size - 1)
            pl.when((n & size) != 0)(functools.partial(
                fn, pl.multiple_of(zo + done, ROW_ALIGN), pl.multiple_of(go + done, ROW_ALIGN), size))


def _dispatch_kernel(pad_lo_ref, pad_hi_ref, runs_ref, runs_prev_ref, lp_ref, gate_ref, x_ref, mod_ref, g_ref,
                     xs_ref, z_ref, zero_ref, sems, zsem):
    i = pl.program_id(0)
    n = pl.num_programs(0)

    @pl.when(i == 0)
    def _():
        zero_ref[...] = jnp.zeros_like(zero_ref)
        for e in range(N_EXPERTS):
            lo, hi = pad_lo_ref[e] // ROW_ALIGN, pad_hi_ref[e] // ROW_ALIGN
            blk = lambda r: pltpu.make_async_copy(
                zero_ref, xs_ref.at[pl.ds(pl.multiple_of(r * ROW_ALIGN, ROW_ALIGN), ROW_ALIGN)], zsem)
            lax.fori_loop(lo, hi, lambda r, _: blk(r).start(), None)
            lax.fori_loop(lo, hi, lambda r, _: blk(r).wait(), None)

    def piece(slot, zrow, grow, size):
        return pltpu.make_async_copy(z_ref.at[slot, pl.ds(zrow, size)], xs_ref.at[pl.ds(grow, size)], sems.at[slot])

    def step(slot):
        f = _ffn_input(x_ref[...], mod_ref, g_ref).astype(BF16)
        rows = lax.broadcasted_iota(jnp.int32, (Z_ROWS, TM), 0)
        hit0, hit1 = lp_ref[0:1, :] == rows, lp_ref[1:2, :] == rows
        sel = jnp.where(hit0 | hit1, 1.0, 0.0).astype(BF16)
        z_ref[slot, :, :D_MODEL] = jnp.dot(sel, f, preferred_element_type=F32)
        gate = jnp.sum(jnp.where(hit0, gate_ref[0:1, :], 0.0) + jnp.where(hit1, gate_ref[1:2, :], 0.0),
                       axis=-1, keepdims=True)
        z_ref[slot, :, D_MODEL:] = jnp.broadcast_to(gate, (Z_ROWS, D_XS - D_MODEL))
        _for_each_run_piece(runs_ref, lambda zrow, grow, size: piece(slot, zrow, grow, size).start())

        @pl.when(i >= 1)
        def _():
            _for_each_run_piece(runs_prev_ref, lambda zrow, grow, size: piece(1 - slot, zrow, grow, size).wait())

        @pl.when(i == n - 1)
        def _():
            _for_each_run_piece(runs_ref, lambda zrow, grow, size: piece(slot, zrow, grow, size).wait())

    for slot in range(2):
        pl.when(i % 2 == slot)(functools.partial(step, slot))


def _dispatch(layer, x_lat, mod, pre_g, runs, lp_row, gate_row, pad_lo, pad_hi):
    smem = lambda: pl.BlockSpec(memory_space=pltpu.SMEM)
    runs_spec = lambda back: pl.BlockSpec((None, 1, 3 * N_EXPERTS), lambda i: (jnp.maximum(i - back, 0), 0, 0),
                                          memory_space=pltpu.SMEM)
    return pl.pallas_call(
        _dispatch_kernel,
        grid=(N_TOK_TILES,),
        in_specs=[
            smem(), smem(), runs_spec(0), runs_spec(1),
            pl.BlockSpec((None, 2, TM), lambda i: (i, 0, 0)),
            pl.BlockSpec((None, 2, TM), lambda i: (i, 0, 0)),
            pl.BlockSpec((TM, D_MODEL), lambda i: (i, 0)),
            _mod_spec(layer, TM),
            _resident((1, D_MODEL), lambda i: (0, 0)),
        ],
        out_specs=pl.BlockSpec(memory_space=pl.ANY),
        out_shape=jax.ShapeDtypeStruct((XS_TILES * TM, D_XS), F32),
        scratch_shapes=[pltpu.VMEM((2, Z_ROWS, D_XS), F32), pltpu.VMEM((ROW_ALIGN, D_XS), F32),
                        pltpu.SemaphoreType.DMA((2,)), pltpu.SemaphoreType.DMA(())],
        compiler_params=_params(("arbitrary",), 48),
        name=f"moe_dispatch_l{layer}",
    )(pad_lo, pad_hi, runs, runs, lp_row, gate_row, x_lat, mod, pre_g)


def _expert_kernel(texp_ref, tvalid_ref, xs_ref, w1_ref, w3_ref, w2_ref, ys_ref, xb_ref, acc_ref):
    j = pl.program_id(0)
    k = pl.program_id(1)

    @pl.when((tvalid_ref[j] == 0) & (k == 0))
    def _():
        ys_ref[...] = jnp.zeros_like(ys_ref)

    @pl.when(tvalid_ref[j] == 1)
    def _():
        @pl.when(k == 0)
        def _():
            xb_ref[...] = xs_ref[:, :D_MODEL].astype(BF16)
            acc_ref[...] = jnp.zeros_like(acc_ref)

        xb = xb_ref[...]
        h1 = jnp.dot(xb, w1_ref[...], preferred_element_type=F32)
        h3 = jnp.dot(xb, w3_ref[...], preferred_element_type=F32)
        acc_ref[...] += jnp.dot((_silu(h1) * h3).astype(BF16), w2_ref[...], preferred_element_type=F32)

        @pl.when(k == pl.num_programs(1) - 1)
        def _():
            ys_ref[...] = acc_ref[...] * xs_ref[:, D_MODEL:D_MODEL + 1]


def _experts(layer, xs, texp, tvalid, w1, w3, w2):
    n_k = D_EXPERT // TF_MOE
    row = lambda j, k, texp, tvalid: (j, 0)
    kk = lambda j, k, tvalid: jnp.where(tvalid[j] == 1, k, n_k - 1)
    return pl.pallas_call(
        _expert_kernel,
        grid_spec=pltpu.PrefetchScalarGridSpec(
            num_scalar_prefetch=2,
            grid=(XS_TILES, n_k),
            in_specs=[
                pl.BlockSpec((TM, D_XS), row),
                pl.BlockSpec((None, D_MODEL, TF_MOE), lambda j, k, texp, tvalid: (texp[j], 0, kk(j, k, tvalid))),
                pl.BlockSpec((None, D_MODEL, TF_MOE), lambda j, k, texp, tvalid: (texp[j], 0, kk(j, k, tvalid))),
                pl.BlockSpec((None, TF_MOE, D_MODEL), lambda j, k, texp, tvalid: (texp[j], kk(j, k, tvalid), 0)),
            ],
            out_specs=pl.BlockSpec((TM, D_MODEL), row),
            scratch_shapes=[pltpu.VMEM((TM, D_MODEL), BF16), pltpu.VMEM((TM, D_MODEL), F32)],
        ),
        out_shape=jax.ShapeDtypeStruct((XS_TILES * TM, D_MODEL), F32),
        compiler_params=_params(("arbitrary", "arbitrary"), 56),
        name=f"moe_experts_l{layer}",
    )(texp, tvalid, xs, w1, w3, w2)


def _combine_kernel(runs_ref, runs_next_ref, ys_ref, lp_ref, x_ref, mod_ref, pg_ref, o_ref, buf_ref, sems):
    i = pl.program_id(0)
    n = pl.num_programs(0)

    def piece(slot, zrow, grow, size):
        return pltpu.make_async_copy(ys_ref.at[pl.ds(grow, size)], buf_ref.at[slot, pl.ds(zrow, size)], sems.at[slot])

    def step(slot):
        @pl.when(i == 0)
        def _():
            buf_ref[...] = jnp.zeros_like(buf_ref)
            _for_each_run_piece(runs_ref, lambda zrow, grow, size: piece(slot, zrow, grow, size).start())

        @pl.when(i + 1 < n)
        def _():
            _for_each_run_piece(runs_next_ref, lambda zrow, grow, size: piece(1 - slot, zrow, grow, size).start())

        _for_each_run_piece(runs_ref, lambda zrow, grow, size: piece(slot, zrow, grow, size).wait())
        cols = lax.broadcasted_iota(jnp.int32, (TM, Z_ROWS), 1)
        pick = jnp.where((lp_ref[:, 0:1] == cols) | (lp_ref[:, 1:2] == cols), 1.0, 0.0).astype(BF16)
        y = jnp.dot(pick, buf_ref[slot].astype(BF16), preferred_element_type=F32)
        o_ref[...] = x_ref[...] + mod_ref[5:6, :] * (_rms(y) * pg_ref[...])

    for slot in range(2):
        pl.when(i % 2 == slot)(functools.partial(step, slot))


def _combine(layer, ys, runs, lp_col, x_lat, mod, post_g):
    row = lambda i: (i, 0)
    runs_spec = lambda fwd: pl.BlockSpec(
        (None, 1, 3 * N_EXPERTS), lambda i: (jnp.minimum(i + fwd, N_TOK_TILES - 1), 0, 0), memory_space=pltpu.SMEM)
    return pl.pallas_call(
        _combine_kernel,
        grid=(N_TOK_TILES,),
        in_specs=[
            runs_spec(0), runs_spec(1),
            pl.BlockSpec(memory_space=pl.ANY),
            pl.BlockSpec((TM, 2), row),
            pl.BlockSpec((TM, D_MODEL), row),
            _mod_spec(layer, TM),
            _resident((1, D_MODEL), lambda i: (0, 0)),
        ],
        out_specs=pl.BlockSpec((TM, D_MODEL), row),
        out_shape=jax.ShapeDtypeStruct((N_LAT, D_MODEL), F32),
        scratch_shapes=[pltpu.VMEM((2, Z_ROWS, D_MODEL), F32), pltpu.SemaphoreType.DMA((2,))],
        compiler_params=_params(("arbitrary",), 40),
        name=f"moe_combine_l{layer}",
    )(runs, runs, ys, lp_col, x_lat, mod, post_g)


def kernel(x, c, ctx, c_ctx, ada_w, ada_b, pre_mix_g, post_mix_g, pre_ffn_g, post_ffn_g, w_in, w_out,
           lam_q1, lam_k1, lam_q2, lam_k2, subln_g, sg_ln_g, sg_ln_b, sg_w, sg_b,
           ffn_w1, ffn_w3, ffn_w2, router_w, moe_w1, moe_w3, moe_w2):
    x_lat = x.reshape(N_LAT, D_MODEL)
    x_ctx = ctx.reshape(N_CTX, D_MODEL)
    cc = jnp.concatenate([c, c_ctx[None, :], jnp.zeros((MOD_ROWS - BATCH - 1, D_MODEL), F32)], axis=0)
    mod = _modulation(cc, ada_w, ada_b)
    rope_tabs = _rope_tables()
    n_lat_tiles = N_LAT // TP
    x_all = None
    for l in range(DEPTH):
        last = l == DEPTH - 1
        vec = lambda a: a[l][None, :]
        if l == 0:
            xa, xb, xb_tile0 = x_lat, x_ctx, 0
        else:
            xa, xb, xb_tile0 = x_all, x_all, n_lat_tiles
        q, kt, v, u, gn = _in_proj(l, xa, xb, xb_tile0, mod, vec(pre_mix_g), w_in[l].astype(BF16),
                                   rope_tabs, vec(sg_ln_g), vec(sg_ln_b))
        lam_params = [vec(lam_q1), vec(lam_k1), vec(lam_q2), vec(lam_k2)]
        a_lat = _attention(l, q, kt, v, lam_params, vec(subln_g), latent=True)
        a_ctx = a_lat if last else _attention(l, q, kt, v, lam_params, vec(subln_g), latent=False)
        sg_bias = jnp.repeat(sg_b[l].T, CHUNK, axis=1)
        n_tiles = n_lat_tiles if last else N_ALL // TP
        x_mid = _out_proj(l, n_tiles, a_lat, a_ctx, u, gn, sg_w[l].astype(BF16), sg_bias,
                          w_out[l].astype(BF16), vec(post_mix_g), mod, xa, xb, xb_tile0)
        if l % 2 == 0:
            i = l // 2
            x_all = _ffn(l, x_mid, mod, vec(pre_ffn_g), ffn_w1[i].astype(BF16), ffn_w3[i].astype(BF16),
                         ffn_w2[i].astype(BF16), vec(post_ffn_g))
        else:
            i = l // 2
            ids, rank, gate, cnt = _router(l, x_mid, mod, vec(pre_ffn_g), router_w[i])
            runs, lp_col, lp_row, gate_row, texp, tvalid, pad_lo, pad_hi = _routing_tables(ids, rank, gate, cnt)
            xs = _dispatch(l, x_mid, mod, vec(pre_ffn_g), runs, lp_row, gate_row, pad_lo, pad_hi)
            ys = _experts(l, xs, texp, tvalid, moe_w1[i].astype(BF16), moe_w3[i].astype(BF16),
                          moe_w2[i].astype(BF16))
            x_all = _combine(l, ys, runs, lp_col, x_mid, mod, vec(post_ffn_g))
    return x_all.reshape(BATCH, SEQ, D_MODEL)
```

```python
import functools
import math

import jax
import jax.numpy as jnp
from jax import lax
from jax.experimental import pallas as pl
from jax.experimental.pallas import tpu as pltpu

D_MODEL = 1024
BATCH = 16
SEQ = 2048
DEPTH = 2
GRID_W = 64
CTX_LEN = 256
DA_WIDTH = 512
DA_HEADS = 4
DA_VDIM = 128
DA_HEAD_DIM = 64
SG_WIDTH = 512
SG_GROUPS = 4
CHUNK = 128
ROPE_BASE = 10000.0
D_FF = 2816
N_EXPERTS = 8
D_EXPERT = 3584
EPS = 1e-6
IN_COLS = 3 * DA_WIDTH + 2 * SG_WIDTH

N_LAT = BATCH * SEQ
N_CTX = BATCH * CTX_LEN
N_ALL = N_LAT + N_CTX
MOD_ROWS = 32

TP = 1024
TM = 512
TQ = 1024
TQ_SUB = 512
KEY_SEG = 2048
TF_DENSE = 256
TF_MOE = 1792
N_TOK_TILES = N_LAT // TM
ROW_ALIGN = 8
RUN_SIZES = (512, 256, 128, 64, 32, 16, 8)
Z_ROWS = 2 * TM + N_EXPERTS * ROW_ALIGN
D_XS = D_MODEL + 128
XS_TILES = (2 * N_LAT + N_TOK_TILES * N_EXPERTS * (ROW_ALIGN - 1)) // TM + 1 + N_EXPERTS
MIB = 2 ** 20

BF16 = jnp.bfloat16
F32 = jnp.float32


def _rms(x):
    return x * lax.rsqrt(jnp.mean(x * x, axis=-1, keepdims=True) + EPS)


def _silu(x):
    return x * jax.nn.sigmoid(x)


def _resident(block_shape, index_map):
    return pl.BlockSpec(block_shape, index_map, pipeline_mode=pl.Buffered(1))


def _params(semantics, vmem_mib):
    return pltpu.CompilerParams(dimension_semantics=semantics, vmem_limit_bytes=vmem_mib * MIB)


def _mod_kernel(cc_ref, w_ref, b_ref, o_ref):
    a = _silu(cc_ref[...]).astype(BF16)
    o_ref[...] = jnp.dot(a, w_ref[...].astype(BF16), preferred_element_type=F32) + b_ref[...]


def _modulation(cc, ada_w, ada_b):
    tn = 1536
    out = pl.pallas_call(
        _mod_kernel,
        grid=(DEPTH, 6 * D_MODEL // tn),
        in_specs=[
            pl.BlockSpec((MOD_ROWS, D_MODEL), lambda l, j: (0, 0)),
            pl.BlockSpec((None, D_MODEL, tn), lambda l, j: (l, 0, j)),
            pl.BlockSpec((None, 1, tn), lambda l, j: (l, 0, j)),
        ],
        out_specs=pl.BlockSpec((None, MOD_ROWS, tn), lambda l, j: (l, 0, j)),
        out_shape=jax.ShapeDtypeStruct((DEPTH, MOD_ROWS, 6 * D_MODEL), F32),
        compiler_params=_params(("parallel", "parallel"), 40),
        name="modulation",
    )(cc, ada_w, ada_b.reshape(DEPTH, 1, 6 * D_MODEL))
    return out.reshape(DEPTH, MOD_ROWS, 6, D_MODEL)


def _mod_spec(layer, tile):
    tiles_per_batch = SEQ // tile
    return pl.BlockSpec((None, None, 6, D_MODEL),
                        lambda i: (layer, jnp.minimum(i // tiles_per_batch, BATCH), 0, 0))


def _in_proj_kernel(n_lat_tiles, xa_ref, xb_ref, mod_ref, g_ref, w_ref, cos_ref, sa_ref, sb_ref,
                    lng_ref, lnb_ref, q_ref, kt_ref, v_ref, u_ref, gn_ref):
    i = pl.program_id(0)
    x = jnp.where(i < n_lat_tiles, xa_ref[...], xb_ref[...])
    h = _rms(x) * g_ref[...]
    h = h * (1.0 + mod_ref[1:2, :]) + mod_ref[0:1, :]
    p = jnp.dot(h.astype(BF16), w_ref[...], preferred_element_type=F32)

    cos, sa, sb = cos_ref[...], sa_ref[...], sb_ref[...]

    def rope(t):
        return t * cos + pltpu.roll(t, 112, 1) * sa + pltpu.roll(t, 16, 1) * sb

    for hd in range(DA_HEADS):
        lo, hi = hd * DA_VDIM, (hd + 1) * DA_VDIM
        q_ref[:, lo:hi] = (rope(p[:, lo:hi]) * (DA_HEAD_DIM ** -0.5)).astype(BF16)
        kt_ref[lo:hi, :] = rope(p[:, DA_WIDTH + lo:DA_WIDTH + hi]).T.astype(BF16)
    v_ref[...] = p[:, 2 * DA_WIDTH:3 * DA_WIDTH].astype(BF16)
    u_ref[...] = p[:, 3 * DA_WIDTH:3 * DA_WIDTH + SG_WIDTH].astype(BF16)
    gv = p[:, 3 * DA_WIDTH + SG_WIDTH:]
    mu = jnp.mean(gv, axis=-1, keepdims=True)
    var = jnp.mean(jnp.square(gv - mu), axis=-1, keepdims=True)
    gn_ref[...] = ((gv - mu) * lax.rsqrt(var + EPS) * lng_ref[...] + lnb_ref[...]).astype(BF16)


def _in_proj(layer, xa, xb, xb_tile0, mod, pre_g, w_in, rope_tabs, ln_g, ln_b):
    n_lat_tiles = N_LAT // TP
    n_tiles = N_ALL // TP
    tiles_per_batch = SEQ // TP
    tab_spec = pl.BlockSpec(
        (TP, DA_VDIM), lambda i: (jnp.where(i < n_lat_tiles, i % tiles_per_batch, tiles_per_batch), 0))
    row = lambda i: (i, 0)
    outs = pl.pallas_call(
        functools.partial(_in_proj_kernel, n_lat_tiles),
        grid=(n_tiles,),
        in_specs=[
            pl.BlockSpec((TP, D_MODEL), lambda i: (jnp.minimum(i, n_lat_tiles - 1), 0)),
            pl.BlockSpec((TP, D_MODEL), lambda i: (xb_tile0 + jnp.maximum(i - n_lat_tiles, 0), 0)),
            _mod_spec(layer, TP),
            _resident((1, D_MODEL), lambda i: (0, 0)),
            _resident((D_MODEL, IN_COLS), lambda i: (0, 0)),
            tab_spec, tab_spec, tab_spec,
            _resident((1, SG_WIDTH), lambda i: (0, 0)),
            _resident((1, SG_WIDTH), lambda i: (0, 0)),
        ],
        out_specs=[
            pl.BlockSpec((TP, DA_WIDTH), row),
            pl.BlockSpec((DA_WIDTH, TP), lambda i: (0, i)),
            pl.BlockSpec((TP, DA_WIDTH), row),
            pl.BlockSpec((TP, SG_WIDTH), row),
            pl.BlockSpec((TP, SG_WIDTH), row),
        ],
        out_shape=[
            jax.ShapeDtypeStruct((N_ALL, DA_WIDTH), BF16),
            jax.ShapeDtypeStruct((DA_WIDTH, N_ALL), BF16),
            jax.ShapeDtypeStruct((N_ALL, DA_WIDTH), BF16),
            jax.ShapeDtypeStruct((N_ALL, SG_WIDTH), BF16),
            jax.ShapeDtypeStruct((N_ALL, SG_WIDTH), BF16),
        ],
        compiler_params=_params(("parallel",), 48),
        name=f"in_proj_l{layer}",
    )(xa, xb, mod, pre_g, w_in, *rope_tabs, ln_g, ln_b)
    return outs


def _rope_tables():
    pos = jnp.arange(SEQ, dtype=jnp.int32)
    row = (pos // GRID_W).astype(F32)
    col = (pos % GRID_W).astype(F32)
    half = DA_HEAD_DIM // 2
    inv = ROPE_BASE ** (-jnp.arange(0, half, 2, dtype=F32) / half)
    ang_r = row[:, None] * inv
    ang_c = col[:, None] * inv
    ang = jnp.concatenate([ang_r, ang_r, ang_c, ang_c], axis=-1)
    cos = jnp.tile(jnp.cos(ang), (1, 2))
    sin = jnp.tile(jnp.sin(ang), (1, 2))
    first = (jnp.arange(DA_VDIM) % 32) < 16
    sa = jnp.where(first, -sin, 0.0)
    sb = jnp.where(first, 0.0, sin)
    ident = jnp.zeros((TP, DA_VDIM), F32)
    return (jnp.concatenate([cos, ident + 1.0], axis=0),
            jnp.concatenate([sa, ident], axis=0),
            jnp.concatenate([sb, ident], axis=0))


def _attn_kernel(n_src, n_cast, lambda_init, q_ref, *refs):
    kt_refs = refs[:n_src]
    v_refs = refs[n_src:2 * n_src]
    lq1_ref, lk1_ref, lq2_ref, lk2_ref, subg_ref = refs[2 * n_src:2 * n_src + 5]
    n_in = 2 * n_src + 5
    cast_in_refs = refs[n_in:n_in + n_cast]
    o_ref = refs[n_in + n_cast]
    cast_out_refs = refs[n_in + n_cast + 1:n_in + 2 * n_cast + 1]
    vext_refs = refs[-n_src:]

    for src_ref, dst_ref in zip(cast_in_refs, cast_out_refs):
        dst_ref[...] = src_ref[...].astype(BF16)

    @pl.when(pl.program_id(2) == 0)
    def _():
        for v_ref, ve_ref in zip(v_refs, vext_refs):
            ve_ref[:, :DA_VDIM] = v_ref[...]
            ve_ref[:, DA_VDIM:] = jnp.ones((v_ref.shape[0], DA_VDIM), BF16)

    segs = []
    for kt_ref, ve_ref in zip(kt_refs, vext_refs):
        n_keys = ve_ref.shape[0]
        for lo in range(0, n_keys, KEY_SEG):
            segs.append((kt_ref, ve_ref, lo, min(lo + KEY_SEG, n_keys)))

    lam = (jnp.exp(jnp.sum(lq1_ref[...] * lk1_ref[...], axis=-1, keepdims=True))
           - jnp.exp(jnp.sum(lq2_ref[...] * lk2_ref[...], axis=-1, keepdims=True)) + lambda_init)
    tq_sub = min(TQ_SUB, q_ref.shape[0])

    def scores(r0):
        q = q_ref[r0:r0 + tq_sub, :]
        first = lax.broadcasted_iota(jnp.int32, q.shape, 1) < DA_HEAD_DIM
        zero = jnp.zeros_like(q)
        qms = (jnp.where(first, q, zero), jnp.where(first, zero, q))
        return [[jnp.dot(qm, kt_ref[:, lo:hi], preferred_element_type=F32) for kt_ref, _, lo, hi in segs]
                for qm in qms]

    def finish(r0, ss_maps):
        maps = []
        for ss in ss_maps:
            m = functools.reduce(jnp.maximum, [jnp.max(s, axis=-1, keepdims=True) for s in ss])
            ts = [jnp.dot(jnp.exp(s - m).astype(BF16), ve_ref[lo:hi, :], preferred_element_type=F32)
                  for s, (_, ve_ref, lo, hi) in zip(ss, segs)]
            acc = functools.reduce(jnp.add, ts)
            maps.append(acc[:, :DA_VDIM] / acc[:, DA_VDIM:])
        o = maps[0] - lam * maps[1]
        o_ref[r0:r0 + tq_sub, :] = (_rms(o) * subg_ref[...] * (1.0 - lambda_init)).astype(BF16)

    starts = list(range(0, q_ref.shape[0], tq_sub))
    pending = scores(starts[0])
    for idx, r0 in enumerate(starts):
        nxt = scores(starts[idx + 1]) if idx + 1 < len(starts) else None
        finish(r0, pending)
        pending = nxt


def _attention(layer, q, kt, v, lam_params, subg, latent, casts=()):
    lambda_init = 0.8 - 0.6 * math.exp(-0.3 * layer)
    ctx_blk0 = N_LAT // CTX_LEN
    small = lambda: pl.BlockSpec((1, DA_HEAD_DIM), lambda b, h, t: (0, 0))
    tail_specs = [small(), small(), small(), small(), pl.BlockSpec((1, DA_VDIM), lambda b, h, t: (0, 0))]
    kt_ctx = pl.BlockSpec((DA_VDIM, CTX_LEN), lambda b, h, t: (h, ctx_blk0 + b))
    v_ctx = pl.BlockSpec((CTX_LEN, DA_VDIM), lambda b, h, t: (ctx_blk0 + b, h))
    if latent:
        tq, n_q, n_rows = TQ, SEQ // TQ, N_LAT
        q_map = lambda b, h, t: (b * n_q + t, h)
        o_map = q_map
        kt_specs = [kt_ctx, pl.BlockSpec((DA_VDIM, SEQ), lambda b, h, t: (h, b))]
        v_specs = [v_ctx, pl.BlockSpec((SEQ, DA_VDIM), lambda b, h, t: (b, h))]
        key_lens = [CTX_LEN, SEQ]
    else:
        tq, n_q, n_rows = CTX_LEN, 1, N_CTX
        q_map = lambda b, h, t: (ctx_blk0 + b, h)
        o_map = lambda b, h, t: (b, h)
        kt_specs, v_specs, key_lens = [kt_ctx], [v_ctx], [CTX_LEN]
    n_src = len(key_lens)
    n_steps = BATCH * DA_HEADS * n_q
    cast_specs = []
    for w in casts:
        rows = next(r for r in range(16, w.shape[0] + 1, 16) if w.shape[0] % r == 0 and w.shape[0] // r <= n_steps)
        last = w.shape[0] // rows - 1
        cast_specs.append(pl.BlockSpec(
            (rows, w.shape[1]),
            lambda b, h, t, last=last: (jnp.minimum((b * DA_HEADS + h) * n_q + t, last), 0)))
    outs = pl.pallas_call(
        functools.partial(_attn_kernel, n_src, len(casts), lambda_init),
        grid=(BATCH, DA_HEADS, n_q),
        in_specs=[pl.BlockSpec((tq, DA_VDIM), q_map)] + kt_specs + v_specs + tail_specs + cast_specs,
        out_specs=[pl.BlockSpec((tq, DA_VDIM), o_map)] + cast_specs,
        out_shape=[jax.ShapeDtypeStruct((n_rows, DA_WIDTH), BF16)]
                  + [jax.ShapeDtypeStruct(w.shape, BF16) for w in casts],
        scratch_shapes=[pltpu.VMEM((n, 2 * DA_VDIM), BF16) for n in key_lens],
        compiler_params=_params(("parallel", "parallel", "arbitrary"), 56),
        name=f"attn_l{layer}_{'lat' if latent else 'ctx'}",
    )(q, *([kt] * n_src), *([v] * n_src), *lam_params, subg, *casts)
    return outs[0], outs[1:]


def _out_proj_kernel(n_lat_tiles, aa_ref, ab_ref, u_ref, gn_ref, sgw_ref, sgb_ref, w_ref, pg_ref, mod_ref,
                     xa_ref, xb_ref, o_ref, s_ref):
    i = pl.program_id(0)
    a = jnp.where(i < n_lat_tiles, aa_ref[...], ab_ref[...])
    for c in range(TP // CHUNK):
        rows = slice(c * CHUNK, (c + 1) * CHUNK)
        for g in range(SG_GROUPS):
            cols = slice(g * CHUNK, (g + 1) * CHUNK)
            mixed = jnp.dot(sgw_ref[g], gn_ref[rows, cols], preferred_element_type=F32) + sgb_ref[:, cols]
            s_ref[rows, cols] = (u_ref[rows, cols].astype(F32) * mixed).astype(BF16)
    o = (jnp.dot(a, w_ref[:DA_WIDTH, :], preferred_element_type=F32)
         + jnp.dot(s_ref[...], w_ref[DA_WIDTH:, :], preferred_element_type=F32))
    x = jnp.where(i < n_lat_tiles, xa_ref[...], xb_ref[...])
    o_ref[...] = x + mod_ref[2:3, :] * (_rms(o) * pg_ref[...])


def _out_proj(layer, n_tiles, a_lat, a_ctx, u, gn, sg_w, sg_bias, w_out, post_g, mod, xa, xb, xb_tile0):
    n_lat_tiles = N_LAT // TP
    row = lambda i: (i, 0)
    half = lambda: pl.BlockSpec((TP, DA_WIDTH), row)
    return pl.pallas_call(
        functools.partial(_out_proj_kernel, n_lat_tiles),
        grid=(n_tiles,),
        in_specs=[
            pl.BlockSpec((TP, DA_WIDTH), lambda i: (jnp.minimum(i, n_lat_tiles - 1), 0)),
            pl.BlockSpec((TP, DA_WIDTH), lambda i: (jnp.maximum(i - n_lat_tiles, 0), 0)),
            half(), half(),
            _resident((SG_GROUPS, CHUNK, CHUNK), lambda i: (0, 0, 0)),
            _resident((CHUNK, SG_WIDTH), lambda i: (0, 0)),
            _resident((D_MODEL, D_MODEL), lambda i: (0, 0)),
            _resident((1, D_MODEL), lambda i: (0, 0)),
            _mod_spec(layer, TP),
            pl.BlockSpec((TP, D_MODEL), lambda i: (jnp.minimum(i, n_lat_tiles - 1), 0)),
            pl.BlockSpec((TP, D_MODEL), lambda i: (xb_tile0 + jnp.maximum(i - n_lat_tiles, 0), 0)),
        ],
        out_specs=pl.BlockSpec((TP, D_MODEL), row),
        out_shape=jax.ShapeDtypeStruct((n_tiles * TP, D_MODEL), F32),
        scratch_shapes=[pltpu.VMEM((TP, SG_WIDTH), BF16)],
        compiler_params=_params(("parallel",), 48),
        name=f"out_proj_l{layer}",
    )(a_lat, a_ctx, u, gn, sg_w, sg_bias, w_out, post_g, mod, xa, xb)


def _ffn_kernel(x_ref, mod_ref, g_ref, w1_ref, w3_ref, w2_ref, pg_ref, o_ref):
    x = x_ref[...]
    f = ((_rms(x) * g_ref[...]) * (1.0 + mod_ref[4:5, :]) + mod_ref[3:4, :]).astype(BF16)
    y = None
    for k in range(D_FF // TF_DENSE):
        cols = slice(k * TF_DENSE, (k + 1) * TF_DENSE)
        h1 = jnp.dot(f, w1_ref[:, cols], preferred_element_type=F32)
        h3 = jnp.dot(f, w3_ref[:, cols], preferred_element_type=F32)
        t = jnp.dot((_silu(h1) * h3).astype(BF16), w2_ref[cols, :], preferred_element_type=F32)
        y = t if y is None else y + t
    o_ref[...] = x + mod_ref[5:6, :] * (_rms(y) * pg_ref[...])


def _ffn(layer, x_all, mod, pre_g, w1, w3, w2, post_g):
    row = lambda i: (i, 0)
    return pl.pallas_call(
        _ffn_kernel,
        grid=(N_ALL // TP,),
        in_specs=[
            pl.BlockSpec((TP, D_MODEL), row),
            _mod_spec(layer, TP),
            _resident((1, D_MODEL), lambda i: (0, 0)),
            _resident((D_MODEL, D_FF), lambda i: (0, 0)),
            _resident((D_MODEL, D_FF), lambda i: (0, 0)),
            _resident((D_FF, D_MODEL), lambda i: (0, 0)),
            _resident((1, D_MODEL), lambda i: (0, 0)),
        ],
        out_specs=pl.BlockSpec((TP, D_MODEL), row),
        out_shape=jax.ShapeDtypeStruct((N_ALL, D_MODEL), F32),
        compiler_params=_params(("parallel",), 56),
        name=f"ffn_l{layer}",
    )(x_all, mod, pre_g, w1, w3, w2, post_g)


def _ffn_input(x, mod_ref, g_ref):
    return (_rms(x) * g_ref[...]) * (1.0 + mod_ref[4:5, :]) + mod_ref[3:4, :]


def _router_kernel(x_ref, mod_ref, g_ref, rw_ref, ids_ref, rank_ref, gate_ref, cnt_ref):
    f = _ffn_input(x_ref[...], mod_ref, g_ref)
    rw = rw_ref[...]
    f_hi, rw_hi = f.astype(BF16), rw.astype(BF16)
    f_lo, rw_lo = (f - f_hi.astype(F32)).astype(BF16), (rw - rw_hi.astype(F32)).astype(BF16)
    logits = (jnp.dot(f_hi, rw_hi, preferred_element_type=F32) + jnp.dot(f_lo, rw_hi, preferred_element_type=F32)
              + jnp.dot(f_hi, rw_lo, preferred_element_type=F32))
    lane = lax.broadcasted_iota(jnp.int32, logits.shape, 1)
    v1 = jnp.max(logits, axis=-1, keepdims=True)
    i1 = jnp.min(jnp.where(logits == v1, lane, N_EXPERTS), axis=-1, keepdims=True)
    rest = jnp.where(lane == i1, -jnp.inf, logits)
    v2 = jnp.max(rest, axis=-1, keepdims=True)
    i2 = jnp.min(jnp.where(rest == v2, lane, N_EXPERTS), axis=-1, keepdims=True)
    e = jnp.exp(v2 - v1)
    w1 = 1.0 / (1.0 + e)

    sel1, sel2 = lane == i1, lane == i2
    onehot = jnp.where(sel1 | sel2, 1.0, 0.0)
    r = lax.broadcasted_iota(jnp.int32, (TM, TM), 0)
    c = lax.broadcasted_iota(jnp.int32, (TM, TM), 1)
    tri = jnp.where(c < r, 1.0, 0.0).astype(BF16)
    prefix = jnp.dot(tri, onehot.astype(BF16), preferred_element_type=F32)
    rank1 = jnp.sum(jnp.where(sel1, prefix, 0.0), axis=-1, keepdims=True)
    rank2 = jnp.sum(jnp.where(sel2, prefix, 0.0), axis=-1, keepdims=True)

    slot0 = lax.broadcasted_iota(jnp.int32, (TM, 2), 1) == 0
    ids_ref[...] = jnp.where(slot0, i1, i2)
    rank_ref[...] = jnp.where(slot0, rank1, rank2).astype(jnp.int32)
    gate_ref[...] = jnp.where(slot0, w1, e * w1)
    cnt_ref[...] = jnp.sum(onehot, axis=0, keepdims=True).astype(jnp.int32)


def _router(layer, x_lat, mod, pre_g, router_w):
    row = lambda i: (i, 0)
    pair = lambda: pl.BlockSpec((TM, 2), row)
    return pl.pallas_call(
        _router_kernel,
        grid=(N_TOK_TILES,),
        in_specs=[
            pl.BlockSpec((TM, D_MODEL), row),
            _mod_spec(layer, TM),
            _resident((1, D_MODEL), lambda i: (0, 0)),
            _resident((D_MODEL, N_EXPERTS), lambda i: (0, 0)),
        ],
        out_specs=[pair(), pair(), pair(), pl.BlockSpec((None, 1, N_EXPERTS), lambda i: (i, 0, 0))],
        out_shape=[jax.ShapeDtypeStruct((N_LAT, 2), jnp.int32),
                   jax.ShapeDtypeStruct((N_LAT, 2), jnp.int32),
                   jax.ShapeDtypeStruct((N_LAT, 2), F32),
                   jax.ShapeDtypeStruct((N_TOK_TILES, 1, N_EXPERTS), jnp.int32)],
        compiler_params=_params(("parallel",), 32),
        name=f"router_l{layer}",
    )(x_lat, mod, pre_g, router_w)


def _routing_tables(ids, rank, gate, cnt):
    cnt = cnt.reshape(N_TOK_TILES, N_EXPERTS)
    run = (cnt + ROW_ALIGN - 1) // ROW_ALIGN * ROW_ALIGN
    zoff = jnp.cumsum(run, axis=1) - run
    grp = jnp.sum(run, axis=0)
    tiles = (grp + TM - 1) // TM
    tile_end = jnp.cumsum(tiles)
    off = (tile_end - tiles) * TM
    gstart = off[None, :] + jnp.cumsum(run, axis=0) - run
    runs = jnp.concatenate([run, zoff, gstart], axis=1).astype(jnp.int32).reshape(N_TOK_TILES, 1, 3 * N_EXPERTS)

    ids3 = ids.reshape(N_TOK_TILES, TM, 2)
    hit = ids3[..., None] == jnp.arange(N_EXPERTS, dtype=jnp.int32)
    lp = jnp.sum(jnp.where(hit, zoff[:, None, None, :], 0), axis=-1) + rank.reshape(N_TOK_TILES, TM, 2)
    lp = lp.astype(jnp.int32)
    lp_col = lp.reshape(N_LAT, 2)
    lp_row = jnp.transpose(lp, (0, 2, 1))
    gate_row = jnp.transpose(gate.reshape(N_TOK_TILES, TM, 2), (0, 2, 1))

    j = jnp.arange(XS_TILES, dtype=jnp.int32)
    n_used = tile_end[-1]
    valid = (j < n_used).astype(jnp.int32)
    jc = jnp.minimum(j, n_used - 1)
    texp = jnp.minimum(jnp.sum(jc[:, None] >= tile_end[None, :], axis=1), N_EXPERTS - 1).astype(jnp.int32)
    pad_lo = (off + grp).astype(jnp.int32)
    pad_hi = (tile_end * TM).astype(jnp.int32).at[N_EXPERTS - 1].set(XS_TILES * TM)
    return runs, lp_col, lp_row, gate_row, texp, valid, pad_lo, pad_hi


def _for_each_run_piece(runs_ref, fn):
    for e in range(N_EXPERTS):
        n = runs_ref[0, e]
        zo = runs_ref[0, N_EXPERTS + e]
        go = runs_ref[0, 2 * N_EXPERTS + e]
        for size in RUN_SIZES:
            done = n & ~(2 * size - 1)
            pl.when((n & size) != 0)(functools.partial(
                fn, pl.multiple_of(zo + done, ROW_ALIGN), pl.multiple_of(go + done, ROW_ALIGN), size))


def _dispatch_kernel(pad_lo_ref, pad_hi_ref, runs_ref, runs_prev_ref, lp_ref, gate_ref, x_ref, mod_ref, g_ref,
                     xs_ref, z_ref, zero_ref, sems, zsem):
    i = pl.program_id(0)
    n = pl.num_programs(0)

    @pl.when(i == 0)
    def _():
        zero_ref[...] = jnp.zeros_like(zero_ref)
        for e in range(N_EXPERTS):
            lo, hi = pad_lo_ref[e] // ROW_ALIGN, pad_hi_ref[e] // ROW_ALIGN
            blk = lambda r: pltpu.make_async_copy(
                zero_ref, xs_ref.at[pl.ds(pl.multiple_of(r * ROW_ALIGN, ROW_ALIGN), ROW_ALIGN)], zsem)
            lax.fori_loop(lo, hi, lambda r, _: blk(r).start(), None)
            lax.fori_loop(lo, hi, lambda r, _: blk(r).wait(), None)

    def piece(slot, zrow, grow, size):
        return pltpu.make_async_copy(z_ref.at[slot, pl.ds(zrow, size)], xs_ref.at[pl.ds(grow, size)], sems.at[slot])

    def step(slot):
        f = _ffn_input(x_ref[...], mod_ref, g_ref).astype(BF16)
        rows = lax.broadcasted_iota(jnp.int32, (Z_ROWS, TM), 0)
        hit0, hit1 = lp_ref[0:1, :] == rows, lp_ref[1:2, :] == rows
        sel = jnp.where(hit0 | hit1, 1.0, 0.0).astype(BF16)
        z_ref[slot, :, :D_MODEL] = jnp.dot(sel, f, preferred_element_type=F32)
        gate = jnp.sum(jnp.where(hit0, gate_ref[0:1, :], 0.0) + jnp.where(hit1, gate_ref[1:2, :], 0.0),
                       axis=-1, keepdims=True)
        z_ref[slot, :, D_MODEL:] = jnp.broadcast_to(gate, (Z_ROWS, D_XS - D_MODEL))
        _for_each_run_piece(runs_ref, lambda zrow, grow, size: piece(slot, zrow, grow, size).start())

        @pl.when(i >= 1)
        def _():
            _for_each_run_piece(runs_prev_ref, lambda zrow, grow, size: piece(1 - slot, zrow, grow, size).wait())

        @pl.when(i == n - 1)
        def _():
            _for_each_run_piece(runs_ref, lambda zrow, grow, size: piece(slot, zrow, grow, size).wait())

    for slot in range(2):
        pl.when(i % 2 == slot)(functools.partial(step, slot))


def _dispatch(layer, x_lat, mod, pre_g, runs, lp_row, gate_row, pad_lo, pad_hi):
    smem = lambda: pl.BlockSpec(memory_space=pltpu.SMEM)
    runs_spec = lambda back: pl.BlockSpec((None, 1, 3 * N_EXPERTS), lambda i: (jnp.maximum(i - back, 0), 0, 0),
                                          memory_space=pltpu.SMEM)
    return pl.pallas_call(
        _dispatch_kernel,
        grid=(N_TOK_TILES,),
        in_specs=[
            smem(), smem(), runs_spec(0), runs_spec(1),
            pl.BlockSpec((None, 2, TM), lambda i: (i, 0, 0)),
            pl.BlockSpec((None, 2, TM), lambda i: (i, 0, 0)),
            pl.BlockSpec((TM, D_MODEL), lambda i: (i, 0)),
            _mod_spec(layer, TM),
            _resident((1, D_MODEL), lambda i: (0, 0)),
        ],
        out_specs=pl.BlockSpec(memory_space=pl.ANY),
        out_shape=jax.ShapeDtypeStruct((XS_TILES * TM, D_XS), F32),
        scratch_shapes=[pltpu.VMEM((2, Z_ROWS, D_XS), F32), pltpu.VMEM((ROW_ALIGN, D_XS), F32),
                        pltpu.SemaphoreType.DMA((2,)), pltpu.SemaphoreType.DMA(())],
        compiler_params=_params(("arbitrary",), 48),
        name=f"moe_dispatch_l{layer}",
    )(pad_lo, pad_hi, runs, runs, lp_row, gate_row, x_lat, mod, pre_g)


def _expert_kernel(texp_ref, tvalid_ref, xs_ref, w1_ref, w3_ref, w2_ref, ys_ref, xb_ref, acc_ref):
    j = pl.program_id(0)
    k = pl.program_id(1)

    @pl.when((tvalid_ref[j] == 0) & (k == 0))
    def _():
        ys_ref[...] = jnp.zeros_like(ys_ref)

    @pl.when(tvalid_ref[j] == 1)
    def _():
        @pl.when(k == 0)
        def _():
            xb_ref[...] = xs_ref[:, :D_MODEL].astype(BF16)
            acc_ref[...] = jnp.zeros_like(acc_ref)

        xb = xb_ref[...]
        h1 = jnp.dot(xb, w1_ref[...], preferred_element_type=F32)
        h3 = jnp.dot(xb, w3_ref[...], preferred_element_type=F32)
        acc_ref[...] += jnp.dot((_silu(h1) * h3).astype(BF16), w2_ref[...], preferred_element_type=F32)

        @pl.when(k == pl.num_programs(1) - 1)
        def _():
            ys_ref[...] = acc_ref[...] * xs_ref[:, D_MODEL:D_MODEL + 1]


def _experts(layer, xs, texp, tvalid, w1, w3, w2):
    n_k = D_EXPERT // TF_MOE
    row = lambda j, k, texp, tvalid: (j, 0)
    kk = lambda j, k, tvalid: jnp.where(tvalid[j] == 1, k, n_k - 1)
    return pl.pallas_call(
        _expert_kernel,
        grid_spec=pltpu.PrefetchScalarGridSpec(
            num_scalar_prefetch=2,
            grid=(XS_TILES, n_k),
            in_specs=[
                pl.BlockSpec((TM, D_XS), row),
                pl.BlockSpec((None, D_MODEL, TF_MOE), lambda j, k, texp, tvalid: (texp[j], 0, kk(j, k, tvalid))),
                pl.BlockSpec((None, D_MODEL, TF_MOE), lambda j, k, texp, tvalid: (texp[j], 0, kk(j, k, tvalid))),
                pl.BlockSpec((None, TF_MOE, D_MODEL), lambda j, k, texp, tvalid: (texp[j], kk(j, k, tvalid), 0)),
            ],
            out_specs=pl.BlockSpec((TM, D_MODEL), row),
            scratch_shapes=[pltpu.VMEM((TM, D_MODEL), BF16), pltpu.VMEM((TM, D_MODEL), F32)],
        ),
        out_shape=jax.ShapeDtypeStruct((XS_TILES * TM, D_MODEL), F32),
        compiler_params=_params(("arbitrary", "arbitrary"), 56),
        name=f"moe_experts_l{layer}",
    )(texp, tvalid, xs, w1, w3, w2)


def _combine_kernel(runs_ref, runs_next_ref, ys_ref, lp_ref, x_ref, mod_ref, pg_ref, o_ref, buf_ref, sems):
    i = pl.program_id(0)
    n = pl.num_programs(0)

    def piece(slot, zrow, grow, size):
        return pltpu.make_async_copy(ys_ref.at[pl.ds(grow, size)], buf_ref.at[slot, pl.ds(zrow, size)], sems.at[slot])

    def step(slot):
        @pl.when(i == 0)
        def _():
            buf_ref[...] = jnp.zeros_like(buf_ref)
            _for_each_run_piece(runs_ref, lambda zrow, grow, size: piece(slot, zrow, grow, size).start())

        @pl.when(i + 1 < n)
        def _():
            _for_each_run_piece(runs_next_ref, lambda zrow, grow, size: piece(1 - slot, zrow, grow, size).start())

        _for_each_run_piece(runs_ref, lambda zrow, grow, size: piece(slot, zrow, grow, size).wait())
        cols = lax.broadcasted_iota(jnp.int32, (TM, Z_ROWS), 1)
        pick = jnp.where((lp_ref[:, 0:1] == cols) | (lp_ref[:, 1:2] == cols), 1.0, 0.0).astype(BF16)
        y = jnp.dot(pick, buf_ref[slot].astype(BF16), preferred_element_type=F32)
        o_ref[...] = x_ref[...] + mod_ref[5:6, :] * (_rms(y) * pg_ref[...])

    for slot in range(2):
        pl.when(i % 2 == slot)(functools.partial(step, slot))


def _combine(layer, ys, runs, lp_col, x_lat, mod, post_g):
    row = lambda i: (i, 0)
    runs_spec = lambda fwd: pl.BlockSpec(
        (None, 1, 3 * N_EXPERTS), lambda i: (jnp.minimum(i + fwd, N_TOK_TILES - 1), 0, 0), memory_space=pltpu.SMEM)
    return pl.pallas_call(
        _combine_kernel,
        grid=(N_TOK_TILES,),
        in_specs=[
            runs_spec(0), runs_spec(1),
            pl.BlockSpec(memory_space=pl.ANY),
            pl.BlockSpec((TM, 2), row),
            pl.BlockSpec((TM, D_MODEL), row),
            _mod_spec(layer, TM),
            _resident((1, D_MODEL), lambda i: (0, 0)),
        ],
        out_specs=pl.BlockSpec((TM, D_MODEL), row),
        out_shape=jax.ShapeDtypeStruct((N_LAT, D_MODEL), F32),
        scratch_shapes=[pltpu.VMEM((2, Z_ROWS, D_MODEL), F32), pltpu.SemaphoreType.DMA((2,))],
        compiler_params=_params(("arbitrary",), 40),
        name=f"moe_combine_l{layer}",
    )(runs, runs, ys, lp_col, x_lat, mod, post_g)


def kernel(x, c, ctx, c_ctx, ada_w, ada_b, pre_mix_g, post_mix_g, pre_ffn_g, post_ffn_g, w_in, w_out,
           lam_q1, lam_k1, lam_q2, lam_k2, subln_g, sg_ln_g, sg_ln_b, sg_w, sg_b,
           ffn_w1, ffn_w3, ffn_w2, router_w, moe_w1, moe_w3, moe_w2):
    x_lat = x.reshape(N_LAT, D_MODEL)
    x_ctx = ctx.reshape(N_CTX, D_MODEL)
    cc = jnp.concatenate([c, c_ctx[None, :], jnp.zeros((MOD_ROWS - BATCH - 1, D_MODEL), F32)], axis=0)
    mod = _modulation(cc, ada_w, ada_b)
    rope_tabs = _rope_tables()
    n_lat_tiles = N_LAT // TP
    x_all = None
    for l in range(DEPTH):
        last = l == DEPTH - 1
        vec = lambda a: a[l][None, :]
        if l == 0:
            xa, xb, xb_tile0 = x_lat, x_ctx, 0
        else:
            xa, xb, xb_tile0 = x_all, x_all, n_lat_tiles
        q, kt, v, u, gn = _in_proj(l, xa, xb, xb_tile0, mod, vec(pre_mix_g), w_in[l].astype(BF16),
                                   rope_tabs, vec(sg_ln_g), vec(sg_ln_b))
        lam_params = [vec(lam_q1), vec(lam_k1), vec(lam_q2), vec(lam_k2)]
        casts = [w.reshape(-1, w.shape[-1]) for w in (ffn_w1, ffn_w3, ffn_w2, moe_w1, moe_w3, moe_w2)] if l == 0 else []
        a_lat, cast_out = _attention(l, q, kt, v, lam_params, vec(subln_g), latent=True, casts=casts)
        if l == 0:
            ffn_bf16 = [w.reshape(o.shape) for w, o in zip(cast_out[:3], (ffn_w1, ffn_w3, ffn_w2))]
            moe_bf16 = [w.reshape(o.shape) for w, o in zip(cast_out[3:], (moe_w1, moe_w3, moe_w2))]
        a_ctx = a_lat if last else _attention(l, q, kt, v, lam_params, vec(subln_g), latent=False)[0]
        sg_bias = jnp.repeat(sg_b[l].T, CHUNK, axis=1)
        n_tiles = n_lat_tiles if last else N_ALL // TP
        x_mid = _out_proj(l, n_tiles, a_lat, a_ctx, u, gn, sg_w[l].astype(BF16), sg_bias,
                          w_out[l].astype(BF16), vec(post_mix_g), mod, xa, xb, xb_tile0)
        if l % 2 == 0:
            i = l // 2
            x_all = _ffn(l, x_mid, mod, vec(pre_ffn_g), ffn_bf16[0][i], ffn_bf16[1][i], ffn_bf16[2][i],
                         vec(post_ffn_g))
        else:
            i = l // 2
            ids, rank, gate, cnt = _router(l, x_mid, mod, vec(pre_ffn_g), router_w[i])
            runs, lp_col, lp_row, gate_row, texp, tvalid, pad_lo, pad_hi = _routing_tables(ids, rank, gate, cnt)
            xs = _dispatch(l, x_mid, mod, vec(pre_ffn_g), runs, lp_row, gate_row, pad_lo, pad_hi)
            ys = _experts(l, xs, texp, tvalid, moe_bf16[0][i], moe_bf16[1][i], moe_bf16[2][i])
            x_all = _combine(l, ys, runs, lp_col, x_mid, mod, vec(post_ffn_g))
    return x_all.reshape(BATCH, SEQ, D_MODEL)
```

```python
import functools
import math

import jax
import jax.numpy as jnp
from jax import lax
from jax.experimental import pallas as pl
from jax.experimental.pallas import tpu as pltpu

D_MODEL = 1024
BATCH = 16
SEQ = 2048
DEPTH = 2
GRID_W = 64
CTX_LEN = 256
DA_WIDTH = 512
DA_HEADS = 4
DA_VDIM = 128
DA_HEAD_DIM = 64
SG_WIDTH = 512
SG_GROUPS = 4
CHUNK = 128
ROPE_BASE = 10000.0
D_FF = 2816
N_EXPERTS = 8
D_EXPERT = 3584
EPS = 1e-6
IN_COLS = 3 * DA_WIDTH + 2 * SG_WIDTH

N_LAT = BATCH * SEQ
N_CTX = BATCH * CTX_LEN
N_ALL = N_LAT + N_CTX
MOD_ROWS = 32

TP = 1024
TM = 512
TQ = 1024
TQ_SUB = 512
KEY_SEG = 2048
TF_DENSE = 256
TF_MOE = 1792
N_TOK_TILES = N_LAT // TM
ROW_ALIGN = 8
RUN_SIZES = (512, 256, 128, 64, 32, 16, 8)
Z_ROWS = 2 * TM + N_EXPERTS * ROW_ALIGN
D_XS = D_MODEL + 128
XS_TILES = (2 * N_LAT + N_TOK_TILES * N_EXPERTS * (ROW_ALIGN - 1)) // TM + 1 + N_EXPERTS
MIB = 2 ** 20

BF16 = jnp.bfloat16
F32 = jnp.float32


def _rms(x):
    return x * lax.rsqrt(jnp.mean(x * x, axis=-1, keepdims=True) + EPS)


def _silu(x):
    return x * jax.nn.sigmoid(x)


def _resident(block_shape, index_map):
    return pl.BlockSpec(block_shape, index_map, pipeline_mode=pl.Buffered(1))


def _params(semantics, vmem_mib):
    return pltpu.CompilerParams(dimension_semantics=semantics, vmem_limit_bytes=vmem_mib * MIB)


def _mod_kernel(cc_ref, w_ref, b_ref, o_ref):
    a = _silu(cc_ref[...]).astype(BF16)
    o_ref[...] = jnp.dot(a, w_ref[...].astype(BF16), preferred_element_type=F32) + b_ref[...]


def _modulation(cc, ada_w, ada_b):
    tn = 1536
    out = pl.pallas_call(
        _mod_kernel,
        grid=(DEPTH, 6 * D_MODEL // tn),
        in_specs=[
            pl.BlockSpec((MOD_ROWS, D_MODEL), lambda l, j: (0, 0)),
            pl.BlockSpec((None, D_MODEL, tn), lambda l, j: (l, 0, j)),
            pl.BlockSpec((None, 1, tn), lambda l, j: (l, 0, j)),
        ],
        out_specs=pl.BlockSpec((None, MOD_ROWS, tn), lambda l, j: (l, 0, j)),
        out_shape=jax.ShapeDtypeStruct((DEPTH, MOD_ROWS, 6 * D_MODEL), F32),
        compiler_params=_params(("parallel", "parallel"), 40),
        name="modulation",
    )(cc, ada_w, ada_b.reshape(DEPTH, 1, 6 * D_MODEL))
    return out.reshape(DEPTH, MOD_ROWS, 6, D_MODEL)


def _mod_spec(layer, tile):
    tiles_per_batch = SEQ // tile
    return pl.BlockSpec((None, None, 6, D_MODEL),
                        lambda i: (layer, jnp.minimum(i // tiles_per_batch, BATCH), 0, 0))


def _in_proj_kernel(n_lat_tiles, xa_ref, xb_ref, mod_ref, g_ref, w_ref, cos_ref, sa_ref, sb_ref,
                    lng_ref, lnb_ref, q_ref, kt_ref, v_ref, u_ref, gn_ref):
    i = pl.program_id(0)
    x = jnp.where(i < n_lat_tiles, xa_ref[...], xb_ref[...])
    h = _rms(x) * g_ref[...]
    h = h * (1.0 + mod_ref[1:2, :]) + mod_ref[0:1, :]
    p = jnp.dot(h.astype(BF16), w_ref[...], preferred_element_type=F32)

    cos, sa, sb = cos_ref[...], sa_ref[...], sb_ref[...]

    def rope(t):
        return t * cos + pltpu.roll(t, 112, 1) * sa + pltpu.roll(t, 16, 1) * sb

    for hd in range(DA_HEADS):
        lo, hi = hd * DA_VDIM, (hd + 1) * DA_VDIM
        q_ref[:, lo:hi] = (rope(p[:, lo:hi]) * (DA_HEAD_DIM ** -0.5)).astype(BF16)
        kt_ref[lo:hi, :] = rope(p[:, DA_WIDTH + lo:DA_WIDTH + hi]).T.astype(BF16)
    v_ref[...] = p[:, 2 * DA_WIDTH:3 * DA_WIDTH].astype(BF16)
    u_ref[...] = p[:, 3 * DA_WIDTH:3 * DA_WIDTH + SG_WIDTH].astype(BF16)
    gv = p[:, 3 * DA_WIDTH + SG_WIDTH:]
    mu = jnp.mean(gv, axis=-1, keepdims=True)
    var = jnp.mean(jnp.square(gv - mu), axis=-1, keepdims=True)
    gn_ref[...] = ((gv - mu) * lax.rsqrt(var + EPS) * lng_ref[...] + lnb_ref[...]).astype(BF16)


def _in_proj(layer, xa, xb, xb_tile0, mod, pre_g, w_in, rope_tabs, ln_g, ln_b):
    n_lat_tiles = N_LAT // TP
    n_tiles = N_ALL // TP
    tiles_per_batch = SEQ // TP
    tab_spec = pl.BlockSpec(
        (TP, DA_VDIM), lambda i: (jnp.where(i < n_lat_tiles, i % tiles_per_batch, tiles_per_batch), 0))
    row = lambda i: (i, 0)
    outs = pl.pallas_call(
        functools.partial(_in_proj_kernel, n_lat_tiles),
        grid=(n_tiles,),
        in_specs=[
            pl.BlockSpec((TP, D_MODEL), lambda i: (jnp.minimum(i, n_lat_tiles - 1), 0)),
            pl.BlockSpec((TP, D_MODEL), lambda i: (xb_tile0 + jnp.maximum(i - n_lat_tiles, 0), 0)),
            _mod_spec(layer, TP),
            _resident((1, D_MODEL), lambda i: (0, 0)),
            _resident((D_MODEL, IN_COLS), lambda i: (0, 0)),
            tab_spec, tab_spec, tab_spec,
            _resident((1, SG_WIDTH), lambda i: (0, 0)),
            _resident((1, SG_WIDTH), lambda i: (0, 0)),
        ],
        out_specs=[
            pl.BlockSpec((TP, DA_WIDTH), row),
            pl.BlockSpec((DA_WIDTH, TP), lambda i: (0, i)),
            pl.BlockSpec((TP, DA_WIDTH), row),
            pl.BlockSpec((TP, SG_WIDTH), row),
            pl.BlockSpec((TP, SG_WIDTH), row),
        ],
        out_shape=[
            jax.ShapeDtypeStruct((N_ALL, DA_WIDTH), BF16),
            jax.ShapeDtypeStruct((DA_WIDTH, N_ALL), BF16),
            jax.ShapeDtypeStruct((N_ALL, DA_WIDTH), BF16),
            jax.ShapeDtypeStruct((N_ALL, SG_WIDTH), BF16),
            jax.ShapeDtypeStruct((N_ALL, SG_WIDTH), BF16),
        ],
        compiler_params=_params(("parallel",), 48),
        name=f"in_proj_l{layer}",
    )(xa, xb, mod, pre_g, w_in, *rope_tabs, ln_g, ln_b)
    return outs


def _rope_tables():
    pos = jnp.arange(SEQ, dtype=jnp.int32)
    row = (pos // GRID_W).astype(F32)
    col = (pos % GRID_W).astype(F32)
    half = DA_HEAD_DIM // 2
    inv = ROPE_BASE ** (-jnp.arange(0, half, 2, dtype=F32) / half)
    ang_r = row[:, None] * inv
    ang_c = col[:, None] * inv
    ang = jnp.concatenate([ang_r, ang_r, ang_c, ang_c], axis=-1)
    cos = jnp.tile(jnp.cos(ang), (1, 2))
    sin = jnp.tile(jnp.sin(ang), (1, 2))
    first = (jnp.arange(DA_VDIM) % 32) < 16
    sa = jnp.where(first, -sin, 0.0)
    sb = jnp.where(first, 0.0, sin)
    ident = jnp.zeros((TP, DA_VDIM), F32)
    return (jnp.concatenate([cos, ident + 1.0], axis=0),
            jnp.concatenate([sa, ident], axis=0),
            jnp.concatenate([sb, ident], axis=0))


def _attn_kernel(n_src, n_cast, lambda_init, q_ref, *refs):
    kt_refs = refs[:n_src]
    v_refs = refs[n_src:2 * n_src]
    lq1_ref, lk1_ref, lq2_ref, lk2_ref, subg_ref = refs[2 * n_src:2 * n_src + 5]
    n_in = 2 * n_src + 5
    cast_in_refs = refs[n_in:n_in + n_cast]
    o_ref = refs[n_in + n_cast]
    cast_out_refs = refs[n_in + n_cast + 1:n_in + 2 * n_cast + 1]
    vext_refs = refs[-n_src:]

    for src_ref, dst_ref in zip(cast_in_refs, cast_out_refs):
        dst_ref[...] = src_ref[...].astype(BF16)

    n_heads = q_ref.shape[1] // DA_VDIM
    head = lambda hd: slice(hd * DA_VDIM, (hd + 1) * DA_VDIM)
    head_ext = lambda hd: slice(hd * 2 * DA_VDIM, (hd + 1) * 2 * DA_VDIM)

    @pl.when(pl.program_id(2) == 0)
    def _():
        for v_ref, ve_ref in zip(v_refs, vext_refs):
            for hd in range(n_heads):
                ve_ref[:, hd * 2 * DA_VDIM:hd * 2 * DA_VDIM + DA_VDIM] = v_ref[:, head(hd)]
                ve_ref[:, hd * 2 * DA_VDIM + DA_VDIM:(hd + 1) * 2 * DA_VDIM] = jnp.ones((v_ref.shape[0], DA_VDIM), BF16)

    segs = []
    for kt_ref, ve_ref in zip(kt_refs, vext_refs):
        n_keys = ve_ref.shape[0]
        for lo in range(0, n_keys, KEY_SEG):
            segs.append((kt_ref, ve_ref, lo, min(lo + KEY_SEG, n_keys)))

    lam = (jnp.exp(jnp.sum(lq1_ref[...] * lk1_ref[...], axis=-1, keepdims=True))
           - jnp.exp(jnp.sum(lq2_ref[...] * lk2_ref[...], axis=-1, keepdims=True)) + lambda_init)
    tq_sub = min(TQ_SUB, q_ref.shape[0])

    def scores(hd, r0):
        q = q_ref[r0:r0 + tq_sub, head(hd)]
        first = lax.broadcasted_iota(jnp.int32, q.shape, 1) < DA_HEAD_DIM
        zero = jnp.zeros_like(q)
        qms = (jnp.where(first, q, zero), jnp.where(first, zero, q))
        return [[jnp.dot(qm, kt_ref[head(hd), lo:hi], preferred_element_type=F32) for kt_ref, _, lo, hi in segs]
                for qm in qms]

    def finish(hd, r0, ss_maps):
        maps = []
        for ss in ss_maps:
            m = functools.reduce(jnp.maximum, [jnp.max(s, axis=-1, keepdims=True) for s in ss])
            ts = [jnp.dot(jnp.exp(s - m).astype(BF16), ve_ref[lo:hi, head_ext(hd)], preferred_element_type=F32)
                  for s, (_, ve_ref, lo, hi) in zip(ss, segs)]
            acc = functools.reduce(jnp.add, ts)
            maps.append(acc[:, :DA_VDIM] / acc[:, DA_VDIM:])
        o = maps[0] - lam * maps[1]
        o_ref[r0:r0 + tq_sub, head(hd)] = (_rms(o) * subg_ref[...] * (1.0 - lambda_init)).astype(BF16)

    items = [(hd, r0) for hd in range(n_heads) for r0 in range(0, q_ref.shape[0], tq_sub)]
    pending = scores(*items[0])
    for idx, item in enumerate(items):
        nxt = scores(*items[idx + 1]) if idx + 1 < len(items) else None
        finish(*item, pending)
        pending = nxt


def _attention(layer, q, kt, v, lam_params, subg, latent, casts=()):
    lambda_init = 0.8 - 0.6 * math.exp(-0.3 * layer)
    ctx_blk0 = N_LAT // CTX_LEN
    small = lambda: pl.BlockSpec((1, DA_HEAD_DIM), lambda b, h, t: (0, 0))
    tail_specs = [small(), small(), small(), small(), pl.BlockSpec((1, DA_VDIM), lambda b, h, t: (0, 0))]
    hw = DA_VDIM if latent else DA_WIDTH
    n_hblk = DA_WIDTH // hw
    kt_ctx = pl.BlockSpec((hw, CTX_LEN), lambda b, h, t: (h, ctx_blk0 + b))
    v_ctx = pl.BlockSpec((CTX_LEN, hw), lambda b, h, t: (ctx_blk0 + b, h))
    if latent:
        tq, n_q, n_rows = TQ, SEQ // TQ, N_LAT
        q_map = lambda b, h, t: (b * n_q + t, h)
        o_map = q_map
        kt_specs = [kt_ctx, pl.BlockSpec((hw, SEQ), lambda b, h, t: (h, b))]
        v_specs = [v_ctx, pl.BlockSpec((SEQ, hw), lambda b, h, t: (b, h))]
        key_lens = [CTX_LEN, SEQ]
    else:
        tq, n_q, n_rows = CTX_LEN, 1, N_CTX
        q_map = lambda b, h, t: (ctx_blk0 + b, h)
        o_map = lambda b, h, t: (b, h)
        kt_specs, v_specs, key_lens = [kt_ctx], [v_ctx], [CTX_LEN]
    n_src = len(key_lens)
    n_steps = BATCH * n_hblk * n_q
    cast_specs = []
    for w in casts:
        rows = next(r for r in range(16, w.shape[0] + 1, 16) if w.shape[0] % r == 0 and w.shape[0] // r <= n_steps)
        last = w.shape[0] // rows - 1
        cast_specs.append(pl.BlockSpec(
            (rows, w.shape[1]),
            lambda b, h, t, last=last: (jnp.minimum((b * n_hblk + h) * n_q + t, last), 0)))
    outs = pl.pallas_call(
        functools.partial(_attn_kernel, n_src, len(casts), lambda_init),
        grid=(BATCH, n_hblk, n_q),
        in_specs=[pl.BlockSpec((tq, hw), q_map)] + kt_specs + v_specs + tail_specs + cast_specs,
        out_specs=[pl.BlockSpec((tq, hw), o_map)] + cast_specs,
        out_shape=[jax.ShapeDtypeStruct((n_rows, DA_WIDTH), BF16)]
                  + [jax.ShapeDtypeStruct(w.shape, BF16) for w in casts],
        scratch_shapes=[pltpu.VMEM((n, 2 * hw), BF16) for n in key_lens],
        compiler_params=_params(("parallel", "parallel", "arbitrary"), 56),
        name=f"attn_l{layer}_{'lat' if latent else 'ctx'}",
    )(q, *([kt] * n_src), *([v] * n_src), *lam_params, subg, *casts)
    return outs[0], outs[1:]


def _out_proj_kernel(n_lat_tiles, aa_ref, ab_ref, u_ref, gn_ref, sgw_ref, sgb_ref, w_ref, pg_ref, mod_ref,
                     xa_ref, xb_ref, o_ref, s_ref):
    i = pl.program_id(0)
    a = jnp.where(i < n_lat_tiles, aa_ref[...], ab_ref[...])
    for c in range(TP // CHUNK):
        rows = slice(c * CHUNK, (c + 1) * CHUNK)
        for g in range(SG_GROUPS):
            cols = slice(g * CHUNK, (g + 1) * CHUNK)
            mixed = jnp.dot(sgw_ref[g], gn_ref[rows, cols], preferred_element_type=F32) + sgb_ref[:, cols]
            s_ref[rows, cols] = (u_ref[rows, cols].astype(F32) * mixed).astype(BF16)
    o = (jnp.dot(a, w_ref[:DA_WIDTH, :], preferred_element_type=F32)
         + jnp.dot(s_ref[...], w_ref[DA_WIDTH:, :], preferred_element_type=F32))
    x = jnp.where(i < n_lat_tiles, xa_ref[...], xb_ref[...])
    o_ref[...] = x + mod_ref[2:3, :] * (_rms(o) * pg_ref[...])


def _out_proj(layer, n_tiles, a_lat, a_ctx, u, gn, sg_w, sg_bias, w_out, post_g, mod, xa, xb, xb_tile0):
    n_lat_tiles = N_LAT // TP
    row = lambda i: (i, 0)
    half = lambda: pl.BlockSpec((TP, DA_WIDTH), row)
    return pl.pallas_call(
        functools.partial(_out_proj_kernel, n_lat_tiles),
        grid=(n_tiles,),
        in_specs=[
            pl.BlockSpec((TP, DA_WIDTH), lambda i: (jnp.minimum(i, n_lat_tiles - 1), 0)),
            pl.BlockSpec((TP, DA_WIDTH), lambda i: (jnp.maximum(i - n_lat_tiles, 0), 0)),
            half(), half(),
            _resident((SG_GROUPS, CHUNK, CHUNK), lambda i: (0, 0, 0)),
            _resident((CHUNK, SG_WIDTH), lambda i: (0, 0)),
            _resident((D_MODEL, D_MODEL), lambda i: (0, 0)),
            _resident((1, D_MODEL), lambda i: (0, 0)),
            _mod_spec(layer, TP),
            pl.BlockSpec((TP, D_MODEL), lambda i: (jnp.minimum(i, n_lat_tiles - 1), 0)),
            pl.BlockSpec((TP, D_MODEL), lambda i: (xb_tile0 + jnp.maximum(i - n_lat_tiles, 0), 0)),
        ],
        out_specs=pl.BlockSpec((TP, D_MODEL), row),
        out_shape=jax.ShapeDtypeStruct((n_tiles * TP, D_MODEL), F32),
        scratch_shapes=[pltpu.VMEM((TP, SG_WIDTH), BF16)],
        compiler_params=_params(("parallel",), 48),
        name=f"out_proj_l{layer}",
    )(a_lat, a_ctx, u, gn, sg_w, sg_bias, w_out, post_g, mod, xa, xb)


def _ffn_kernel(x_ref, mod_ref, g_ref, w1_ref, w3_ref, w2_ref, pg_ref, o_ref):
    x = x_ref[...]
    f = ((_rms(x) * g_ref[...]) * (1.0 + mod_ref[4:5, :]) + mod_ref[3:4, :]).astype(BF16)
    y = None
    for k in range(D_FF // TF_DENSE):
        cols = slice(k * TF_DENSE, (k + 1) * TF_DENSE)
        h1 = jnp.dot(f, w1_ref[:, cols], preferred_element_type=F32)
        h3 = jnp.dot(f, w3_ref[:, cols], preferred_element_type=F32)
        t = jnp.dot((_silu(h1) * h3).astype(BF16), w2_ref[cols, :], preferred_element_type=F32)
        y = t if y is None else y + t
    o_ref[...] = x + mod_ref[5:6, :] * (_rms(y) * pg_ref[...])


def _ffn(layer, x_all, mod, pre_g, w1, w3, w2, post_g):
    row = lambda i: (i, 0)
    return pl.pallas_call(
        _ffn_kernel,
        grid=(N_ALL // TP,),
        in_specs=[
            pl.BlockSpec((TP, D_MODEL), row),
            _mod_spec(layer, TP),
            _resident((1, D_MODEL), lambda i: (0, 0)),
            _resident((D_MODEL, D_FF), lambda i: (0, 0)),
            _resident((D_MODEL, D_FF), lambda i: (0, 0)),
            _resident((D_FF, D_MODEL), lambda i: (0, 0)),
            _resident((1, D_MODEL), lambda i: (0, 0)),
        ],
        out_specs=pl.BlockSpec((TP, D_MODEL), row),
        out_shape=jax.ShapeDtypeStruct((N_ALL, D_MODEL), F32),
        compiler_params=_params(("parallel",), 56),
        name=f"ffn_l{layer}",
    )(x_all, mod, pre_g, w1, w3, w2, post_g)


def _ffn_input(x, mod_ref, g_ref):
    return (_rms(x) * g_ref[...]) * (1.0 + mod_ref[4:5, :]) + mod_ref[3:4, :]


def _router_kernel(x_ref, mod_ref, g_ref, rw_ref, ids_ref, rank_ref, rows_ref, cnt_ref):
    f = _ffn_input(x_ref[...], mod_ref, g_ref)
    rw = rw_ref[...]
    f_hi, rw_hi = f.astype(BF16), rw.astype(BF16)
    f_lo, rw_lo = (f - f_hi.astype(F32)).astype(BF16), (rw - rw_hi.astype(F32)).astype(BF16)
    logits = (jnp.dot(f_hi, rw_hi, preferred_element_type=F32) + jnp.dot(f_lo, rw_hi, preferred_element_type=F32)
              + jnp.dot(f_hi, rw_lo, preferred_element_type=F32))
    lane = lax.broadcasted_iota(jnp.int32, logits.shape, 1)
    v1 = jnp.max(logits, axis=-1, keepdims=True)
    i1 = jnp.min(jnp.where(logits == v1, lane, N_EXPERTS), axis=-1, keepdims=True)
    rest = jnp.where(lane == i1, -jnp.inf, logits)
    v2 = jnp.max(rest, axis=-1, keepdims=True)
    i2 = jnp.min(jnp.where(rest == v2, lane, N_EXPERTS), axis=-1, keepdims=True)
    e = jnp.exp(v2 - v1)
    w1 = 1.0 / (1.0 + e)

    sel1, sel2 = lane == i1, lane == i2
    onehot = jnp.where(sel1 | sel2, 1.0, 0.0)
    r = lax.broadcasted_iota(jnp.int32, (TM, TM), 0)
    c = lax.broadcasted_iota(jnp.int32, (TM, TM), 1)
    tri = jnp.where(c < r, 1.0, 0.0).astype(BF16)
    prefix = jnp.dot(tri, onehot.astype(BF16), preferred_element_type=F32)
    rank1 = jnp.sum(jnp.where(sel1, prefix, 0.0), axis=-1, keepdims=True)
    rank2 = jnp.sum(jnp.where(sel2, prefix, 0.0), axis=-1, keepdims=True)

    slot0 = lax.broadcasted_iota(jnp.int32, (TM, 2), 1) == 0
    ids_ref[...] = jnp.where(slot0, i1, i2)
    rank_ref[...] = jnp.where(slot0, rank1, rank2).astype(jnp.int32)
    cnt_ref[...] = jnp.sum(onehot, axis=0, keepdims=True).astype(jnp.int32)
    col = lax.broadcasted_iota(jnp.int32, (TM, 128), 1)
    packed = jnp.zeros((TM, 128), F32)
    for k, val in enumerate([i1.astype(F32), i2.astype(F32), rank1, rank2, w1, e * w1]):
        packed = jnp.where(col == k, val, packed)
    rows_ref[...] = packed.T[:8, :]


def _router(layer, x_lat, mod, pre_g, router_w):
    row = lambda i: (i, 0)
    pair = lambda: pl.BlockSpec((TM, 2), row)
    return pl.pallas_call(
        _router_kernel,
        grid=(N_TOK_TILES,),
        in_specs=[
            pl.BlockSpec((TM, D_MODEL), row),
            _mod_spec(layer, TM),
            _resident((1, D_MODEL), lambda i: (0, 0)),
            _resident((D_MODEL, N_EXPERTS), lambda i: (0, 0)),
        ],
        out_specs=[pair(), pair(), pl.BlockSpec((None, 8, TM), lambda i: (i, 0, 0)),
                   pl.BlockSpec((None, 1, N_EXPERTS), lambda i: (i, 0, 0))],
        out_shape=[jax.ShapeDtypeStruct((N_LAT, 2), jnp.int32),
                   jax.ShapeDtypeStruct((N_LAT, 2), jnp.int32),
                   jax.ShapeDtypeStruct((N_TOK_TILES, 8, TM), F32),
                   jax.ShapeDtypeStruct((N_TOK_TILES, 1, N_EXPERTS), jnp.int32)],
        compiler_params=_params(("parallel",), 32),
        name=f"router_l{layer}",
    )(x_lat, mod, pre_g, router_w)


def _routing_tables(cnt):
    cnt = cnt.reshape(N_TOK_TILES, N_EXPERTS)
    run = (cnt + ROW_ALIGN - 1) // ROW_ALIGN * ROW_ALIGN
    zoff = jnp.cumsum(run, axis=1) - run
    grp = jnp.sum(run, axis=0)
    tiles = (grp + TM - 1) // TM
    tile_end = jnp.cumsum(tiles)
    off = (tile_end - tiles) * TM
    gstart = off[None, :] + jnp.cumsum(run, axis=0) - run
    runs = jnp.concatenate([run, zoff, gstart], axis=1).astype(jnp.int32).reshape(N_TOK_TILES, 1, 3 * N_EXPERTS)

    j = jnp.arange(XS_TILES, dtype=jnp.int32)
    n_used = tile_end[-1]
    valid = (j < n_used).astype(jnp.int32)
    jc = jnp.minimum(j, n_used - 1)
    texp = jnp.minimum(jnp.sum(jc[:, None] >= tile_end[None, :], axis=1), N_EXPERTS - 1).astype(jnp.int32)
    pad_lo = (off + grp).astype(jnp.int32)
    pad_hi = (tile_end * TM).astype(jnp.int32).at[N_EXPERTS - 1].set(XS_TILES * TM)
    return runs, texp, valid, pad_lo, pad_hi


def _for_each_run_piece(runs_ref, fn):
    for e in range(N_EXPERTS):
        n = runs_ref[0, e]
        zo = runs_ref[0, N_EXPERTS + e]
        go = runs_ref[0, 2 * N_EXPERTS + e]
        for size in RUN_SIZES:
            done = n & ~(2 * size - 1)
            pl.when((n & size) != 0)(functools.partial(
                fn, pl.multiple_of(zo + done, ROW_ALIGN), pl.multiple_of(go + done, ROW_ALIGN), size))


def _block_position(ids, rank, runs_ref):
    start = jnp.zeros_like(ids)
    for e in range(N_EXPERTS):
        start = jnp.where(ids == e, runs_ref[0, N_EXPERTS + e], start)
    return start + rank


def _dispatch_kernel(pad_lo_ref, pad_hi_ref, runs_ref, runs_prev_ref, tok_ref, x_ref, mod_ref, g_ref,
                     xs_ref, z_ref, zero_ref, sems, zsem):
    i = pl.program_id(0)
    n = pl.num_programs(0)

    @pl.when(i == 0)
    def _():
        zero_ref[...] = jnp.zeros_like(zero_ref)
        for e in range(N_EXPERTS):
            lo, hi = pad_lo_ref[e] // ROW_ALIGN, pad_hi_ref[e] // ROW_ALIGN
            blk = lambda r: pltpu.make_async_copy(
                zero_ref, xs_ref.at[pl.ds(pl.multiple_of(r * ROW_ALIGN, ROW_ALIGN), ROW_ALIGN)], zsem)
            lax.fori_loop(lo, hi, lambda r, _: blk(r).start(), None)
            lax.fori_loop(lo, hi, lambda r, _: blk(r).wait(), None)

    def piece(slot, zrow, grow, size):
        return pltpu.make_async_copy(z_ref.at[slot, pl.ds(zrow, size)], xs_ref.at[pl.ds(grow, size)], sems.at[slot])

    def step(slot):
        f = _ffn_input(x_ref[...], mod_ref, g_ref).astype(BF16)
        tok = tok_ref[...]
        lp = _block_position(tok[0:2, :].astype(jnp.int32), tok[2:4, :].astype(jnp.int32), runs_ref)
        rows = lax.broadcasted_iota(jnp.int32, (Z_ROWS, TM), 0)
        hit0, hit1 = lp[0:1, :] == rows, lp[1:2, :] == rows
        sel = jnp.where(hit0 | hit1, 1.0, 0.0).astype(BF16)
        z_ref[slot, :, :D_MODEL] = jnp.dot(sel, f, preferred_element_type=F32)
        gate = jnp.sum(jnp.where(hit0, tok[4:5, :], 0.0) + jnp.where(hit1, tok[5:6, :], 0.0),
                       axis=-1, keepdims=True)
        z_ref[slot, :, D_MODEL:] = jnp.broadcast_to(gate, (Z_ROWS, D_XS - D_MODEL))
        _for_each_run_piece(runs_ref, lambda zrow, grow, size: piece(slot, zrow, grow, size).start())

        @pl.when(i >= 1)
        def _():
            _for_each_run_piece(runs_prev_ref, lambda zrow, grow, size: piece(1 - slot, zrow, grow, size).wait())

        @pl.when(i == n - 1)
        def _():
            _for_each_run_piece(runs_ref, lambda zrow, grow, size: piece(slot, zrow, grow, size).wait())

    for slot in range(2):
        pl.when(i % 2 == slot)(functools.partial(step, slot))


def _dispatch(layer, x_lat, mod, pre_g, runs, tok_rows, pad_lo, pad_hi):
    smem = lambda: pl.BlockSpec(memory_space=pltpu.SMEM)
    runs_spec = lambda back: pl.BlockSpec((None, 1, 3 * N_EXPERTS), lambda i: (jnp.maximum(i - back, 0), 0, 0),
                                          memory_space=pltpu.SMEM)
    return pl.pallas_call(
        _dispatch_kernel,
        grid=(N_TOK_TILES,),
        in_specs=[
            smem(), smem(), runs_spec(0), runs_spec(1),
            pl.BlockSpec((None, 8, TM), lambda i: (i, 0, 0)),
            pl.BlockSpec((TM, D_MODEL), lambda i: (i, 0)),
            _mod_spec(layer, TM),
            _resident((1, D_MODEL), lambda i: (0, 0)),
        ],
        out_specs=pl.BlockSpec(memory_space=pl.ANY),
        out_shape=jax.ShapeDtypeStruct((XS_TILES * TM, D_XS), F32),
        scratch_shapes=[pltpu.VMEM((2, Z_ROWS, D_XS), F32), pltpu.VMEM((ROW_ALIGN, D_XS), F32),
                        pltpu.SemaphoreType.DMA((2,)), pltpu.SemaphoreType.DMA(())],
        compiler_params=_params(("arbitrary",), 48),
        name=f"moe_dispatch_l{layer}",
    )(pad_lo, pad_hi, runs, runs, tok_rows, x_lat, mod, pre_g)


def _expert_kernel(texp_ref, tvalid_ref, xs_ref, w1_ref, w3_ref, w2_ref, ys_ref, xb_ref, acc_ref):
    j = pl.program_id(0)
    k = pl.program_id(1)

    @pl.when((tvalid_ref[j] == 0) & (k == 0))
    def _():
        ys_ref[...] = jnp.zeros_like(ys_ref)

    @pl.when(tvalid_ref[j] == 1)
    def _():
        @pl.when(k == 0)
        def _():
            xb_ref[...] = xs_ref[:, :D_MODEL].astype(BF16)
            acc_ref[...] = jnp.zeros_like(acc_ref)

        xb = xb_ref[...]
        h1 = jnp.dot(xb, w1_ref[...], preferred_element_type=F32)
        h3 = jnp.dot(xb, w3_ref[...], preferred_element_type=F32)
        acc_ref[...] += jnp.dot((_silu(h1) * h3).astype(BF16), w2_ref[...], preferred_element_type=F32)

        @pl.when(k == pl.num_programs(1) - 1)
        def _():
            ys_ref[...] = acc_ref[...] * xs_ref[:, D_MODEL:D_MODEL + 1]


def _experts(layer, xs, texp, tvalid, w1, w3, w2):
    n_k = D_EXPERT // TF_MOE
    row = lambda j, k, texp, tvalid: (j, 0)
    kk = lambda j, k, tvalid: jnp.where(tvalid[j] == 1, k, n_k - 1)
    return pl.pallas_call(
        _expert_kernel,
        grid_spec=pltpu.PrefetchScalarGridSpec(
            num_scalar_prefetch=2,
            grid=(XS_TILES, n_k),
            in_specs=[
                pl.BlockSpec((TM, D_XS), row),
                pl.BlockSpec((None, D_MODEL, TF_MOE), lambda j, k, texp, tvalid: (texp[j], 0, kk(j, k, tvalid))),
                pl.BlockSpec((None, D_MODEL, TF_MOE), lambda j, k, texp, tvalid: (texp[j], 0, kk(j, k, tvalid))),
                pl.BlockSpec((None, TF_MOE, D_MODEL), lambda j, k, texp, tvalid: (texp[j], kk(j, k, tvalid), 0)),
            ],
            out_specs=pl.BlockSpec((TM, D_MODEL), row),
            scratch_shapes=[pltpu.VMEM((TM, D_MODEL), BF16), pltpu.VMEM((TM, D_MODEL), F32)],
        ),
        out_shape=jax.ShapeDtypeStruct((XS_TILES * TM, D_MODEL), F32),
        compiler_params=_params(("arbitrary", "arbitrary"), 56),
        name=f"moe_experts_l{layer}",
    )(texp, tvalid, xs, w1, w3, w2)


def _combine_kernel(runs_ref, runs_next_ref, ys_ref, ids_ref, rank_ref, x_ref, mod_ref, pg_ref, o_ref, buf_ref, sems):
    i = pl.program_id(0)
    n = pl.num_programs(0)

    def piece(slot, zrow, grow, size):
        return pltpu.make_async_copy(ys_ref.at[pl.ds(grow, size)], buf_ref.at[slot, pl.ds(zrow, size)], sems.at[slot])

    def step(slot):
        @pl.when(i == 0)
        def _():
            buf_ref[...] = jnp.zeros_like(buf_ref)
            _for_each_run_piece(runs_ref, lambda zrow, grow, size: piece(slot, zrow, grow, size).start())

        @pl.when(i + 1 < n)
        def _():
            _for_each_run_piece(runs_next_ref, lambda zrow, grow, size: piece(1 - slot, zrow, grow, size).start())

        _for_each_run_piece(runs_ref, lambda zrow, grow, size: piece(slot, zrow, grow, size).wait())
        lp = _block_position(ids_ref[...], rank_ref[...], runs_ref)
        cols = lax.broadcasted_iota(jnp.int32, (TM, Z_ROWS), 1)
        pick = jnp.where((lp[:, 0:1] == cols) | (lp[:, 1:2] == cols), 1.0, 0.0).astype(BF16)
        y = jnp.dot(pick, buf_ref[slot].astype(BF16), preferred_element_type=F32)
        o_ref[...] = x_ref[...] + mod_ref[5:6, :] * (_rms(y) * pg_ref[...])

    for slot in range(2):
        pl.when(i % 2 == slot)(functools.partial(step, slot))


def _combine(layer, ys, runs, ids, rank, x_lat, mod, post_g):
    row = lambda i: (i, 0)
    runs_spec = lambda fwd: pl.BlockSpec(
        (None, 1, 3 * N_EXPERTS), lambda i: (jnp.minimum(i + fwd, N_TOK_TILES - 1), 0, 0), memory_space=pltpu.SMEM)
    return pl.pallas_call(
        _combine_kernel,
        grid=(N_TOK_TILES,),
        in_specs=[
            runs_spec(0), runs_spec(1),
            pl.BlockSpec(memory_space=pl.ANY),
            pl.BlockSpec((TM, 2), row),
            pl.BlockSpec((TM, 2), row),
            pl.BlockSpec((TM, D_MODEL), row),
            _mod_spec(layer, TM),
            _resident((1, D_MODEL), lambda i: (0, 0)),
        ],
        out_specs=pl.BlockSpec((TM, D_MODEL), row),
        out_shape=jax.ShapeDtypeStruct((N_LAT, D_MODEL), F32),
        scratch_shapes=[pltpu.VMEM((2, Z_ROWS, D_MODEL), F32), pltpu.SemaphoreType.DMA((2,))],
        compiler_params=_params(("arbitrary",), 40),
        name=f"moe_combine_l{layer}",
    )(runs, runs, ys, ids, rank, x_lat, mod, post_g)


def kernel(x, c, ctx, c_ctx, ada_w, ada_b, pre_mix_g, post_mix_g, pre_ffn_g, post_ffn_g, w_in, w_out,
           lam_q1, lam_k1, lam_q2, lam_k2, subln_g, sg_ln_g, sg_ln_b, sg_w, sg_b,
           ffn_w1, ffn_w3, ffn_w2, router_w, moe_w1, moe_w3, moe_w2):
    x_lat = x.reshape(N_LAT, D_MODEL)
    x_ctx = ctx.reshape(N_CTX, D_MODEL)
    cc = jnp.concatenate([c, c_ctx[None, :], jnp.zeros((MOD_ROWS - BATCH - 1, D_MODEL), F32)], axis=0)
    mod = _modulation(cc, ada_w, ada_b)
    rope_tabs = _rope_tables()
    n_lat_tiles = N_LAT // TP
    x_all = None
    for l in range(DEPTH):
        last = l == DEPTH - 1
        vec = lambda a: a[l][None, :]
        if l == 0:
            xa, xb, xb_tile0 = x_lat, x_ctx, 0
        else:
            xa, xb, xb_tile0 = x_all, x_all, n_lat_tiles
        if l == 0:
            w_in_l, w_out_l, sg_w_l = w_in[0].astype(BF16), w_out[0].astype(BF16), sg_w[0].astype(BF16)
            late = (ffn_w1, ffn_w3, ffn_w2, moe_w1, moe_w3, moe_w2, w_in[1:], w_out[1:], sg_w[1:])
            casts = [w.reshape(-1, w.shape[-1]) for w in late]
        else:
            w_in_l, w_out_l, sg_w_l, casts = w_in_late[l - 1], w_out_late[l - 1], sg_w_late[l - 1], []
        q, kt, v, u, gn = _in_proj(l, xa, xb, xb_tile0, mod, vec(pre_mix_g), w_in_l,
                                   rope_tabs, vec(sg_ln_g), vec(sg_ln_b))
        lam_params = [vec(lam_q1), vec(lam_k1), vec(lam_q2), vec(lam_k2)]
        a_lat, cast_out = _attention(l, q, kt, v, lam_params, vec(subln_g), latent=True, casts=casts)
        if l == 0:
            late_bf16 = [w.reshape(o.shape) for w, o in zip(cast_out, late)]
            ffn_bf16, moe_bf16 = late_bf16[0:3], late_bf16[3:6]
            w_in_late, w_out_late, sg_w_late = late_bf16[6:9]
        a_ctx = a_lat if last else _attention(l, q, kt, v, lam_params, vec(subln_g), latent=False)[0]
        sg_bias = jnp.repeat(sg_b[l].T, CHUNK, axis=1)
        n_tiles = n_lat_tiles if last else N_ALL // TP
        x_mid = _out_proj(l, n_tiles, a_lat, a_ctx, u, gn, sg_w_l, sg_bias, w_out_l, vec(post_mix_g), mod,
                          xa, xb, xb_tile0)
        if l % 2 == 0:
            i = l // 2
            x_all = _ffn(l, x_mid, mod, vec(pre_ffn_g), ffn_bf16[0][i], ffn_bf16[1][i], ffn_bf16[2][i],
                         vec(post_ffn_g))
        else:
            i = l // 2
            ids, rank, tok_rows, cnt = _router(l, x_mid, mod, vec(pre_ffn_g), router_w[i])
            runs, texp, tvalid, pad_lo, pad_hi = _routing_tables(cnt)
            xs = _dispatch(l, x_mid, mod, vec(pre_ffn_g), runs, tok_rows, pad_lo, pad_hi)
            ys = _experts(l, xs, texp, tvalid, moe_bf16[0][i], moe_bf16[1][i], moe_bf16[2][i])
            x_all = _combine(l, ys, runs, ids, rank, x_mid, mod, vec(post_ffn_g))
    return x_all.reshape(BATCH, SEQ, D_MODEL)
```

```python
import functools
import math

import jax
import jax.numpy as jnp
from jax import lax
from jax.experimental import pallas as pl
from jax.experimental.pallas import tpu as pltpu

D_MODEL = 1024
BATCH = 16
SEQ = 2048
DEPTH = 2
GRID_W = 64
CTX_LEN = 256
DA_WIDTH = 512
DA_HEADS = 4
DA_VDIM = 128
DA_HEAD_DIM = 64
SG_WIDTH = 512
SG_GROUPS = 4
CHUNK = 128
ROPE_BASE = 10000.0
D_FF = 2816
N_EXPERTS = 8
D_EXPERT = 3584
EPS = 1e-6
IN_COLS = 3 * DA_WIDTH + 2 * SG_WIDTH

N_LAT = BATCH * SEQ
N_CTX = BATCH * CTX_LEN
N_ALL = N_LAT + N_CTX
MOD_ROWS = 32

TP = 1024
TP_SUB = 512
TM = 512
TQ = 1024
TQ_SUB = 512
KEY_SEG = 2048
TF_DENSE = 256
TF_MOE = 1792
N_TOK_TILES = N_LAT // TM
ROW_ALIGN = 8
RUN_SIZES = (512, 256, 128, 64, 32, 16, 8)
Z_ROWS = 2 * TM + N_EXPERTS * ROW_ALIGN
D_XS = D_MODEL + 128
XS_TILES = (2 * N_LAT + N_TOK_TILES * N_EXPERTS * (ROW_ALIGN - 1)) // TM + 1 + N_EXPERTS
MIB = 2 ** 20

BF16 = jnp.bfloat16
F32 = jnp.float32


def _rms(x):
    return x * lax.rsqrt(jnp.mean(x * x, axis=-1, keepdims=True) + EPS)


def _silu(x):
    return x * jax.nn.sigmoid(x)


def _resident(block_shape, index_map):
    return pl.BlockSpec(block_shape, index_map, pipeline_mode=pl.Buffered(1))


def _params(semantics, vmem_mib):
    return pltpu.CompilerParams(dimension_semantics=semantics, vmem_limit_bytes=vmem_mib * MIB)


def _mod_kernel(cc_ref, w_ref, b_ref, o_ref):
    a = _silu(cc_ref[...]).astype(BF16)
    o_ref[...] = jnp.dot(a, w_ref[...].astype(BF16), preferred_element_type=F32) + b_ref[...]


def _modulation(cc, ada_w, ada_b):
    tn = 1536
    out = pl.pallas_call(
        _mod_kernel,
        grid=(DEPTH, 6 * D_MODEL // tn),
        in_specs=[
            pl.BlockSpec((MOD_ROWS, D_MODEL), lambda l, j: (0, 0)),
            pl.BlockSpec((None, D_MODEL, tn), lambda l, j: (l, 0, j)),
            pl.BlockSpec((None, 1, tn), lambda l, j: (l, 0, j)),
        ],
        out_specs=pl.BlockSpec((None, MOD_ROWS, tn), lambda l, j: (l, 0, j)),
        out_shape=jax.ShapeDtypeStruct((DEPTH, MOD_ROWS, 6 * D_MODEL), F32),
        compiler_params=_params(("parallel", "parallel"), 40),
        name="modulation",
    )(cc, ada_w, ada_b.reshape(DEPTH, 1, 6 * D_MODEL))
    return out.reshape(DEPTH, MOD_ROWS, 6, D_MODEL)


def _mod_spec(layer, tile):
    tiles_per_batch = SEQ // tile
    return pl.BlockSpec((None, None, 6, D_MODEL),
                        lambda i: (layer, jnp.minimum(i // tiles_per_batch, BATCH), 0, 0))


def _in_proj_kernel(n_lat_tiles, xa_ref, xb_ref, mod_ref, g_ref, w_ref, cos_ref, sa_ref, sb_ref,
                    lng_ref, lnb_ref, q_ref, kt_ref, v_ref, u_ref, gn_ref):
    i = pl.program_id(0)
    def project(r0):
        rows = slice(r0, r0 + TP_SUB)
        x = jnp.where(i < n_lat_tiles, xa_ref[rows, :], xb_ref[rows, :])
        h = _rms(x) * g_ref[...]
        h = h * (1.0 + mod_ref[1:2, :]) + mod_ref[0:1, :]
        return jnp.dot(h.astype(BF16), w_ref[...], preferred_element_type=F32)

    def finish(r0, p):
        rows = slice(r0, r0 + TP_SUB)
        cos, sa, sb = cos_ref[rows, :], sa_ref[rows, :], sb_ref[rows, :]

        def rope(t):
            return t * cos + pltpu.roll(t, 112, 1) * sa + pltpu.roll(t, 16, 1) * sb

        for hd in range(DA_HEADS):
            lo, hi = hd * DA_VDIM, (hd + 1) * DA_VDIM
            q_ref[rows, lo:hi] = (rope(p[:, lo:hi]) * (DA_HEAD_DIM ** -0.5)).astype(BF16)
            kt_ref[lo:hi, rows] = rope(p[:, DA_WIDTH + lo:DA_WIDTH + hi]).T.astype(BF16)
        v_ref[rows, :] = p[:, 2 * DA_WIDTH:3 * DA_WIDTH].astype(BF16)
        u_ref[rows, :] = p[:, 3 * DA_WIDTH:3 * DA_WIDTH + SG_WIDTH].astype(BF16)
        gv = p[:, 3 * DA_WIDTH + SG_WIDTH:]
        mu = jnp.mean(gv, axis=-1, keepdims=True)
        var = jnp.mean(jnp.square(gv - mu), axis=-1, keepdims=True)
        gn_ref[rows, :] = ((gv - mu) * lax.rsqrt(var + EPS) * lng_ref[...] + lnb_ref[...]).astype(BF16)

    starts = list(range(0, TP, TP_SUB))
    pending = project(starts[0])
    for idx, r0 in enumerate(starts):
        nxt = project(starts[idx + 1]) if idx + 1 < len(starts) else None
        finish(r0, pending)
        pending = nxt


def _in_proj(layer, xa, xb, xb_tile0, mod, pre_g, w_in, rope_tabs, ln_g, ln_b):
    n_lat_tiles = N_LAT // TP
    n_tiles = N_ALL // TP
    tiles_per_batch = SEQ // TP
    tab_spec = pl.BlockSpec(
        (TP, DA_VDIM), lambda i: (jnp.where(i < n_lat_tiles, i % tiles_per_batch, tiles_per_batch), 0))
    row = lambda i: (i, 0)
    outs = pl.pallas_call(
        functools.partial(_in_proj_kernel, n_lat_tiles),
        grid=(n_tiles,),
        in_specs=[
            pl.BlockSpec((TP, D_MODEL), lambda i: (jnp.minimum(i, n_lat_tiles - 1), 0)),
            pl.BlockSpec((TP, D_MODEL), lambda i: (xb_tile0 + jnp.maximum(i - n_lat_tiles, 0), 0)),
            _mod_spec(layer, TP),
            _resident((1, D_MODEL), lambda i: (0, 0)),
            _resident((D_MODEL, IN_COLS), lambda i: (0, 0)),
            tab_spec, tab_spec, tab_spec,
            _resident((1, SG_WIDTH), lambda i: (0, 0)),
            _resident((1, SG_WIDTH), lambda i: (0, 0)),
        ],
        out_specs=[
            pl.BlockSpec((TP, DA_WIDTH), row),
            pl.BlockSpec((DA_WIDTH, TP), lambda i: (0, i)),
            pl.BlockSpec((TP, DA_WIDTH), row),
            pl.BlockSpec((TP, SG_WIDTH), row),
            pl.BlockSpec((TP, SG_WIDTH), row),
        ],
        out_shape=[
            jax.ShapeDtypeStruct((N_ALL, DA_WIDTH), BF16),
            jax.ShapeDtypeStruct((DA_WIDTH, N_ALL), BF16),
            jax.ShapeDtypeStruct((N_ALL, DA_WIDTH), BF16),
            jax.ShapeDtypeStruct((N_ALL, SG_WIDTH), BF16),
            jax.ShapeDtypeStruct((N_ALL, SG_WIDTH), BF16),
        ],
        compiler_params=_params(("parallel",), 48),
        name=f"in_proj_l{layer}",
    )(xa, xb, mod, pre_g, w_in, *rope_tabs, ln_g, ln_b)
    return outs


def _rope_tables():
    pos = jnp.arange(SEQ, dtype=jnp.int32)
    row = (pos // GRID_W).astype(F32)
    col = (pos % GRID_W).astype(F32)
    half = DA_HEAD_DIM // 2
    inv = ROPE_BASE ** (-jnp.arange(0, half, 2, dtype=F32) / half)
    ang_r = row[:, None] * inv
    ang_c = col[:, None] * inv
    ang = jnp.concatenate([ang_r, ang_r, ang_c, ang_c], axis=-1)
    cos = jnp.tile(jnp.cos(ang), (1, 2))
    sin = jnp.tile(jnp.sin(ang), (1, 2))
    first = (jnp.arange(DA_VDIM) % 32) < 16
    sa = jnp.where(first, -sin, 0.0)
    sb = jnp.where(first, 0.0, sin)
    ident = jnp.zeros((TP, DA_VDIM), F32)
    return (jnp.concatenate([cos, ident + 1.0], axis=0),
            jnp.concatenate([sa, ident], axis=0),
            jnp.concatenate([sb, ident], axis=0))


def _attn_kernel(n_src, n_cast, lambda_init, q_ref, *refs):
    kt_refs = refs[:n_src]
    v_refs = refs[n_src:2 * n_src]
    lq1_ref, lk1_ref, lq2_ref, lk2_ref, subg_ref = refs[2 * n_src:2 * n_src + 5]
    n_in = 2 * n_src + 5
    cast_in_refs = refs[n_in:n_in + n_cast]
    o_ref = refs[n_in + n_cast]
    cast_out_refs = refs[n_in + n_cast + 1:n_in + 2 * n_cast + 1]
    vext_refs = refs[-n_src:]

    for src_ref, dst_ref in zip(cast_in_refs, cast_out_refs):
        dst_ref[...] = src_ref[...].astype(BF16)

    n_heads = q_ref.shape[1] // DA_VDIM
    head = lambda hd: slice(hd * DA_VDIM, (hd + 1) * DA_VDIM)
    head_ext = lambda hd: slice(hd * 2 * DA_VDIM, (hd + 1) * 2 * DA_VDIM)

    @pl.when(pl.program_id(2) == 0)
    def _():
        for v_ref, ve_ref in zip(v_refs, vext_refs):
            for hd in range(n_heads):
                ve_ref[:, hd * 2 * DA_VDIM:hd * 2 * DA_VDIM + DA_VDIM] = v_ref[:, head(hd)]
                ve_ref[:, hd * 2 * DA_VDIM + DA_VDIM:(hd + 1) * 2 * DA_VDIM] = jnp.ones((v_ref.shape[0], DA_VDIM), BF16)

    segs = []
    for kt_ref, ve_ref in zip(kt_refs, vext_refs):
        n_keys = ve_ref.shape[0]
        for lo in range(0, n_keys, KEY_SEG):
            segs.append((kt_ref, ve_ref, lo, min(lo + KEY_SEG, n_keys)))

    lam = (jnp.exp(jnp.sum(lq1_ref[...] * lk1_ref[...], axis=-1, keepdims=True))
           - jnp.exp(jnp.sum(lq2_ref[...] * lk2_ref[...], axis=-1, keepdims=True)) + lambda_init)
    tq_sub = min(TQ_SUB, q_ref.shape[0])

    def scores(hd, r0):
        q = q_ref[r0:r0 + tq_sub, head(hd)]
        first = lax.broadcasted_iota(jnp.int32, q.shape, 1) < DA_HEAD_DIM
        zero = jnp.zeros_like(q)
        qms = (jnp.where(first, q, zero), jnp.where(first, zero, q))
        return [[jnp.dot(qm, kt_ref[head(hd), lo:hi], preferred_element_type=F32) for kt_ref, _, lo, hi in segs]
                for qm in qms]

    def finish(hd, r0, ss_maps):
        maps = []
        for ss in ss_maps:
            m = functools.reduce(jnp.maximum, [jnp.max(s, axis=-1, keepdims=True) for s in ss])
            ts = [jnp.dot(jnp.exp(s - m).astype(BF16), ve_ref[lo:hi, head_ext(hd)], preferred_element_type=F32)
                  for s, (_, ve_ref, lo, hi) in zip(ss, segs)]
            acc = functools.reduce(jnp.add, ts)
            maps.append(acc[:, :DA_VDIM] / acc[:, DA_VDIM:])
        o = maps[0] - lam * maps[1]
        o_ref[r0:r0 + tq_sub, head(hd)] = (_rms(o) * subg_ref[...] * (1.0 - lambda_init)).astype(BF16)

    items = [(hd, r0) for hd in range(n_heads) for r0 in range(0, q_ref.shape[0], tq_sub)]
    pending = scores(*items[0])
    for idx, item in enumerate(items):
        nxt = scores(*items[idx + 1]) if idx + 1 < len(items) else None
        finish(*item, pending)
        pending = nxt


def _attention(layer, q, kt, v, lam_params, subg, latent, casts=()):
    lambda_init = 0.8 - 0.6 * math.exp(-0.3 * layer)
    ctx_blk0 = N_LAT // CTX_LEN
    small = lambda: pl.BlockSpec((1, DA_HEAD_DIM), lambda b, h, t: (0, 0))
    tail_specs = [small(), small(), small(), small(), pl.BlockSpec((1, DA_VDIM), lambda b, h, t: (0, 0))]
    hw = DA_VDIM if latent else DA_WIDTH
    n_hblk = DA_WIDTH // hw
    kt_ctx = pl.BlockSpec((hw, CTX_LEN), lambda b, h, t: (h, ctx_blk0 + b))
    v_ctx = pl.BlockSpec((CTX_LEN, hw), lambda b, h, t: (ctx_blk0 + b, h))
    if latent:
        tq, n_q, n_rows = TQ, SEQ // TQ, N_LAT
        q_map = lambda b, h, t: (b * n_q + t, h)
        o_map = q_map
        kt_specs = [kt_ctx, pl.BlockSpec((hw, SEQ), lambda b, h, t: (h, b))]
        v_specs = [v_ctx, pl.BlockSpec((SEQ, hw), lambda b, h, t: (b, h))]
        key_lens = [CTX_LEN, SEQ]
    else:
        tq, n_q, n_rows = CTX_LEN, 1, N_CTX
        q_map = lambda b, h, t: (ctx_blk0 + b, h)
        o_map = lambda b, h, t: (b, h)
        kt_specs, v_specs, key_lens = [kt_ctx], [v_ctx], [CTX_LEN]
    n_src = len(key_lens)
    n_steps = BATCH * n_hblk * n_q
    cast_specs = []
    for w in casts:
        rows = next(r for r in range(16, w.shape[0] + 1, 16) if w.shape[0] % r == 0 and w.shape[0] // r <= n_steps)
        last = w.shape[0] // rows - 1
        cast_specs.append(pl.BlockSpec(
            (rows, w.shape[1]),
            lambda b, h, t, last=last: (jnp.minimum((b * n_hblk + h) * n_q + t, last), 0)))
    outs = pl.pallas_call(
        functools.partial(_attn_kernel, n_src, len(casts), lambda_init),
        grid=(BATCH, n_hblk, n_q),
        in_specs=[pl.BlockSpec((tq, hw), q_map)] + kt_specs + v_specs + tail_specs + cast_specs,
        out_specs=[pl.BlockSpec((tq, hw), o_map)] + cast_specs,
        out_shape=[jax.ShapeDtypeStruct((n_rows, DA_WIDTH), BF16)]
                  + [jax.ShapeDtypeStruct(w.shape, BF16) for w in casts],
        scratch_shapes=[pltpu.VMEM((n, 2 * hw), BF16) for n in key_lens],
        compiler_params=_params(("parallel", "parallel", "arbitrary"), 56),
        name=f"attn_l{layer}_{'lat' if latent else 'ctx'}",
    )(q, *([kt] * n_src), *([v] * n_src), *lam_params, subg, *casts)
    return outs[0], outs[1:]


def _out_proj_kernel(n_lat_tiles, aa_ref, ab_ref, u_ref, gn_ref, sgw_ref, sgb_ref, w_ref, pg_ref, mod_ref,
                     xa_ref, xb_ref, o_ref, s_ref):
    i = pl.program_id(0)
    a = jnp.where(i < n_lat_tiles, aa_ref[...], ab_ref[...])
    for c in range(TP // CHUNK):
        rows = slice(c * CHUNK, (c + 1) * CHUNK)
        for g in range(SG_GROUPS):
            cols = slice(g * CHUNK, (g + 1) * CHUNK)
            mixed = jnp.dot(sgw_ref[g], gn_ref[rows, cols], preferred_element_type=F32) + sgb_ref[:, cols]
            s_ref[rows, cols] = (u_ref[rows, cols].astype(F32) * mixed).astype(BF16)
    o = (jnp.dot(a, w_ref[:DA_WIDTH, :], preferred_element_type=F32)
         + jnp.dot(s_ref[...], w_ref[DA_WIDTH:, :], preferred_element_type=F32))
    x = jnp.where(i < n_lat_tiles, xa_ref[...], xb_ref[...])
    o_ref[...] = x + mod_ref[2:3, :] * (_rms(o) * pg_ref[...])


def _out_proj(layer, n_tiles, a_lat, a_ctx, u, gn, sg_w, sg_bias, w_out, post_g, mod, xa, xb, xb_tile0):
    n_lat_tiles = N_LAT // TP
    row = lambda i: (i, 0)
    half = lambda: pl.BlockSpec((TP, DA_WIDTH), row)
    return pl.pallas_call(
        functools.partial(_out_proj_kernel, n_lat_tiles),
        grid=(n_tiles,),
        in_specs=[
            pl.BlockSpec((TP, DA_WIDTH), lambda i: (jnp.minimum(i, n_lat_tiles - 1), 0)),
            pl.BlockSpec((TP, DA_WIDTH), lambda i: (jnp.maximum(i - n_lat_tiles, 0), 0)),
            half(), half(),
            _resident((SG_GROUPS, CHUNK, CHUNK), lambda i: (0, 0, 0)),
            _resident((CHUNK, SG_WIDTH), lambda i: (0, 0)),
            _resident((D_MODEL, D_MODEL), lambda i: (0, 0)),
            _resident((1, D_MODEL), lambda i: (0, 0)),
            _mod_spec(layer, TP),
            pl.BlockSpec((TP, D_MODEL), lambda i: (jnp.minimum(i, n_lat_tiles - 1), 0)),
            pl.BlockSpec((TP, D_MODEL), lambda i: (xb_tile0 + jnp.maximum(i - n_lat_tiles, 0), 0)),
        ],
        out_specs=pl.BlockSpec((TP, D_MODEL), row),
        out_shape=jax.ShapeDtypeStruct((n_tiles * TP, D_MODEL), F32),
        scratch_shapes=[pltpu.VMEM((TP, SG_WIDTH), BF16)],
        compiler_params=_params(("parallel",), 48),
        name=f"out_proj_l{layer}",
    )(a_lat, a_ctx, u, gn, sg_w, sg_bias, w_out, post_g, mod, xa, xb)


def _ffn_kernel(x_ref, mod_ref, g_ref, w1_ref, w3_ref, w2_ref, pg_ref, o_ref):
    x = x_ref[...]
    f = ((_rms(x) * g_ref[...]) * (1.0 + mod_ref[4:5, :]) + mod_ref[3:4, :]).astype(BF16)
    y = None
    for k in range(D_FF // TF_DENSE):
        cols = slice(k * TF_DENSE, (k + 1) * TF_DENSE)
        h1 = jnp.dot(f, w1_ref[:, cols], preferred_element_type=F32)
        h3 = jnp.dot(f, w3_ref[:, cols], preferred_element_type=F32)
        t = jnp.dot((_silu(h1) * h3).astype(BF16), w2_ref[cols, :], preferred_element_type=F32)
        y = t if y is None else y + t
    o_ref[...] = x + mod_ref[5:6, :] * (_rms(y) * pg_ref[...])


def _ffn(layer, x_all, mod, pre_g, w1, w3, w2, post_g):
    row = lambda i: (i, 0)
    return pl.pallas_call(
        _ffn_kernel,
        grid=(N_ALL // TP,),
        in_specs=[
            pl.BlockSpec((TP, D_MODEL), row),
            _mod_spec(layer, TP),
            _resident((1, D_MODEL), lambda i: (0, 0)),
            _resident((D_MODEL, D_FF), lambda i: (0, 0)),
            _resident((D_MODEL, D_FF), lambda i: (0, 0)),
            _resident((D_FF, D_MODEL), lambda i: (0, 0)),
            _resident((1, D_MODEL), lambda i: (0, 0)),
        ],
        out_specs=pl.BlockSpec((TP, D_MODEL), row),
        out_shape=jax.ShapeDtypeStruct((N_ALL, D_MODEL), F32),
        compiler_params=_params(("parallel",), 56),
        name=f"ffn_l{layer}",
    )(x_all, mod, pre_g, w1, w3, w2, post_g)


def _ffn_input(x, mod_ref, g_ref):
    return (_rms(x) * g_ref[...]) * (1.0 + mod_ref[4:5, :]) + mod_ref[3:4, :]


def _router_kernel(x_ref, mod_ref, g_ref, rw_ref, f_ref, ids_ref, rank_ref, rows_ref, cnt_ref):
    f = _ffn_input(x_ref[...], mod_ref, g_ref)
    rw = rw_ref[...]
    f_hi, rw_hi = f.astype(BF16), rw.astype(BF16)
    f_ref[...] = f_hi
    f_lo, rw_lo = (f - f_hi.astype(F32)).astype(BF16), (rw - rw_hi.astype(F32)).astype(BF16)
    logits = (jnp.dot(f_hi, rw_hi, preferred_element_type=F32) + jnp.dot(f_lo, rw_hi, preferred_element_type=F32)
              + jnp.dot(f_hi, rw_lo, preferred_element_type=F32))
    lane = lax.broadcasted_iota(jnp.int32, logits.shape, 1)
    v1 = jnp.max(logits, axis=-1, keepdims=True)
    i1 = jnp.min(jnp.where(logits == v1, lane, N_EXPERTS), axis=-1, keepdims=True)
    rest = jnp.where(lane == i1, -jnp.inf, logits)
    v2 = jnp.max(rest, axis=-1, keepdims=True)
    i2 = jnp.min(jnp.where(rest == v2, lane, N_EXPERTS), axis=-1, keepdims=True)
    e = jnp.exp(v2 - v1)
    w1 = 1.0 / (1.0 + e)

    sel1, sel2 = lane == i1, lane == i2
    onehot = jnp.where(sel1 | sel2, 1.0, 0.0)
    r = lax.broadcasted_iota(jnp.int32, (TM, TM), 0)
    c = lax.broadcasted_iota(jnp.int32, (TM, TM), 1)
    tri = jnp.where(c < r, 1.0, 0.0).astype(BF16)
    prefix = jnp.dot(tri, onehot.astype(BF16), preferred_element_type=F32)
    rank1 = jnp.sum(jnp.where(sel1, prefix, 0.0), axis=-1, keepdims=True)
    rank2 = jnp.sum(jnp.where(sel2, prefix, 0.0), axis=-1, keepdims=True)

    slot0 = lax.broadcasted_iota(jnp.int32, (TM, 2), 1) == 0
    ids_ref[...] = jnp.where(slot0, i1, i2)
    rank_ref[...] = jnp.where(slot0, rank1, rank2).astype(jnp.int32)
    cnt_ref[...] = jnp.sum(onehot, axis=0, keepdims=True).astype(jnp.int32)
    col = lax.broadcasted_iota(jnp.int32, (TM, 128), 1)
    packed = jnp.zeros((TM, 128), F32)
    for k, val in enumerate([i1.astype(F32), i2.astype(F32), rank1, rank2, w1, e * w1]):
        packed = jnp.where(col == k, val, packed)
    rows_ref[...] = packed.T[:8, :]


def _router(layer, x_lat, mod, pre_g, router_w):
    row = lambda i: (i, 0)
    pair = lambda: pl.BlockSpec((TM, 2), row)
    return pl.pallas_call(
        _router_kernel,
        grid=(N_TOK_TILES,),
        in_specs=[
            pl.BlockSpec((TM, D_MODEL), row),
            _mod_spec(layer, TM),
            _resident((1, D_MODEL), lambda i: (0, 0)),
            _resident((D_MODEL, N_EXPERTS), lambda i: (0, 0)),
        ],
        out_specs=[pl.BlockSpec((TM, D_MODEL), row), pair(), pair(),
                   pl.BlockSpec((None, 8, TM), lambda i: (i, 0, 0)),
                   pl.BlockSpec((None, 1, N_EXPERTS), lambda i: (i, 0, 0))],
        out_shape=[jax.ShapeDtypeStruct((N_LAT, D_MODEL), BF16),
                   jax.ShapeDtypeStruct((N_LAT, 2), jnp.int32),
                   jax.ShapeDtypeStruct((N_LAT, 2), jnp.int32),
                   jax.ShapeDtypeStruct((N_TOK_TILES, 8, TM), F32),
                   jax.ShapeDtypeStruct((N_TOK_TILES, 1, N_EXPERTS), jnp.int32)],
        compiler_params=_params(("parallel",), 32),
        name=f"router_l{layer}",
    )(x_lat, mod, pre_g, router_w)


def _routing_tables(cnt):
    cnt = cnt.reshape(N_TOK_TILES, N_EXPERTS)
    run = (cnt + ROW_ALIGN - 1) // ROW_ALIGN * ROW_ALIGN
    zoff = jnp.cumsum(run, axis=1) - run
    grp = jnp.sum(run, axis=0)
    tiles = (grp + TM - 1) // TM
    tile_end = jnp.cumsum(tiles)
    off = (tile_end - tiles) * TM
    gstart = off[None, :] + jnp.cumsum(run, axis=0) - run
    runs = jnp.concatenate([run, zoff, gstart], axis=1).astype(jnp.int32).reshape(N_TOK_TILES, 1, 3 * N_EXPERTS)

    j = jnp.arange(XS_TILES, dtype=jnp.int32)
    n_used = tile_end[-1]
    valid = (j < n_used).astype(jnp.int32)
    jc = jnp.minimum(j, n_used - 1)
    texp = jnp.minimum(jnp.sum(jc[:, None] >= tile_end[None, :], axis=1), N_EXPERTS - 1).astype(jnp.int32)
    pad_lo = (off + grp).astype(jnp.int32)
    pad_hi = (tile_end * TM).astype(jnp.int32).at[N_EXPERTS - 1].set(XS_TILES * TM)
    return runs, texp, valid, pad_lo, pad_hi


def _for_each_run_piece(runs_ref, fn):
    for e in range(N_EXPERTS):
        n = runs_ref[0, e]
        zo = runs_ref[0, N_EXPERTS + e]
        go = runs_ref[0, 2 * N_EXPERTS + e]
        for size in RUN_SIZES:
            done = n & ~(2 * size - 1)
            pl.when((n & size) != 0)(functools.partial(
                fn, pl.multiple_of(zo + done, ROW_ALIGN), pl.multiple_of(go + done, ROW_ALIGN), size))


def _block_position(ids, rank, runs_ref):
    start = jnp.zeros_like(ids)
    for e in range(N_EXPERTS):
        start = jnp.where(ids == e, runs_ref[0, N_EXPERTS + e], start)
    return start + rank


def _dispatch_kernel(pad_lo_ref, pad_hi_ref, runs_ref, runs_prev_ref, tok_ref, f_ref,
                     xs_ref, z_ref, zero_ref, sems, zsem):
    i = pl.program_id(0)
    n = pl.num_programs(0)

    @pl.when(i == 0)
    def _():
        zero_ref[...] = jnp.zeros_like(zero_ref)
        for e in range(N_EXPERTS):
            lo, hi = pad_lo_ref[e] // ROW_ALIGN, pad_hi_ref[e] // ROW_ALIGN
            blk = lambda r: pltpu.make_async_copy(
                zero_ref, xs_ref.at[pl.ds(pl.multiple_of(r * ROW_ALIGN, ROW_ALIGN), ROW_ALIGN)], zsem)
            lax.fori_loop(lo, hi, lambda r, _: blk(r).start(), None)
            lax.fori_loop(lo, hi, lambda r, _: blk(r).wait(), None)

    def piece(slot, zrow, grow, size):
        return pltpu.make_async_copy(z_ref.at[slot, pl.ds(zrow, size)], xs_ref.at[pl.ds(grow, size)], sems.at[slot])

    def step(slot):
        f = f_ref[...]
        tok = tok_ref[...]
        lp = _block_position(tok[0:2, :].astype(jnp.int32), tok[2:4, :].astype(jnp.int32), runs_ref)
        rows = lax.broadcasted_iota(jnp.int32, (Z_ROWS, TM), 0)
        hit0, hit1 = lp[0:1, :] == rows, lp[1:2, :] == rows
        sel = jnp.where(hit0 | hit1, 1.0, 0.0).astype(BF16)
        z_ref[slot, :, :D_MODEL] = jnp.dot(sel, f, preferred_element_type=F32)
        gate = jnp.sum(jnp.where(hit0, tok[4:5, :], 0.0) + jnp.where(hit1, tok[5:6, :], 0.0),
                       axis=-1, keepdims=True)
        z_ref[slot, :, D_MODEL:] = jnp.broadcast_to(gate, (Z_ROWS, D_XS - D_MODEL))
        _for_each_run_piece(runs_ref, lambda zrow, grow, size: piece(slot, zrow, grow, size).start())

        @pl.when(i >= 1)
        def _():
            _for_each_run_piece(runs_prev_ref, lambda zrow, grow, size: piece(1 - slot, zrow, grow, size).wait())

        @pl.when(i == n - 1)
        def _():
            _for_each_run_piece(runs_ref, lambda zrow, grow, size: piece(slot, zrow, grow, size).wait())

    for slot in range(2):
        pl.when(i % 2 == slot)(functools.partial(step, slot))


def _dispatch(layer, f, runs, tok_rows, pad_lo, pad_hi):
    smem = lambda: pl.BlockSpec(memory_space=pltpu.SMEM)
    runs_spec = lambda back: pl.BlockSpec((None, 1, 3 * N_EXPERTS), lambda i: (jnp.maximum(i - back, 0), 0, 0),
                                          memory_space=pltpu.SMEM)
    return pl.pallas_call(
        _dispatch_kernel,
        grid=(N_TOK_TILES,),
        in_specs=[
            smem(), smem(), runs_spec(0), runs_spec(1),
            pl.BlockSpec((None, 8, TM), lambda i: (i, 0, 0)),
            pl.BlockSpec((TM, D_MODEL), lambda i: (i, 0)),
        ],
        out_specs=pl.BlockSpec(memory_space=pl.ANY),
        out_shape=jax.ShapeDtypeStruct((XS_TILES * TM, D_XS), F32),
        scratch_shapes=[pltpu.VMEM((2, Z_ROWS, D_XS), F32), pltpu.VMEM((ROW_ALIGN, D_XS), F32),
                        pltpu.SemaphoreType.DMA((2,)), pltpu.SemaphoreType.DMA(())],
        compiler_params=_params(("arbitrary",), 48),
        name=f"moe_dispatch_l{layer}",
    )(pad_lo, pad_hi, runs, runs, tok_rows, f)


def _expert_kernel(texp_ref, tvalid_ref, xs_ref, w1_ref, w3_ref, w2_ref, ys_ref, xb_ref, acc_ref):
    j = pl.program_id(0)
    k = pl.program_id(1)

    @pl.when((tvalid_ref[j] == 0) & (k == 0))
    def _():
        ys_ref[...] = jnp.zeros_like(ys_ref)

    @pl.when(tvalid_ref[j] == 1)
    def _():
        n_k = D_EXPERT // TF_MOE
        assert n_k >= 2

        def chunk(first, last):
            if first:
                xb = xs_ref[:, :D_MODEL].astype(BF16)
                xb_ref[...] = xb
            else:
                xb = xb_ref[...]
            h1 = jnp.dot(xb, w1_ref[...], preferred_element_type=F32)
            h3 = jnp.dot(xb, w3_ref[...], preferred_element_type=F32)
            y = jnp.dot((_silu(h1) * h3).astype(BF16), w2_ref[...], preferred_element_type=F32)
            if not first:
                y = acc_ref[...] + y
            if last:
                ys_ref[...] = y * xs_ref[:, D_MODEL:D_MODEL + 1]
            else:
                acc_ref[...] = y

        pl.when(k == 0)(functools.partial(chunk, True, False))
        if n_k > 2:
            pl.when((k > 0) & (k < n_k - 1))(functools.partial(chunk, False, False))
        pl.when(k == n_k - 1)(functools.partial(chunk, False, True))


def _experts(layer, xs, texp, tvalid, w1, w3, w2):
    n_k = D_EXPERT // TF_MOE
    row = lambda j, k, texp, tvalid: (j, 0)
    kk = lambda j, k, tvalid: jnp.where(tvalid[j] == 1, k, n_k - 1)
    return pl.pallas_call(
        _expert_kernel,
        grid_spec=pltpu.PrefetchScalarGridSpec(
            num_scalar_prefetch=2,
            grid=(XS_TILES, n_k),
            in_specs=[
                pl.BlockSpec((TM, D_XS), row),
                pl.BlockSpec((None, D_MODEL, TF_MOE), lambda j, k, texp, tvalid: (texp[j], 0, kk(j, k, tvalid))),
                pl.BlockSpec((None, D_MODEL, TF_MOE), lambda j, k, texp, tvalid: (texp[j], 0, kk(j, k, tvalid))),
                pl.BlockSpec((None, TF_MOE, D_MODEL), lambda j, k, texp, tvalid: (texp[j], kk(j, k, tvalid), 0)),
            ],
            out_specs=pl.BlockSpec((TM, D_MODEL), row),
            scratch_shapes=[pltpu.VMEM((TM, D_MODEL), BF16), pltpu.VMEM((TM, D_MODEL), F32)],
        ),
        out_shape=jax.ShapeDtypeStruct((XS_TILES * TM, D_MODEL), F32),
        compiler_params=_params(("arbitrary", "arbitrary"), 56),
        name=f"moe_experts_l{layer}",
    )(texp, tvalid, xs, w1, w3, w2)


def _combine_kernel(runs_ref, runs_next_ref, ys_ref, ids_ref, rank_ref, x_ref, mod_ref, pg_ref, o_ref, buf_ref, sems):
    i = pl.program_id(0)
    n = pl.num_programs(0)

    def piece(slot, zrow, grow, size):
        return pltpu.make_async_copy(ys_ref.at[pl.ds(grow, size)], buf_ref.at[slot, pl.ds(zrow, size)], sems.at[slot])

    def step(slot):
        @pl.when(i == 0)
        def _():
            buf_ref[...] = jnp.zeros_like(buf_ref)
            _for_each_run_piece(runs_ref, lambda zrow, grow, size: piece(slot, zrow, grow, size).start())

        @pl.when(i + 1 < n)
        def _():
            _for_each_run_piece(runs_next_ref, lambda zrow, grow, size: piece(1 - slot, zrow, grow, size).start())

        _for_each_run_piece(runs_ref, lambda zrow, grow, size: piece(slot, zrow, grow, size).wait())
        lp = _block_position(ids_ref[...], rank_ref[...], runs_ref)
        cols = lax.broadcasted_iota(jnp.int32, (TM, Z_ROWS), 1)
        pick = jnp.where((lp[:, 0:1] == cols) | (lp[:, 1:2] == cols), 1.0, 0.0).astype(BF16)
        y = jnp.dot(pick, buf_ref[slot].astype(BF16), preferred_element_type=F32)
        o_ref[...] = x_ref[...] + mod_ref[5:6, :] * (_rms(y) * pg_ref[...])

    for slot in range(2):
        pl.when(i % 2 == slot)(functools.partial(step, slot))


def _combine(layer, ys, runs, ids, rank, x_lat, mod, post_g):
    row = lambda i: (i, 0)
    runs_spec = lambda fwd: pl.BlockSpec(
        (None, 1, 3 * N_EXPERTS), lambda i: (jnp.minimum(i + fwd, N_TOK_TILES - 1), 0, 0), memory_space=pltpu.SMEM)
    return pl.pallas_call(
        _combine_kernel,
        grid=(N_TOK_TILES,),
        in_specs=[
            runs_spec(0), runs_spec(1),
            pl.BlockSpec(memory_space=pl.ANY),
            pl.BlockSpec((TM, 2), row),
            pl.BlockSpec((TM, 2), row),
            pl.BlockSpec((TM, D_MODEL), row),
            _mod_spec(layer, TM),
            _resident((1, D_MODEL), lambda i: (0, 0)),
        ],
        out_specs=pl.BlockSpec((TM, D_MODEL), row),
        out_shape=jax.ShapeDtypeStruct((N_LAT, D_MODEL), F32),
        scratch_shapes=[pltpu.VMEM((2, Z_ROWS, D_MODEL), F32), pltpu.SemaphoreType.DMA((2,))],
        compiler_params=_params(("arbitrary",), 40),
        name=f"moe_combine_l{layer}",
    )(runs, runs, ys, ids, rank, x_lat, mod, post_g)


def kernel(x, c, ctx, c_ctx, ada_w, ada_b, pre_mix_g, post_mix_g, pre_ffn_g, post_ffn_g, w_in, w_out,
           lam_q1, lam_k1, lam_q2, lam_k2, subln_g, sg_ln_g, sg_ln_b, sg_w, sg_b,
           ffn_w1, ffn_w3, ffn_w2, router_w, moe_w1, moe_w3, moe_w2):
    x_lat = x.reshape(N_LAT, D_MODEL)
    x_ctx = ctx.reshape(N_CTX, D_MODEL)
    cc = jnp.concatenate([c, c_ctx[None, :], jnp.zeros((MOD_ROWS - BATCH - 1, D_MODEL), F32)], axis=0)
    mod = _modulation(cc, ada_w, ada_b)
    rope_tabs = _rope_tables()
    n_lat_tiles = N_LAT // TP
    x_all = None
    for l in range(DEPTH):
        last = l == DEPTH - 1
        vec = lambda a: a[l][None, :]
        if l == 0:
            xa, xb, xb_tile0 = x_lat, x_ctx, 0
        else:
            xa, xb, xb_tile0 = x_all, x_all, n_lat_tiles
        if l == 0:
            w_in_l, w_out_l, sg_w_l = w_in[0].astype(BF16), w_out[0].astype(BF16), sg_w[0].astype(BF16)
            late = (ffn_w1, ffn_w3, ffn_w2, moe_w1, moe_w3, moe_w2, w_in[1:], w_out[1:], sg_w[1:])
            casts = [w.reshape(-1, w.shape[-1]) for w in late]
        else:
            w_in_l, w_out_l, sg_w_l, casts = w_in_late[l - 1], w_out_late[l - 1], sg_w_late[l - 1], []
        q, kt, v, u, gn = _in_proj(l, xa, xb, xb_tile0, mod, vec(pre_mix_g), w_in_l,
                                   rope_tabs, vec(sg_ln_g), vec(sg_ln_b))
        lam_params = [vec(lam_q1), vec(lam_k1), vec(lam_q2), vec(lam_k2)]
        a_lat, cast_out = _attention(l, q, kt, v, lam_params, vec(subln_g), latent=True, casts=casts)
        if l == 0:
            late_bf16 = [w.reshape(o.shape) for w, o in zip(cast_out, late)]
            ffn_bf16, moe_bf16 = late_bf16[0:3], late_bf16[3:6]
            w_in_late, w_out_late, sg_w_late = late_bf16[6:9]
        a_ctx = a_lat if last else _attention(l, q, kt, v, lam_params, vec(subln_g), latent=False)[0]
        sg_bias = jnp.repeat(sg_b[l].T, CHUNK, axis=1)
        n_tiles = n_lat_tiles if last else N_ALL // TP
        x_mid = _out_proj(l, n_tiles, a_lat, a_ctx, u, gn, sg_w_l, sg_bias, w_out_l, vec(post_mix_g), mod,
                          xa, xb, xb_tile0)
        if l % 2 == 0:
            i = l // 2
            x_all = _ffn(l, x_mid, mod, vec(pre_ffn_g), ffn_bf16[0][i], ffn_bf16[1][i], ffn_bf16[2][i],
                         vec(post_ffn_g))
        else:
            i = l // 2
            f, ids, rank, tok_rows, cnt = _router(l, x_mid, mod, vec(pre_ffn_g), router_w[i])
            runs, texp, tvalid, pad_lo, pad_hi = _routing_tables(cnt)
            xs = _dispatch(l, f, runs, tok_rows, pad_lo, pad_hi)
            ys = _experts(l, xs, texp, tvalid, moe_bf16[0][i], moe_bf16[1][i], moe_bf16[2][i])
            x_all = _combine(l, ys, runs, ids, rank, x_mid, mod, vec(post_ffn_g))
    return x_all.reshape(BATCH, SEQ, D_MODEL)
```

```python
import functools
import math

import jax
import jax.numpy as jnp
from jax import lax
from jax.experimental import pallas as pl
from jax.experimental.pallas import tpu as pltpu

D_MODEL = 1024
BATCH = 16
SEQ = 2048
DEPTH = 2
GRID_W = 64
CTX_LEN = 256
DA_WIDTH = 512
DA_HEADS = 4
DA_VDIM = 128
DA_HEAD_DIM = 64
SG_WIDTH = 512
SG_GROUPS = 4
CHUNK = 128
ROPE_BASE = 10000.0
D_FF = 2816
N_EXPERTS = 8
D_EXPERT = 3584
EPS = 1e-6
IN_COLS = 3 * DA_WIDTH + 2 * SG_WIDTH

N_LAT = BATCH * SEQ
N_CTX = BATCH * CTX_LEN
N_ALL = N_LAT + N_CTX
MOD_ROWS = 32

TP = 1024
TP_SUB = 512
TM = 512
TQ = 2048
TQ_SUB = 512
KEY_SEG = 2048
TF_DENSE = 256
TF_MOE = 1792
N_TOK_TILES = N_LAT // TM
ROW_ALIGN = 8
RUN_SIZES = (512, 256, 128, 64, 32, 16, 8)
Z_ROWS = 2 * TM + N_EXPERTS * ROW_ALIGN
D_XS = D_MODEL + 128
XS_TILES = (2 * N_LAT + N_TOK_TILES * N_EXPERTS * (ROW_ALIGN - 1)) // TM + 1 + N_EXPERTS
MIB = 2 ** 20

BF16 = jnp.bfloat16
F32 = jnp.float32


def _rms(x):
    return x * lax.rsqrt(jnp.mean(x * x, axis=-1, keepdims=True) + EPS)


def _silu(x):
    return x * jax.nn.sigmoid(x)


def _resident(block_shape, index_map):
    return pl.BlockSpec(block_shape, index_map, pipeline_mode=pl.Buffered(1))


def _params(semantics, vmem_mib):
    return pltpu.CompilerParams(dimension_semantics=semantics, vmem_limit_bytes=vmem_mib * MIB)


def _mod_kernel(cc_ref, w_ref, b_ref, o_ref):
    a = _silu(cc_ref[...]).astype(BF16)
    o_ref[...] = jnp.dot(a, w_ref[...].astype(BF16), preferred_element_type=F32) + b_ref[...]


def _modulation(cc, ada_w, ada_b):
    tn = 1536
    out = pl.pallas_call(
        _mod_kernel,
        grid=(DEPTH, 6 * D_MODEL // tn),
        in_specs=[
            pl.BlockSpec((MOD_ROWS, D_MODEL), lambda l, j: (0, 0)),
            pl.BlockSpec((None, D_MODEL, tn), lambda l, j: (l, 0, j)),
            pl.BlockSpec((None, 1, tn), lambda l, j: (l, 0, j)),
        ],
        out_specs=pl.BlockSpec((None, MOD_ROWS, tn), lambda l, j: (l, 0, j)),
        out_shape=jax.ShapeDtypeStruct((DEPTH, MOD_ROWS, 6 * D_MODEL), F32),
        compiler_params=_params(("parallel", "parallel"), 40),
        name="modulation",
    )(cc, ada_w, ada_b.reshape(DEPTH, 1, 6 * D_MODEL))
    return out.reshape(DEPTH, MOD_ROWS, 6, D_MODEL)


def _mod_spec(layer, tile):
    tiles_per_batch = SEQ // tile
    return pl.BlockSpec((None, None, 6, D_MODEL),
                        lambda i: (layer, jnp.minimum(i // tiles_per_batch, BATCH), 0, 0))


def _in_proj_kernel(n_lat_tiles, xa_ref, xb_ref, mod_ref, g_ref, w_ref, cos_ref, sa_ref, sb_ref,
                    lng_ref, lnb_ref, q_ref, kt_ref, v_ref, u_ref, gn_ref):
    i = pl.program_id(0)
    def project(r0):
        rows = slice(r0, r0 + TP_SUB)
        x = jnp.where(i < n_lat_tiles, xa_ref[rows, :], xb_ref[rows, :])
        h = _rms(x) * g_ref[...]
        h = h * (1.0 + mod_ref[1:2, :]) + mod_ref[0:1, :]
        return jnp.dot(h.astype(BF16), w_ref[...], preferred_element_type=F32)

    def finish(r0, p):
        rows = slice(r0, r0 + TP_SUB)
        cos, sa, sb = cos_ref[rows, :], sa_ref[rows, :], sb_ref[rows, :]

        def rope(t):
            return t * cos + pltpu.roll(t, 112, 1) * sa + pltpu.roll(t, 16, 1) * sb

        for hd in range(DA_HEADS):
            lo, hi = hd * DA_VDIM, (hd + 1) * DA_VDIM
            q_ref[rows, lo:hi] = (rope(p[:, lo:hi]) * (DA_HEAD_DIM ** -0.5)).astype(BF16)
            kt_ref[lo:hi, rows] = rope(p[:, DA_WIDTH + lo:DA_WIDTH + hi]).T.astype(BF16)
        v_ref[rows, :] = p[:, 2 * DA_WIDTH:3 * DA_WIDTH].astype(BF16)
        u_ref[rows, :] = p[:, 3 * DA_WIDTH:3 * DA_WIDTH + SG_WIDTH].astype(BF16)
        gv = p[:, 3 * DA_WIDTH + SG_WIDTH:]
        mu = jnp.mean(gv, axis=-1, keepdims=True)
        var = jnp.mean(jnp.square(gv - mu), axis=-1, keepdims=True)
        gn_ref[rows, :] = ((gv - mu) * lax.rsqrt(var + EPS) * lng_ref[...] + lnb_ref[...]).astype(BF16)

    starts = list(range(0, TP, TP_SUB))
    pending = project(starts[0])
    for idx, r0 in enumerate(starts):
        nxt = project(starts[idx + 1]) if idx + 1 < len(starts) else None
        finish(r0, pending)
        pending = nxt


def _in_proj(layer, xa, xb, xb_tile0, mod, pre_g, w_in, rope_tabs, ln_g, ln_b):
    n_lat_tiles = N_LAT // TP
    n_tiles = N_ALL // TP
    tiles_per_batch = SEQ // TP
    tab_spec = pl.BlockSpec(
        (TP, DA_VDIM), lambda i: (jnp.where(i < n_lat_tiles, i % tiles_per_batch, tiles_per_batch), 0))
    row = lambda i: (i, 0)
    outs = pl.pallas_call(
        functools.partial(_in_proj_kernel, n_lat_tiles),
        grid=(n_tiles,),
        in_specs=[
            pl.BlockSpec((TP, D_MODEL), lambda i: (jnp.minimum(i, n_lat_tiles - 1), 0)),
            pl.BlockSpec((TP, D_MODEL), lambda i: (xb_tile0 + jnp.maximum(i - n_lat_tiles, 0), 0)),
            _mod_spec(layer, TP),
            _resident((1, D_MODEL), lambda i: (0, 0)),
            _resident((D_MODEL, IN_COLS), lambda i: (0, 0)),
            tab_spec, tab_spec, tab_spec,
            _resident((1, SG_WIDTH), lambda i: (0, 0)),
            _resident((1, SG_WIDTH), lambda i: (0, 0)),
        ],
        out_specs=[
            pl.BlockSpec((TP, DA_WIDTH), row),
            pl.BlockSpec((DA_WIDTH, TP), lambda i: (0, i)),
            pl.BlockSpec((TP, DA_WIDTH), row),
            pl.BlockSpec((TP, SG_WIDTH), row),
            pl.BlockSpec((TP, SG_WIDTH), row),
        ],
        out_shape=[
            jax.ShapeDtypeStruct((N_ALL, DA_WIDTH), BF16),
            jax.ShapeDtypeStruct((DA_WIDTH, N_ALL), BF16),
            jax.ShapeDtypeStruct((N_ALL, DA_WIDTH), BF16),
            jax.ShapeDtypeStruct((N_ALL, SG_WIDTH), BF16),
            jax.ShapeDtypeStruct((N_ALL, SG_WIDTH), BF16),
        ],
        compiler_params=_params(("parallel",), 48),
        name=f"in_proj_l{layer}",
    )(xa, xb, mod, pre_g, w_in, *rope_tabs, ln_g, ln_b)
    return outs


def _rope_tables():
    pos = jnp.arange(SEQ, dtype=jnp.int32)
    row = (pos // GRID_W).astype(F32)
    col = (pos % GRID_W).astype(F32)
    half = DA_HEAD_DIM // 2
    inv = ROPE_BASE ** (-jnp.arange(0, half, 2, dtype=F32) / half)
    ang_r = row[:, None] * inv
    ang_c = col[:, None] * inv
    ang = jnp.concatenate([ang_r, ang_r, ang_c, ang_c], axis=-1)
    cos = jnp.tile(jnp.cos(ang), (1, 2))
    sin = jnp.tile(jnp.sin(ang), (1, 2))
    first = (jnp.arange(DA_VDIM) % 32) < 16
    sa = jnp.where(first, -sin, 0.0)
    sb = jnp.where(first, 0.0, sin)
    ident = jnp.zeros((TP, DA_VDIM), F32)
    return (jnp.concatenate([cos, ident + 1.0], axis=0),
            jnp.concatenate([sa, ident], axis=0),
            jnp.concatenate([sb, ident], axis=0))


def _attn_kernel(n_src, n_cast, lambda_init, q_ref, *refs):
    kt_refs = refs[:n_src]
    v_refs = refs[n_src:2 * n_src]
    lq1_ref, lk1_ref, lq2_ref, lk2_ref, subg_ref = refs[2 * n_src:2 * n_src + 5]
    n_in = 2 * n_src + 5
    cast_in_refs = refs[n_in:n_in + n_cast]
    o_ref = refs[n_in + n_cast]
    cast_out_refs = refs[n_in + n_cast + 1:n_in + 2 * n_cast + 1]
    vext_refs = refs[-n_src:]

    for src_ref, dst_ref in zip(cast_in_refs, cast_out_refs):
        dst_ref[...] = src_ref[...].astype(BF16)

    n_heads = q_ref.shape[1] // DA_VDIM
    head = lambda hd: slice(hd * DA_VDIM, (hd + 1) * DA_VDIM)
    head_ext = lambda hd: slice(hd * 2 * DA_VDIM, (hd + 1) * 2 * DA_VDIM)

    @pl.when(pl.program_id(2) == 0)
    def _():
        for v_ref, ve_ref in zip(v_refs, vext_refs):
            for hd in range(n_heads):
                ve_ref[:, hd * 2 * DA_VDIM:hd * 2 * DA_VDIM + DA_VDIM] = v_ref[:, head(hd)]
                ve_ref[:, hd * 2 * DA_VDIM + DA_VDIM:(hd + 1) * 2 * DA_VDIM] = jnp.ones((v_ref.shape[0], DA_VDIM), BF16)

    segs = []
    for kt_ref, ve_ref in zip(kt_refs, vext_refs):
        n_keys = ve_ref.shape[0]
        for lo in range(0, n_keys, KEY_SEG):
            segs.append((kt_ref, ve_ref, lo, min(lo + KEY_SEG, n_keys)))

    lam = (jnp.exp(jnp.sum(lq1_ref[...] * lk1_ref[...], axis=-1, keepdims=True))
           - jnp.exp(jnp.sum(lq2_ref[...] * lk2_ref[...], axis=-1, keepdims=True)) + lambda_init)
    tq_sub = min(TQ_SUB, q_ref.shape[0])

    def scores(hd, r0):
        q = q_ref[r0:r0 + tq_sub, head(hd)]
        first = lax.broadcasted_iota(jnp.int32, q.shape, 1) < DA_HEAD_DIM
        zero = jnp.zeros_like(q)
        qms = (jnp.where(first, q, zero), jnp.where(first, zero, q))
        return [[jnp.dot(qm, kt_ref[head(hd), lo:hi], preferred_element_type=F32) for kt_ref, _, lo, hi in segs]
                for qm in qms]

    def finish(hd, r0, ss_maps):
        maps = []
        for ss in ss_maps:
            m = functools.reduce(jnp.maximum, [jnp.max(s, axis=-1, keepdims=True) for s in ss])
            ts = [jnp.dot(jnp.exp(s - m).astype(BF16), ve_ref[lo:hi, head_ext(hd)], preferred_element_type=F32)
                  for s, (_, ve_ref, lo, hi) in zip(ss, segs)]
            acc = functools.reduce(jnp.add, ts)
            maps.append(acc[:, :DA_VDIM] / acc[:, DA_VDIM:])
        o = maps[0] - lam * maps[1]
        o_ref[r0:r0 + tq_sub, head(hd)] = (_rms(o) * subg_ref[...] * (1.0 - lambda_init)).astype(BF16)

    items = [(hd, r0) for hd in range(n_heads) for r0 in range(0, q_ref.shape[0], tq_sub)]
    pending = scores(*items[0])
    for idx, item in enumerate(items):
        nxt = scores(*items[idx + 1]) if idx + 1 < len(items) else None
        finish(*item, pending)
        pending = nxt


def _attention(layer, q, kt, v, lam_params, subg, latent, casts=()):
    lambda_init = 0.8 - 0.6 * math.exp(-0.3 * layer)
    ctx_blk0 = N_LAT // CTX_LEN
    small = lambda: pl.BlockSpec((1, DA_HEAD_DIM), lambda b, h, t: (0, 0))
    tail_specs = [small(), small(), small(), small(), pl.BlockSpec((1, DA_VDIM), lambda b, h, t: (0, 0))]
    hw = DA_VDIM if latent else DA_WIDTH
    n_hblk = DA_WIDTH // hw
    kt_ctx = pl.BlockSpec((hw, CTX_LEN), lambda b, h, t: (h, ctx_blk0 + b))
    v_ctx = pl.BlockSpec((CTX_LEN, hw), lambda b, h, t: (ctx_blk0 + b, h))
    if latent:
        tq, n_q, n_rows = TQ, SEQ // TQ, N_LAT
        q_map = lambda b, h, t: (b * n_q + t, h)
        o_map = q_map
        kt_specs = [kt_ctx, pl.BlockSpec((hw, SEQ), lambda b, h, t: (h, b))]
        v_specs = [v_ctx, pl.BlockSpec((SEQ, hw), lambda b, h, t: (b, h))]
        key_lens = [CTX_LEN, SEQ]
    else:
        tq, n_q, n_rows = CTX_LEN, 1, N_CTX
        q_map = lambda b, h, t: (ctx_blk0 + b, h)
        o_map = lambda b, h, t: (b, h)
        kt_specs, v_specs, key_lens = [kt_ctx], [v_ctx], [CTX_LEN]
    n_src = len(key_lens)
    n_steps = BATCH * n_hblk * n_q
    cast_specs = []
    for w in casts:
        rows = next(r for r in range(16, w.shape[0] + 1, 16) if w.shape[0] % r == 0 and w.shape[0] // r <= n_steps)
        last = w.shape[0] // rows - 1
        cast_specs.append(pl.BlockSpec(
            (rows, w.shape[1]),
            lambda b, h, t, last=last: (jnp.minimum((b * n_hblk + h) * n_q + t, last), 0)))
    outs = pl.pallas_call(
        functools.partial(_attn_kernel, n_src, len(casts), lambda_init),
        grid=(BATCH, n_hblk, n_q),
        in_specs=[pl.BlockSpec((tq, hw), q_map)] + kt_specs + v_specs + tail_specs + cast_specs,
        out_specs=[pl.BlockSpec((tq, hw), o_map)] + cast_specs,
        out_shape=[jax.ShapeDtypeStruct((n_rows, DA_WIDTH), BF16)]
                  + [jax.ShapeDtypeStruct(w.shape, BF16) for w in casts],
        scratch_shapes=[pltpu.VMEM((n, 2 * hw), BF16) for n in key_lens],
        compiler_params=_params(("parallel", "parallel", "arbitrary"), 56),
        name=f"attn_l{layer}_{'lat' if latent else 'ctx'}",
    )(q, *([kt] * n_src), *([v] * n_src), *lam_params, subg, *casts)
    return outs[0], outs[1:]


def _out_proj_kernel(n_lat_tiles, aa_ref, ab_ref, u_ref, gn_ref, sgw_ref, sgb_ref, w_ref, pg_ref, mod_ref,
                     xa_ref, xb_ref, o_ref, s_ref):
    i = pl.program_id(0)
    s_ref[:, :DA_WIDTH] = jnp.where(i < n_lat_tiles, aa_ref[...], ab_ref[...])
    for c in range(TP // CHUNK):
        rows = slice(c * CHUNK, (c + 1) * CHUNK)
        for g in range(SG_GROUPS):
            cols = slice(g * CHUNK, (g + 1) * CHUNK)
            out_cols = slice(DA_WIDTH + g * CHUNK, DA_WIDTH + (g + 1) * CHUNK)
            mixed = jnp.dot(sgw_ref[g], gn_ref[rows, cols], preferred_element_type=F32) + sgb_ref[:, cols]
            s_ref[rows, out_cols] = (u_ref[rows, cols].astype(F32) * mixed).astype(BF16)
    o = jnp.dot(s_ref[...], w_ref[...], preferred_element_type=F32)
    x = jnp.where(i < n_lat_tiles, xa_ref[...], xb_ref[...])
    o_ref[...] = x + mod_ref[2:3, :] * (_rms(o) * pg_ref[...])


def _out_proj(layer, n_tiles, a_lat, a_ctx, u, gn, sg_w, sg_bias, w_out, post_g, mod, xa, xb, xb_tile0):
    n_lat_tiles = N_LAT // TP
    row = lambda i: (i, 0)
    half = lambda: pl.BlockSpec((TP, DA_WIDTH), row)
    return pl.pallas_call(
        functools.partial(_out_proj_kernel, n_lat_tiles),
        grid=(n_tiles,),
        in_specs=[
            pl.BlockSpec((TP, DA_WIDTH), lambda i: (jnp.minimum(i, n_lat_tiles - 1), 0)),
            pl.BlockSpec((TP, DA_WIDTH), lambda i: (jnp.maximum(i - n_lat_tiles, 0), 0)),
            half(), half(),
            _resident((SG_GROUPS, CHUNK, CHUNK), lambda i: (0, 0, 0)),
            _resident((CHUNK, SG_WIDTH), lambda i: (0, 0)),
            _resident((D_MODEL, D_MODEL), lambda i: (0, 0)),
            _resident((1, D_MODEL), lambda i: (0, 0)),
            _mod_spec(layer, TP),
            pl.BlockSpec((TP, D_MODEL), lambda i: (jnp.minimum(i, n_lat_tiles - 1), 0)),
            pl.BlockSpec((TP, D_MODEL), lambda i: (xb_tile0 + jnp.maximum(i - n_lat_tiles, 0), 0)),
        ],
        out_specs=pl.BlockSpec((TP, D_MODEL), row),
        out_shape=jax.ShapeDtypeStruct((n_tiles * TP, D_MODEL), F32),
        scratch_shapes=[pltpu.VMEM((TP, DA_WIDTH + SG_WIDTH), BF16)],
        compiler_params=_params(("parallel",), 48),
        name=f"out_proj_l{layer}",
    )(a_lat, a_ctx, u, gn, sg_w, sg_bias, w_out, post_g, mod, xa, xb)


def _ffn_kernel(x_ref, mod_ref, g_ref, w1_ref, w3_ref, w2_ref, pg_ref, o_ref):
    x = x_ref[...]
    f = ((_rms(x) * g_ref[...]) * (1.0 + mod_ref[4:5, :]) + mod_ref[3:4, :]).astype(BF16)
    y = None
    for k in range(D_FF // TF_DENSE):
        cols = slice(k * TF_DENSE, (k + 1) * TF_DENSE)
        h1 = jnp.dot(f, w1_ref[:, cols], preferred_element_type=F32)
        h3 = jnp.dot(f, w3_ref[:, cols], preferred_element_type=F32)
        t = jnp.dot((_silu(h1) * h3).astype(BF16), w2_ref[cols, :], preferred_element_type=F32)
        y = t if y is None else y + t
    o_ref[...] = x + mod_ref[5:6, :] * (_rms(y) * pg_ref[...])


def _ffn(layer, x_all, mod, pre_g, w1, w3, w2, post_g):
    row = lambda i: (i, 0)
    return pl.pallas_call(
        _ffn_kernel,
        grid=(N_ALL // TP,),
        in_specs=[
            pl.BlockSpec((TP, D_MODEL), row),
            _mod_spec(layer, TP),
            _resident((1, D_MODEL), lambda i: (0, 0)),
            _resident((D_MODEL, D_FF), lambda i: (0, 0)),
            _resident((D_MODEL, D_FF), lambda i: (0, 0)),
            _resident((D_FF, D_MODEL), lambda i: (0, 0)),
            _resident((1, D_MODEL), lambda i: (0, 0)),
        ],
        out_specs=pl.BlockSpec((TP, D_MODEL), row),
        out_shape=jax.ShapeDtypeStruct((N_ALL, D_MODEL), F32),
        compiler_params=_params(("parallel",), 56),
        name=f"ffn_l{layer}",
    )(x_all, mod, pre_g, w1, w3, w2, post_g)


def _ffn_input(x, mod_ref, g_ref):
    return (_rms(x) * g_ref[...]) * (1.0 + mod_ref[4:5, :]) + mod_ref[3:4, :]


def _router_kernel(x_ref, mod_ref, g_ref, rw_ref, f_ref, ids_ref, rank_ref, rows_ref, cnt_ref):
    f = _ffn_input(x_ref[...], mod_ref, g_ref)
    rw = rw_ref[...]
    f_hi, rw_hi = f.astype(BF16), rw.astype(BF16)
    f_ref[...] = f_hi
    f_lo, rw_lo = (f - f_hi.astype(F32)).astype(BF16), (rw - rw_hi.astype(F32)).astype(BF16)
    logits = (jnp.dot(f_hi, rw_hi, preferred_element_type=F32) + jnp.dot(f_lo, rw_hi, preferred_element_type=F32)
              + jnp.dot(f_hi, rw_lo, preferred_element_type=F32))
    lane = lax.broadcasted_iota(jnp.int32, logits.shape, 1)
    v1 = jnp.max(logits, axis=-1, keepdims=True)
    i1 = jnp.min(jnp.where(logits == v1, lane, N_EXPERTS), axis=-1, keepdims=True)
    rest = jnp.where(lane == i1, -jnp.inf, logits)
    v2 = jnp.max(rest, axis=-1, keepdims=True)
    i2 = jnp.min(jnp.where(rest == v2, lane, N_EXPERTS), axis=-1, keepdims=True)
    e = jnp.exp(v2 - v1)
    w1 = 1.0 / (1.0 + e)

    sel1, sel2 = lane == i1, lane == i2
    onehot = jnp.where(sel1 | sel2, 1.0, 0.0)
    r = lax.broadcasted_iota(jnp.int32, (TM, TM), 0)
    c = lax.broadcasted_iota(jnp.int32, (TM, TM), 1)
    tri = jnp.where(c < r, 1.0, 0.0).astype(BF16)
    prefix = jnp.dot(tri, onehot.astype(BF16), preferred_element_type=F32)
    rank1 = jnp.sum(jnp.where(sel1, prefix, 0.0), axis=-1, keepdims=True)
    rank2 = jnp.sum(jnp.where(sel2, prefix, 0.0), axis=-1, keepdims=True)

    slot0 = lax.broadcasted_iota(jnp.int32, (TM, 2), 1) == 0
    ids_ref[...] = jnp.where(slot0, i1, i2)
    rank_ref[...] = jnp.where(slot0, rank1, rank2).astype(jnp.int32)
    cnt_ref[...] = jnp.sum(onehot, axis=0, keepdims=True).astype(jnp.int32)
    col = lax.broadcasted_iota(jnp.int32, (TM, 128), 1)
    packed = jnp.zeros((TM, 128), F32)
    for k, val in enumerate([i1.astype(F32), i2.astype(F32), rank1, rank2, w1, e * w1]):
        packed = jnp.where(col == k, val, packed)
    rows_ref[...] = packed.T[:8, :]


def _router(layer, x_lat, mod, pre_g, router_w):
    row = lambda i: (i, 0)
    pair = lambda: pl.BlockSpec((TM, 2), row)
    return pl.pallas_call(
        _router_kernel,
        grid=(N_TOK_TILES,),
        in_specs=[
            pl.BlockSpec((TM, D_MODEL), row),
            _mod_spec(layer, TM),
            _resident((1, D_MODEL), lambda i: (0, 0)),
            _resident((D_MODEL, N_EXPERTS), lambda i: (0, 0)),
        ],
        out_specs=[pl.BlockSpec((TM, D_MODEL), row), pair(), pair(),
                   pl.BlockSpec((None, 8, TM), lambda i: (i, 0, 0)),
                   pl.BlockSpec((None, 1, N_EXPERTS), lambda i: (i, 0, 0))],
        out_shape=[jax.ShapeDtypeStruct((N_LAT, D_MODEL), BF16),
                   jax.ShapeDtypeStruct((N_LAT, 2), jnp.int32),
                   jax.ShapeDtypeStruct((N_LAT, 2), jnp.int32),
                   jax.ShapeDtypeStruct((N_TOK_TILES, 8, TM), F32),
                   jax.ShapeDtypeStruct((N_TOK_TILES, 1, N_EXPERTS), jnp.int32)],
        compiler_params=_params(("parallel",), 32),
        name=f"router_l{layer}",
    )(x_lat, mod, pre_g, router_w)


def _routing_tables(cnt):
    cnt = cnt.reshape(N_TOK_TILES, N_EXPERTS)
    run = (cnt + ROW_ALIGN - 1) // ROW_ALIGN * ROW_ALIGN
    zoff = jnp.cumsum(run, axis=1) - run
    grp = jnp.sum(run, axis=0)
    tiles = (grp + TM - 1) // TM
    tile_end = jnp.cumsum(tiles)
    off = (tile_end - tiles) * TM
    gstart = off[None, :] + jnp.cumsum(run, axis=0) - run
    runs = jnp.concatenate([run, zoff, gstart], axis=1).astype(jnp.int32).reshape(N_TOK_TILES, 1, 3 * N_EXPERTS)

    j = jnp.arange(XS_TILES, dtype=jnp.int32)
    n_used = tile_end[-1]
    valid = (j < n_used).astype(jnp.int32)
    jc = jnp.minimum(j, n_used - 1)
    texp = jnp.minimum(jnp.sum(jc[:, None] >= tile_end[None, :], axis=1), N_EXPERTS - 1).astype(jnp.int32)
    pad_lo = (off + grp).astype(jnp.int32)
    pad_hi = (tile_end * TM).astype(jnp.int32).at[N_EXPERTS - 1].set(XS_TILES * TM)
    return runs, texp, valid, pad_lo, pad_hi


def _for_each_run_piece(runs_ref, fn):
    for e in range(N_EXPERTS):
        n = runs_ref[0, e]
        zo = runs_ref[0, N_EXPERTS + e]
        go = runs_ref[0, 2 * N_EXPERTS + e]
        for size in RUN_SIZES:
            done = n & ~(2 * size - 1)
            pl.when((n & size) != 0)(functools.partial(
                fn, pl.multiple_of(zo + done, ROW_ALIGN), pl.multiple_of(go + done, ROW_ALIGN), size))


def _block_position(ids, rank, runs_ref):
    start = jnp.zeros_like(ids)
    for e in range(N_EXPERTS):
        start = jnp.where(ids == e, runs_ref[0, N_EXPERTS + e], start)
    return start + rank


def _dispatch_kernel(pad_lo_ref, pad_hi_ref, runs_ref, runs_prev_ref, tok_ref, f_ref,
                     xs_ref, z_ref, zero_ref, sems, zsem):
    i = pl.program_id(0)
    n = pl.num_programs(0)

    @pl.when(i == 0)
    def _():
        zero_ref[...] = jnp.zeros_like(zero_ref)
        for e in range(N_EXPERTS):
            lo, hi = pad_lo_ref[e] // ROW_ALIGN, pad_hi_ref[e] // ROW_ALIGN
            blk = lambda r: pltpu.make_async_copy(
                zero_ref, xs_ref.at[pl.ds(pl.multiple_of(r * ROW_ALIGN, ROW_ALIGN), ROW_ALIGN)], zsem)
            lax.fori_loop(lo, hi, lambda r, _: blk(r).start(), None)
            lax.fori_loop(lo, hi, lambda r, _: blk(r).wait(), None)

    def piece(slot, zrow, grow, size):
        return pltpu.make_async_copy(z_ref.at[slot, pl.ds(zrow, size)], xs_ref.at[pl.ds(grow, size)], sems.at[slot])

    def step(slot):
        f = f_ref[...]
        tok = tok_ref[...]
        lp = _block_position(tok[0:2, :].astype(jnp.int32), tok[2:4, :].astype(jnp.int32), runs_ref)
        rows = lax.broadcasted_iota(jnp.int32, (Z_ROWS, TM), 0)
        hit0, hit1 = lp[0:1, :] == rows, lp[1:2, :] == rows
        sel = jnp.where(hit0 | hit1, 1.0, 0.0).astype(BF16)
        z_ref[slot, :, :D_MODEL] = jnp.dot(sel, f, preferred_element_type=F32)
        gate = jnp.sum(jnp.where(hit0, tok[4:5, :], 0.0) + jnp.where(hit1, tok[5:6, :], 0.0),
                       axis=-1, keepdims=True)
        z_ref[slot, :, D_MODEL:] = jnp.broadcast_to(gate, (Z_ROWS, D_XS - D_MODEL))
        _for_each_run_piece(runs_ref, lambda zrow, grow, size: piece(slot, zrow, grow, size).start())

        @pl.when(i >= 1)
        def _():
            _for_each_run_piece(runs_prev_ref, lambda zrow, grow, size: piece(1 - slot, zrow, grow, size).wait())

        @pl.when(i == n - 1)
        def _():
            _for_each_run_piece(runs_ref, lambda zrow, grow, size: piece(slot, zrow, grow, size).wait())

    for slot in range(2):
        pl.when(i % 2 == slot)(functools.partial(step, slot))


def _dispatch(layer, f, runs, tok_rows, pad_lo, pad_hi):
    smem = lambda: pl.BlockSpec(memory_space=pltpu.SMEM)
    runs_spec = lambda back: pl.BlockSpec((None, 1, 3 * N_EXPERTS), lambda i: (jnp.maximum(i - back, 0), 0, 0),
                                          memory_space=pltpu.SMEM)
    return pl.pallas_call(
        _dispatch_kernel,
        grid=(N_TOK_TILES,),
        in_specs=[
            smem(), smem(), runs_spec(0), runs_spec(1),
            pl.BlockSpec((None, 8, TM), lambda i: (i, 0, 0)),
            pl.BlockSpec((TM, D_MODEL), lambda i: (i, 0)),
        ],
        out_specs=pl.BlockSpec(memory_space=pl.ANY),
        out_shape=jax.ShapeDtypeStruct((XS_TILES * TM, D_XS), F32),
        scratch_shapes=[pltpu.VMEM((2, Z_ROWS, D_XS), F32), pltpu.VMEM((ROW_ALIGN, D_XS), F32),
                        pltpu.SemaphoreType.DMA((2,)), pltpu.SemaphoreType.DMA(())],
        compiler_params=_params(("arbitrary",), 48),
        name=f"moe_dispatch_l{layer}",
    )(pad_lo, pad_hi, runs, runs, tok_rows, f)


def _expert_kernel(texp_ref, tvalid_ref, xs_ref, w1_ref, w3_ref, w2_ref, ys_ref, xb_ref, acc_ref):
    j = pl.program_id(0)
    k = pl.program_id(1)

    @pl.when((tvalid_ref[j] == 0) & (k == 0))
    def _():
        ys_ref[...] = jnp.zeros_like(ys_ref)

    @pl.when(tvalid_ref[j] == 1)
    def _():
        n_k = D_EXPERT // TF_MOE
        assert n_k >= 2

        def chunk(first, last):
            if first:
                xb = xs_ref[:, :D_MODEL].astype(BF16)
                xb_ref[...] = xb
            else:
                xb = xb_ref[...]
            h1 = jnp.dot(xb, w1_ref[...], preferred_element_type=F32)
            h3 = jnp.dot(xb, w3_ref[...], preferred_element_type=F32)
            y = jnp.dot((_silu(h1) * h3).astype(BF16), w2_ref[...], preferred_element_type=F32)
            if not first:
                y = acc_ref[...] + y
            if last:
                ys_ref[...] = y * xs_ref[:, D_MODEL:D_MODEL + 1]
            else:
                acc_ref[...] = y

        pl.when(k == 0)(functools.partial(chunk, True, False))
        if n_k > 2:
            pl.when((k > 0) & (k < n_k - 1))(functools.partial(chunk, False, False))
        pl.when(k == n_k - 1)(functools.partial(chunk, False, True))


def _experts(layer, xs, texp, tvalid, w1, w3, w2):
    n_k = D_EXPERT // TF_MOE
    row = lambda j, k, texp, tvalid: (j, 0)
    kk = lambda j, k, tvalid: jnp.where(tvalid[j] == 1, k, n_k - 1)
    return pl.pallas_call(
        _expert_kernel,
        grid_spec=pltpu.PrefetchScalarGridSpec(
            num_scalar_prefetch=2,
            grid=(XS_TILES, n_k),
            in_specs=[
                pl.BlockSpec((TM, D_XS), row),
                pl.BlockSpec((None, D_MODEL, TF_MOE), lambda j, k, texp, tvalid: (texp[j], 0, kk(j, k, tvalid))),
                pl.BlockSpec((None, D_MODEL, TF_MOE), lambda j, k, texp, tvalid: (texp[j], 0, kk(j, k, tvalid))),
                pl.BlockSpec((None, TF_MOE, D_MODEL), lambda j, k, texp, tvalid: (texp[j], kk(j, k, tvalid), 0)),
            ],
            out_specs=pl.BlockSpec((TM, D_MODEL), row),
            scratch_shapes=[pltpu.VMEM((TM, D_MODEL), BF16), pltpu.VMEM((TM, D_MODEL), F32)],
        ),
        out_shape=jax.ShapeDtypeStruct((XS_TILES * TM, D_MODEL), F32),
        compiler_params=_params(("arbitrary", "arbitrary"), 56),
        name=f"moe_experts_l{layer}",
    )(texp, tvalid, xs, w1, w3, w2)


def _combine_kernel(runs_ref, runs_next_ref, ys_ref, ids_ref, rank_ref, x_ref, mod_ref, pg_ref, o_ref, buf_ref, sems):
    i = pl.program_id(0)
    n = pl.num_programs(0)

    def piece(slot, zrow, grow, size):
        return pltpu.make_async_copy(ys_ref.at[pl.ds(grow, size)], buf_ref.at[slot, pl.ds(zrow, size)], sems.at[slot])

    def step(slot):
        @pl.when(i == 0)
        def _():
            buf_ref[...] = jnp.zeros_like(buf_ref)
            _for_each_run_piece(runs_ref, lambda zrow, grow, size: piece(slot, zrow, grow, size).start())

        @pl.when(i + 1 < n)
        def _():
            _for_each_run_piece(runs_next_ref, lambda zrow, grow, size: piece(1 - slot, zrow, grow, size).start())

        _for_each_run_piece(runs_ref, lambda zrow, grow, size: piece(slot, zrow, grow, size).wait())
        lp = _block_position(ids_ref[...], rank_ref[...], runs_ref)
        cols = lax.broadcasted_iota(jnp.int32, (TM, Z_ROWS), 1)
        pick = jnp.where((lp[:, 0:1] == cols) | (lp[:, 1:2] == cols), 1.0, 0.0).astype(BF16)
        y = jnp.dot(pick, buf_ref[slot].astype(BF16), preferred_element_type=F32)
        o_ref[...] = x_ref[...] + mod_ref[5:6, :] * (_rms(y) * pg_ref[...])

    for slot in range(2):
        pl.when(i % 2 == slot)(functools.partial(step, slot))


def _combine(layer, ys, runs, ids, rank, x_lat, mod, post_g):
    row = lambda i: (i, 0)
    runs_spec = lambda fwd: pl.BlockSpec(
        (None, 1, 3 * N_EXPERTS), lambda i: (jnp.minimum(i + fwd, N_TOK_TILES - 1), 0, 0), memory_space=pltpu.SMEM)
    return pl.pallas_call(
        _combine_kernel,
        grid=(N_TOK_TILES,),
        in_specs=[
            runs_spec(0), runs_spec(1),
            pl.BlockSpec(memory_space=pl.ANY),
            pl.BlockSpec((TM, 2), row),
            pl.BlockSpec((TM, 2), row),
            pl.BlockSpec((TM, D_MODEL), row),
            _mod_spec(layer, TM),
            _resident((1, D_MODEL), lambda i: (0, 0)),
        ],
        out_specs=pl.BlockSpec((TM, D_MODEL), row),
        out_shape=jax.ShapeDtypeStruct((N_LAT, D_MODEL), F32),
        scratch_shapes=[pltpu.VMEM((2, Z_ROWS, D_MODEL), F32), pltpu.SemaphoreType.DMA((2,))],
        compiler_params=_params(("arbitrary",), 40),
        name=f"moe_combine_l{layer}",
    )(runs, runs, ys, ids, rank, x_lat, mod, post_g)


def kernel(x, c, ctx, c_ctx, ada_w, ada_b, pre_mix_g, post_mix_g, pre_ffn_g, post_ffn_g, w_in, w_out,
           lam_q1, lam_k1, lam_q2, lam_k2, subln_g, sg_ln_g, sg_ln_b, sg_w, sg_b,
           ffn_w1, ffn_w3, ffn_w2, router_w, moe_w1, moe_w3, moe_w2):
    x_lat = x.reshape(N_LAT, D_MODEL)
    x_ctx = ctx.reshape(N_CTX, D_MODEL)
    cc = jnp.concatenate([c, c_ctx[None, :], jnp.zeros((MOD_ROWS - BATCH - 1, D_MODEL), F32)], axis=0)
    mod = _modulation(cc, ada_w, ada_b)
    rope_tabs = _rope_tables()
    n_lat_tiles = N_LAT // TP
    x_all = None
    for l in range(DEPTH):
        last = l == DEPTH - 1
        vec = lambda a: a[l][None, :]
        if l == 0:
            xa, xb, xb_tile0 = x_lat, x_ctx, 0
        else:
            xa, xb, xb_tile0 = x_all, x_all, n_lat_tiles
        if l == 0:
            w_in_l, w_out_l, sg_w_l = w_in[0].astype(BF16), w_out[0].astype(BF16), sg_w[0].astype(BF16)
            late = (ffn_w1, ffn_w3, ffn_w2, moe_w1, moe_w3, moe_w2, w_in[1:], w_out[1:], sg_w[1:])
            casts = [w.reshape(-1, w.shape[-1]) for w in late]
        else:
            w_in_l, w_out_l, sg_w_l, casts = w_in_late[l - 1], w_out_late[l - 1], sg_w_late[l - 1], []
        q, kt, v, u, gn = _in_proj(l, xa, xb, xb_tile0, mod, vec(pre_mix_g), w_in_l,
                                   rope_tabs, vec(sg_ln_g), vec(sg_ln_b))
        lam_params = [vec(lam_q1), vec(lam_k1), vec(lam_q2), vec(lam_k2)]
        a_lat, cast_out = _attention(l, q, kt, v, lam_params, vec(subln_g), latent=True, casts=casts)
        if l == 0:
            late_bf16 = [w.reshape(o.shape) for w, o in zip(cast_out, late)]
            ffn_bf16, moe_bf16 = late_bf16[0:3], late_bf16[3:6]
            w_in_late, w_out_late, sg_w_late = late_bf16[6:9]
        a_ctx = a_lat if last else _attention(l, q, kt, v, lam_params, vec(subln_g), latent=False)[0]
        sg_bias = jnp.repeat(sg_b[l].T, CHUNK, axis=1)
        n_tiles = n_lat_tiles if last else N_ALL // TP
        x_mid = _out_proj(l, n_tiles, a_lat, a_ctx, u, gn, sg_w_l, sg_bias, w_out_l, vec(post_mix_g), mod,
                          xa, xb, xb_tile0)
        if l % 2 == 0:
            i = l // 2
            x_all = _ffn(l, x_mid, mod, vec(pre_ffn_g), ffn_bf16[0][i], ffn_bf16[1][i], ffn_bf16[2][i],
                         vec(post_ffn_g))
        else:
            i = l // 2
            f, ids, rank, tok_rows, cnt = _router(l, x_mid, mod, vec(pre_ffn_g), router_w[i])
            runs, texp, tvalid, pad_lo, pad_hi = _routing_tables(cnt)
            xs = _dispatch(l, f, runs, tok_rows, pad_lo, pad_hi)
            ys = _experts(l, xs, texp, tvalid, moe_bf16[0][i], moe_bf16[1][i], moe_bf16[2][i])
            x_all = _combine(l, ys, runs, ids, rank, x_mid, mod, vec(post_ffn_g))
    return x_all.reshape(BATCH, SEQ, D_MODEL)
```

```python
import functools
import math

import jax
import jax.numpy as jnp
from jax import lax
from jax.experimental import pallas as pl
from jax.experimental.pallas import tpu as pltpu

D_MODEL = 1024
BATCH = 16
SEQ = 2048
DEPTH = 2
GRID_W = 64
CTX_LEN = 256
DA_WIDTH = 512
DA_HEADS = 4
DA_VDIM = 128
DA_HEAD_DIM = 64
SG_WIDTH = 512
SG_GROUPS = 4
CHUNK = 128
ROPE_BASE = 10000.0
D_FF = 2816
N_EXPERTS = 8
D_EXPERT = 3584
EPS = 1e-6
IN_COLS = 3 * DA_WIDTH + 2 * SG_WIDTH

N_LAT = BATCH * SEQ
N_CTX = BATCH * CTX_LEN
N_ALL = N_LAT + N_CTX
MOD_ROWS = 32

TP = 1024
TP_SUB = 512
TM = 512
TQ = 2048
TQ_SUB = 512
KEY_SEG = 2048
TF_DENSE = 256
TF_MOE = 1792
N_TOK_TILES = N_LAT // TM
ROUTE_ROWS = 16
ROW_ALIGN = 8
RUN_SIZES = (512, 256, 128, 64, 32, 16, 8)
Z_ROWS = 2 * TM + N_EXPERTS * ROW_ALIGN
D_XS = D_MODEL + 128
XS_TILES = (2 * N_LAT + N_TOK_TILES * N_EXPERTS * (ROW_ALIGN - 1)) // TM + 1 + N_EXPERTS
MIB = 2 ** 20

BF16 = jnp.bfloat16
F32 = jnp.float32


def _rms(x):
    return x * lax.rsqrt(jnp.mean(x * x, axis=-1, keepdims=True) + EPS)


def _silu(x):
    return x * jax.nn.sigmoid(x)


def _resident(block_shape, index_map):
    return pl.BlockSpec(block_shape, index_map, pipeline_mode=pl.Buffered(1))


def _params(semantics, vmem_mib):
    return pltpu.CompilerParams(dimension_semantics=semantics, vmem_limit_bytes=vmem_mib * MIB)


def _mod_kernel(cc_ref, w_ref, b_ref, o_ref):
    a = _silu(cc_ref[...]).astype(BF16)
    o_ref[...] = jnp.dot(a, w_ref[...].astype(BF16), preferred_element_type=F32) + b_ref[...]


def _modulation(cc, ada_w, ada_b):
    tn = 1536
    out = pl.pallas_call(
        _mod_kernel,
        grid=(DEPTH, 6 * D_MODEL // tn),
        in_specs=[
            pl.BlockSpec((MOD_ROWS, D_MODEL), lambda l, j: (0, 0)),
            pl.BlockSpec((None, D_MODEL, tn), lambda l, j: (l, 0, j)),
            pl.BlockSpec((None, 1, tn), lambda l, j: (l, 0, j)),
        ],
        out_specs=pl.BlockSpec((None, MOD_ROWS, tn), lambda l, j: (l, 0, j)),
        out_shape=jax.ShapeDtypeStruct((DEPTH, MOD_ROWS, 6 * D_MODEL), F32),
        compiler_params=_params(("parallel", "parallel"), 40),
        name="modulation",
    )(cc, ada_w, ada_b.reshape(DEPTH, 1, 6 * D_MODEL))
    return out.reshape(DEPTH, MOD_ROWS, 6, D_MODEL)


def _mod_spec(layer, tile):
    tiles_per_batch = SEQ // tile
    return pl.BlockSpec((None, None, 6, D_MODEL),
                        lambda i: (layer, jnp.minimum(i // tiles_per_batch, BATCH), 0, 0))


def _in_proj_kernel(n_lat_tiles, xa_ref, xb_ref, mod_ref, g_ref, w_ref, cos_ref, sa_ref, sb_ref,
                    lng_ref, lnb_ref, q_ref, kt_ref, v_ref, u_ref, gn_ref):
    i = pl.program_id(0)
    def project(r0):
        rows = slice(r0, r0 + TP_SUB)
        x = jnp.where(i < n_lat_tiles, xa_ref[rows, :], xb_ref[rows, :])
        h = _rms(x) * g_ref[...]
        h = h * (1.0 + mod_ref[1:2, :]) + mod_ref[0:1, :]
        return jnp.dot(h.astype(BF16), w_ref[...], preferred_element_type=F32)

    def finish(r0, p):
        rows = slice(r0, r0 + TP_SUB)
        cos, sa, sb = cos_ref[rows, :], sa_ref[rows, :], sb_ref[rows, :]

        def rope(t):
            return t * cos + pltpu.roll(t, 112, 1) * sa + pltpu.roll(t, 16, 1) * sb

        for hd in range(DA_HEADS):
            lo, hi = hd * DA_VDIM, (hd + 1) * DA_VDIM
            q_ref[rows, lo:hi] = (rope(p[:, lo:hi]) * (DA_HEAD_DIM ** -0.5)).astype(BF16)
            kt_ref[lo:hi, rows] = rope(p[:, DA_WIDTH + lo:DA_WIDTH + hi]).T.astype(BF16)
        v_ref[rows, :] = p[:, 2 * DA_WIDTH:3 * DA_WIDTH].astype(BF16)
        u_ref[rows, :] = p[:, 3 * DA_WIDTH:3 * DA_WIDTH + SG_WIDTH].astype(BF16)
        gv = p[:, 3 * DA_WIDTH + SG_WIDTH:]
        mu = jnp.mean(gv, axis=-1, keepdims=True)
        var = jnp.mean(jnp.square(gv - mu), axis=-1, keepdims=True)
        gn_ref[rows, :] = ((gv - mu) * lax.rsqrt(var + EPS) * lng_ref[...] + lnb_ref[...]).astype(BF16)

    starts = list(range(0, TP, TP_SUB))
    pending = project(starts[0])
    for idx, r0 in enumerate(starts):
        nxt = project(starts[idx + 1]) if idx + 1 < len(starts) else None
        finish(r0, pending)
        pending = nxt


def _in_proj(layer, xa, xb, xb_tile0, mod, pre_g, w_in, rope_tabs, ln_g, ln_b):
    n_lat_tiles = N_LAT // TP
    n_tiles = N_ALL // TP
    tiles_per_batch = SEQ // TP
    tab_spec = pl.BlockSpec(
        (TP, DA_VDIM), lambda i: (jnp.where(i < n_lat_tiles, i % tiles_per_batch, tiles_per_batch), 0))
    row = lambda i: (i, 0)
    outs = pl.pallas_call(
        functools.partial(_in_proj_kernel, n_lat_tiles),
        grid=(n_tiles,),
        in_specs=[
            pl.BlockSpec((TP, D_MODEL), lambda i: (jnp.minimum(i, n_lat_tiles - 1), 0)),
            pl.BlockSpec((TP, D_MODEL), lambda i: (xb_tile0 + jnp.maximum(i - n_lat_tiles, 0), 0)),
            _mod_spec(layer, TP),
            _resident((1, D_MODEL), lambda i: (0, 0)),
            _resident((D_MODEL, IN_COLS), lambda i: (0, 0)),
            tab_spec, tab_spec, tab_spec,
            _resident((1, SG_WIDTH), lambda i: (0, 0)),
            _resident((1, SG_WIDTH), lambda i: (0, 0)),
        ],
        out_specs=[
            pl.BlockSpec((TP, DA_WIDTH), row),
            pl.BlockSpec((DA_WIDTH, TP), lambda i: (0, i)),
            pl.BlockSpec((TP, DA_WIDTH), row),
            pl.BlockSpec((TP, SG_WIDTH), row),
            pl.BlockSpec((TP, SG_WIDTH), row),
        ],
        out_shape=[
            jax.ShapeDtypeStruct((N_ALL, DA_WIDTH), BF16),
            jax.ShapeDtypeStruct((DA_WIDTH, N_ALL), BF16),
            jax.ShapeDtypeStruct((N_ALL, DA_WIDTH), BF16),
            jax.ShapeDtypeStruct((N_ALL, SG_WIDTH), BF16),
            jax.ShapeDtypeStruct((N_ALL, SG_WIDTH), BF16),
        ],
        compiler_params=_params(("parallel",), 48),
        name=f"in_proj_l{layer}",
    )(xa, xb, mod, pre_g, w_in, *rope_tabs, ln_g, ln_b)
    return outs


def _rope_tables():
    pos = jnp.arange(SEQ, dtype=jnp.int32)
    row = (pos // GRID_W).astype(F32)
    col = (pos % GRID_W).astype(F32)
    half = DA_HEAD_DIM // 2
    inv = ROPE_BASE ** (-jnp.arange(0, half, 2, dtype=F32) / half)
    ang_r = row[:, None] * inv
    ang_c = col[:, None] * inv
    ang = jnp.concatenate([ang_r, ang_r, ang_c, ang_c], axis=-1)
    cos = jnp.tile(jnp.cos(ang), (1, 2))
    sin = jnp.tile(jnp.sin(ang), (1, 2))
    first = (jnp.arange(DA_VDIM) % 32) < 16
    sa = jnp.where(first, -sin, 0.0)
    sb = jnp.where(first, 0.0, sin)
    ident = jnp.zeros((TP, DA_VDIM), F32)
    return (jnp.concatenate([cos, ident + 1.0], axis=0),
            jnp.concatenate([sa, ident], axis=0),
            jnp.concatenate([sb, ident], axis=0))


def _attn_kernel(n_src, n_cast, lambda_init, q_ref, *refs):
    kt_refs = refs[:n_src]
    v_refs = refs[n_src:2 * n_src]
    lq1_ref, lk1_ref, lq2_ref, lk2_ref, subg_ref = refs[2 * n_src:2 * n_src + 5]
    n_in = 2 * n_src + 5
    cast_in_refs = refs[n_in:n_in + n_cast]
    o_ref = refs[n_in + n_cast]
    cast_out_refs = refs[n_in + n_cast + 1:n_in + 2 * n_cast + 1]
    vext_refs = refs[-n_src:]

    for src_ref, dst_ref in zip(cast_in_refs, cast_out_refs):
        dst_ref[...] = src_ref[...].astype(BF16)

    n_heads = q_ref.shape[1] // DA_VDIM
    head = lambda hd: slice(hd * DA_VDIM, (hd + 1) * DA_VDIM)
    head_ext = lambda hd: slice(hd * 2 * DA_VDIM, (hd + 1) * 2 * DA_VDIM)

    @pl.when(pl.program_id(2) == 0)
    def _():
        for v_ref, ve_ref in zip(v_refs, vext_refs):
            for hd in range(n_heads):
                ve_ref[:, hd * 2 * DA_VDIM:hd * 2 * DA_VDIM + DA_VDIM] = v_ref[:, head(hd)]
                ve_ref[:, hd * 2 * DA_VDIM + DA_VDIM:(hd + 1) * 2 * DA_VDIM] = jnp.ones((v_ref.shape[0], DA_VDIM), BF16)

    segs = []
    for kt_ref, ve_ref in zip(kt_refs, vext_refs):
        n_keys = ve_ref.shape[0]
        for lo in range(0, n_keys, KEY_SEG):
            segs.append((kt_ref, ve_ref, lo, min(lo + KEY_SEG, n_keys)))

    lam = (jnp.exp(jnp.sum(lq1_ref[...] * lk1_ref[...], axis=-1, keepdims=True))
           - jnp.exp(jnp.sum(lq2_ref[...] * lk2_ref[...], axis=-1, keepdims=True)) + lambda_init)
    tq_sub = min(TQ_SUB, q_ref.shape[0])

    def scores(hd, r0):
        q = q_ref[r0:r0 + tq_sub, head(hd)]
        first = lax.broadcasted_iota(jnp.int32, q.shape, 1) < DA_HEAD_DIM
        zero = jnp.zeros_like(q)
        qms = (jnp.where(first, q, zero), jnp.where(first, zero, q))
        return [[jnp.dot(qm, kt_ref[head(hd), lo:hi], preferred_element_type=F32) for kt_ref, _, lo, hi in segs]
                for qm in qms]

    def finish(hd, r0, ss_maps):
        maps = []
        for ss in ss_maps:
            m = functools.reduce(jnp.maximum, [jnp.max(s, axis=-1, keepdims=True) for s in ss])
            ts = [jnp.dot(jnp.exp(s - m).astype(BF16), ve_ref[lo:hi, head_ext(hd)], preferred_element_type=F32)
                  for s, (_, ve_ref, lo, hi) in zip(ss, segs)]
            acc = functools.reduce(jnp.add, ts)
            maps.append(acc[:, :DA_VDIM] / acc[:, DA_VDIM:])
        o = maps[0] - lam * maps[1]
        o_ref[r0:r0 + tq_sub, head(hd)] = (_rms(o) * subg_ref[...] * (1.0 - lambda_init)).astype(BF16)

    items = [(hd, r0) for hd in range(n_heads) for r0 in range(0, q_ref.shape[0], tq_sub)]
    pending = scores(*items[0])
    for idx, item in enumerate(items):
        nxt = scores(*items[idx + 1]) if idx + 1 < len(items) else None
        finish(*item, pending)
        pending = nxt


def _attention(layer, q, kt, v, lam_params, subg, latent, casts=()):
    lambda_init = 0.8 - 0.6 * math.exp(-0.3 * layer)
    ctx_blk0 = N_LAT // CTX_LEN
    small = lambda: pl.BlockSpec((1, DA_HEAD_DIM), lambda b, h, t: (0, 0))
    tail_specs = [small(), small(), small(), small(), pl.BlockSpec((1, DA_VDIM), lambda b, h, t: (0, 0))]
    hw = DA_VDIM if latent else DA_WIDTH
    n_hblk = DA_WIDTH // hw
    kt_ctx = pl.BlockSpec((hw, CTX_LEN), lambda b, h, t: (h, ctx_blk0 + b))
    v_ctx = pl.BlockSpec((CTX_LEN, hw), lambda b, h, t: (ctx_blk0 + b, h))
    if latent:
        tq, n_q, n_rows = TQ, SEQ // TQ, N_LAT
        q_map = lambda b, h, t: (b * n_q + t, h)
        o_map = q_map
        kt_specs = [kt_ctx, pl.BlockSpec((hw, SEQ), lambda b, h, t: (h, b))]
        v_specs = [v_ctx, pl.BlockSpec((SEQ, hw), lambda b, h, t: (b, h))]
        key_lens = [CTX_LEN, SEQ]
    else:
        tq, n_q, n_rows = CTX_LEN, 1, N_CTX
        q_map = lambda b, h, t: (ctx_blk0 + b, h)
        o_map = lambda b, h, t: (b, h)
        kt_specs, v_specs, key_lens = [kt_ctx], [v_ctx], [CTX_LEN]
    n_src = len(key_lens)
    n_steps = BATCH * n_hblk * n_q
    cast_specs = []
    for w in casts:
        rows = next(r for r in range(16, w.shape[0] + 1, 16) if w.shape[0] % r == 0 and w.shape[0] // r <= n_steps)
        last = w.shape[0] // rows - 1
        cast_specs.append(pl.BlockSpec(
            (rows, w.shape[1]),
            lambda b, h, t, last=last: (jnp.minimum((b * n_hblk + h) * n_q + t, last), 0)))
    outs = pl.pallas_call(
        functools.partial(_attn_kernel, n_src, len(casts), lambda_init),
        grid=(BATCH, n_hblk, n_q),
        in_specs=[pl.BlockSpec((tq, hw), q_map)] + kt_specs + v_specs + tail_specs + cast_specs,
        out_specs=[pl.BlockSpec((tq, hw), o_map)] + cast_specs,
        out_shape=[jax.ShapeDtypeStruct((n_rows, DA_WIDTH), BF16)]
                  + [jax.ShapeDtypeStruct(w.shape, BF16) for w in casts],
        scratch_shapes=[pltpu.VMEM((n, 2 * hw), BF16) for n in key_lens],
        compiler_params=_params(("parallel", "parallel", "arbitrary"), 56),
        name=f"attn_l{layer}_{'lat' if latent else 'ctx'}",
    )(q, *([kt] * n_src), *([v] * n_src), *lam_params, subg, *casts)
    return outs[0], outs[1:]


def _out_proj_kernel(n_lat_tiles, aa_ref, ab_ref, u_ref, gn_ref, sgw_ref, sgb_ref, w_ref, pg_ref, mod_ref,
                     xa_ref, xb_ref, o_ref, s_ref):
    i = pl.program_id(0)
    s_ref[:, :DA_WIDTH] = jnp.where(i < n_lat_tiles, aa_ref[...], ab_ref[...])
    for c in range(TP // CHUNK):
        rows = slice(c * CHUNK, (c + 1) * CHUNK)
        for g in range(SG_GROUPS):
            cols = slice(g * CHUNK, (g + 1) * CHUNK)
            out_cols = slice(DA_WIDTH + g * CHUNK, DA_WIDTH + (g + 1) * CHUNK)
            mixed = jnp.dot(sgw_ref[g], gn_ref[rows, cols], preferred_element_type=F32) + sgb_ref[:, cols]
            s_ref[rows, out_cols] = (u_ref[rows, cols].astype(F32) * mixed).astype(BF16)
    o = jnp.dot(s_ref[...], w_ref[...], preferred_element_type=F32)
    x = jnp.where(i < n_lat_tiles, xa_ref[...], xb_ref[...])
    o_ref[...] = x + mod_ref[2:3, :] * (_rms(o) * pg_ref[...])


def _out_proj(layer, n_tiles, a_lat, a_ctx, u, gn, sg_w, sg_bias, w_out, post_g, mod, xa, xb, xb_tile0):
    n_lat_tiles = N_LAT // TP
    row = lambda i: (i, 0)
    half = lambda: pl.BlockSpec((TP, DA_WIDTH), row)
    return pl.pallas_call(
        functools.partial(_out_proj_kernel, n_lat_tiles),
        grid=(n_tiles,),
        in_specs=[
            pl.BlockSpec((TP, DA_WIDTH), lambda i: (jnp.minimum(i, n_lat_tiles - 1), 0)),
            pl.BlockSpec((TP, DA_WIDTH), lambda i: (jnp.maximum(i - n_lat_tiles, 0), 0)),
            half(), half(),
            _resident((SG_GROUPS, CHUNK, CHUNK), lambda i: (0, 0, 0)),
            _resident((CHUNK, SG_WIDTH), lambda i: (0, 0)),
            _resident((D_MODEL, D_MODEL), lambda i: (0, 0)),
            _resident((1, D_MODEL), lambda i: (0, 0)),
            _mod_spec(layer, TP),
            pl.BlockSpec((TP, D_MODEL), lambda i: (jnp.minimum(i, n_lat_tiles - 1), 0)),
            pl.BlockSpec((TP, D_MODEL), lambda i: (xb_tile0 + jnp.maximum(i - n_lat_tiles, 0), 0)),
        ],
        out_specs=pl.BlockSpec((TP, D_MODEL), row),
        out_shape=jax.ShapeDtypeStruct((n_tiles * TP, D_MODEL), F32),
        scratch_shapes=[pltpu.VMEM((TP, DA_WIDTH + SG_WIDTH), BF16)],
        compiler_params=_params(("parallel",), 48),
        name=f"out_proj_l{layer}",
    )(a_lat, a_ctx, u, gn, sg_w, sg_bias, w_out, post_g, mod, xa, xb)


def _ffn_kernel(x_ref, mod_ref, g_ref, w1_ref, w3_ref, w2_ref, pg_ref, o_ref):
    x = x_ref[...]
    f = ((_rms(x) * g_ref[...]) * (1.0 + mod_ref[4:5, :]) + mod_ref[3:4, :]).astype(BF16)
    y = None
    for k in range(D_FF // TF_DENSE):
        cols = slice(k * TF_DENSE, (k + 1) * TF_DENSE)
        h1 = jnp.dot(f, w1_ref[:, cols], preferred_element_type=F32)
        h3 = jnp.dot(f, w3_ref[:, cols], preferred_element_type=F32)
        t = jnp.dot((_silu(h1) * h3).astype(BF16), w2_ref[cols, :], preferred_element_type=F32)
        y = t if y is None else y + t
    o_ref[...] = x + mod_ref[5:6, :] * (_rms(y) * pg_ref[...])


def _ffn(layer, x_all, mod, pre_g, w1, w3, w2, post_g):
    row = lambda i: (i, 0)
    return pl.pallas_call(
        _ffn_kernel,
        grid=(N_ALL // TP,),
        in_specs=[
            pl.BlockSpec((TP, D_MODEL), row),
            _mod_spec(layer, TP),
            _resident((1, D_MODEL), lambda i: (0, 0)),
            _resident((D_MODEL, D_FF), lambda i: (0, 0)),
            _resident((D_MODEL, D_FF), lambda i: (0, 0)),
            _resident((D_FF, D_MODEL), lambda i: (0, 0)),
            _resident((1, D_MODEL), lambda i: (0, 0)),
        ],
        out_specs=pl.BlockSpec((TP, D_MODEL), row),
        out_shape=jax.ShapeDtypeStruct((N_ALL, D_MODEL), F32),
        compiler_params=_params(("parallel",), 56),
        name=f"ffn_l{layer}",
    )(x_all, mod, pre_g, w1, w3, w2, post_g)


def _ffn_input(x, mod_ref, g_ref):
    return (_rms(x) * g_ref[...]) * (1.0 + mod_ref[4:5, :]) + mod_ref[3:4, :]


def _router_kernel(x_ref, mod_ref, g_ref, rw_ref, f_ref, ids_ref, rank_ref, rows_ref, cnt_ref):
    f = _ffn_input(x_ref[...], mod_ref, g_ref)
    rw = rw_ref[...]
    f_hi, rw_hi = f.astype(BF16), rw.astype(BF16)
    f_ref[...] = f_hi
    f_lo, rw_lo = (f - f_hi.astype(F32)).astype(BF16), (rw - rw_hi.astype(F32)).astype(BF16)
    logits = (jnp.dot(f_hi, rw_hi, preferred_element_type=F32) + jnp.dot(f_lo, rw_hi, preferred_element_type=F32)
              + jnp.dot(f_hi, rw_lo, preferred_element_type=F32))
    row = lax.broadcasted_iota(jnp.int32, (ROUTE_ROWS, TM), 0)
    lg = jnp.where(row < N_EXPERTS, logits.T[:ROUTE_ROWS, :], -jnp.inf)
    v1 = jnp.max(lg, axis=0, keepdims=True)
    i1 = jnp.min(jnp.where(lg == v1, row, N_EXPERTS), axis=0, keepdims=True)
    rest = jnp.where(row == i1, -jnp.inf, lg)
    v2 = jnp.max(rest, axis=0, keepdims=True)
    i2 = jnp.min(jnp.where(rest == v2, row, N_EXPERTS), axis=0, keepdims=True)
    e = jnp.exp(v2 - v1)
    w1 = 1.0 / (1.0 + e)

    sel1, sel2 = row == i1, row == i2
    onehot = jnp.where(sel1 | sel2, 1.0, 0.0)
    r = lax.broadcasted_iota(jnp.int32, (TM, TM), 0)
    c = lax.broadcasted_iota(jnp.int32, (TM, TM), 1)
    earlier = jnp.where(r < c, 1.0, 0.0).astype(BF16)
    prefix = jnp.dot(onehot.astype(BF16), earlier, preferred_element_type=F32)
    rank1 = jnp.sum(jnp.where(sel1, prefix, 0.0), axis=0, keepdims=True)
    rank2 = jnp.sum(jnp.where(sel2, prefix, 0.0), axis=0, keepdims=True)
    cnt_ref[...] = jnp.sum(onehot, axis=1, keepdims=True).astype(jnp.int32)

    tok = jnp.zeros((ROUTE_ROWS, TM), F32)
    for k, val in enumerate([i1.astype(F32), i2.astype(F32), rank1, rank2, w1, e * w1]):
        tok = jnp.where(row == k, val, tok)
    rows_ref[...] = tok[:8, :]
    cols = jnp.concatenate([tok, jnp.zeros((128 - ROUTE_ROWS, TM), F32)], axis=0).T
    ids_ref[...] = cols[:, 0:2].astype(jnp.int32)
    rank_ref[...] = cols[:, 2:4].astype(jnp.int32)


def _router(layer, x_lat, mod, pre_g, router_w):
    row = lambda i: (i, 0)
    pair = lambda: pl.BlockSpec((TM, 2), row)
    return pl.pallas_call(
        _router_kernel,
        grid=(N_TOK_TILES,),
        in_specs=[
            pl.BlockSpec((TM, D_MODEL), row),
            _mod_spec(layer, TM),
            _resident((1, D_MODEL), lambda i: (0, 0)),
            _resident((D_MODEL, 128), lambda i: (0, 0)),
        ],
        out_specs=[pl.BlockSpec((TM, D_MODEL), row), pair(), pair(),
                   pl.BlockSpec((None, 8, TM), lambda i: (i, 0, 0)),
                   pl.BlockSpec((None, ROUTE_ROWS, 1), lambda i: (i, 0, 0))],
        out_shape=[jax.ShapeDtypeStruct((N_LAT, D_MODEL), BF16),
                   jax.ShapeDtypeStruct((N_LAT, 2), jnp.int32),
                   jax.ShapeDtypeStruct((N_LAT, 2), jnp.int32),
                   jax.ShapeDtypeStruct((N_TOK_TILES, 8, TM), F32),
                   jax.ShapeDtypeStruct((N_TOK_TILES, ROUTE_ROWS, 1), jnp.int32)],
        compiler_params=_params(("parallel",), 32),
        name=f"router_l{layer}",
    )(x_lat, mod, pre_g, jnp.pad(router_w, ((0, 0), (0, 128 - N_EXPERTS))))


def _routing_tables(cnt):
    cnt = cnt[:, :N_EXPERTS, 0]
    run =(cnt + ROW_ALIGN - 1) // ROW_ALIGN * ROW_ALIGN
    zoff = jnp.cumsum(run, axis=1) - run
    grp = jnp.sum(run, axis=0)
    tiles = (grp + TM - 1) // TM
    tile_end = jnp.cumsum(tiles)
    off = (tile_end - tiles) * TM
    gstart = off[None, :] + jnp.cumsum(run, axis=0) - run
    runs = jnp.concatenate([run, zoff, gstart], axis=1).astype(jnp.int32).reshape(N_TOK_TILES, 1, 3 * N_EXPERTS)

    j = jnp.arange(XS_TILES, dtype=jnp.int32)
    n_used = tile_end[-1]
    valid = (j < n_used).astype(jnp.int32)
    jc = jnp.minimum(j, n_used - 1)
    texp = jnp.minimum(jnp.sum(jc[:, None] >= tile_end[None, :], axis=1), N_EXPERTS - 1).astype(jnp.int32)
    pad_lo = (off + grp).astype(jnp.int32)
    pad_hi = (tile_end * TM).astype(jnp.int32).at[N_EXPERTS - 1].set(XS_TILES * TM)
    return runs, texp, valid, pad_lo, pad_hi


def _for_each_run_piece(runs_ref, fn):
    for e in range(N_EXPERTS):
        n = runs_ref[0, e]
        zo = runs_ref[0, N_EXPERTS + e]
        go = runs_ref[0, 2 * N_EXPERTS + e]
        for size in RUN_SIZES:
            done = n & ~(2 * size - 1)
            pl.when((n & size) != 0)(functools.partial(
                fn, pl.multiple_of(zo + done, ROW_ALIGN), pl.multiple_of(go + done, ROW_ALIGN), size))


def _block_position(ids, rank, runs_ref):
    start = jnp.zeros_like(ids)
    for e in range(N_EXPERTS):
        start = jnp.where(ids == e, runs_ref[0, N_EXPERTS + e], start)
    return start + rank


def _dispatch_kernel(pad_lo_ref, pad_hi_ref, runs_ref, runs_prev_ref, tok_ref, f_ref,
                     xs_ref, z_ref, zero_ref, sems, zsem):
    i = pl.program_id(0)
    n = pl.num_programs(0)

    @pl.when(i == 0)
    def _():
        zero_ref[...] = jnp.zeros_like(zero_ref)
        for e in range(N_EXPERTS):
            lo, hi = pad_lo_ref[e] // ROW_ALIGN, pad_hi_ref[e] // ROW_ALIGN
            blk = lambda r: pltpu.make_async_copy(
                zero_ref, xs_ref.at[pl.ds(pl.multiple_of(r * ROW_ALIGN, ROW_ALIGN), ROW_ALIGN)], zsem)
            lax.fori_loop(lo, hi, lambda r, _: blk(r).start(), None)
            lax.fori_loop(lo, hi, lambda r, _: blk(r).wait(), None)

    def piece(slot, zrow, grow, size):
        return pltpu.make_async_copy(z_ref.at[slot, pl.ds(zrow, size)], xs_ref.at[pl.ds(grow, size)], sems.at[slot])

    def step(slot):
        f = f_ref[...]
        tok = tok_ref[...]
        lp = _block_position(tok[0:2, :].astype(jnp.int32), tok[2:4, :].astype(jnp.int32), runs_ref)
        rows = lax.broadcasted_iota(jnp.int32, (Z_ROWS, TM), 0)
        hit0, hit1 = lp[0:1, :] == rows, lp[1:2, :] == rows
        sel = jnp.where(hit0 | hit1, 1.0, 0.0).astype(BF16)
        z_ref[slot, :, :D_MODEL] = jnp.dot(sel, f, preferred_element_type=F32)
        gate = jnp.sum(jnp.where(hit0, tok[4:5, :], 0.0) + jnp.where(hit1, tok[5:6, :], 0.0),
                       axis=-1, keepdims=True)
        z_ref[slot, :, D_MODEL:] = jnp.broadcast_to(gate, (Z_ROWS, D_XS - D_MODEL))
        _for_each_run_piece(runs_ref, lambda zrow, grow, size: piece(slot, zrow, grow, size).start())

        @pl.when(i >= 1)
        def _():
            _for_each_run_piece(runs_prev_ref, lambda zrow, grow, size: piece(1 - slot, zrow, grow, size).wait())

        @pl.when(i == n - 1)
        def _():
            _for_each_run_piece(runs_ref, lambda zrow, grow, size: piece(slot, zrow, grow, size).wait())

    for slot in range(2):
        pl.when(i % 2 == slot)(functools.partial(step, slot))


def _dispatch(layer, f, runs, tok_rows, pad_lo, pad_hi):
    smem = lambda: pl.BlockSpec(memory_space=pltpu.SMEM)
    runs_spec = lambda back: pl.BlockSpec((None, 1, 3 * N_EXPERTS), lambda i: (jnp.maximum(i - back, 0), 0, 0),
                                          memory_space=pltpu.SMEM)
    return pl.pallas_call(
        _dispatch_kernel,
        grid=(N_TOK_TILES,),
        in_specs=[
            smem(), smem(), runs_spec(0), runs_spec(1),
            pl.BlockSpec((None, 8, TM), lambda i: (i, 0, 0)),
            pl.BlockSpec((TM, D_MODEL), lambda i: (i, 0)),
        ],
        out_specs=pl.BlockSpec(memory_space=pl.ANY),
        out_shape=jax.ShapeDtypeStruct((XS_TILES * TM, D_XS), F32),
        scratch_shapes=[pltpu.VMEM((2, Z_ROWS, D_XS), F32), pltpu.VMEM((ROW_ALIGN, D_XS), F32),
                        pltpu.SemaphoreType.DMA((2,)), pltpu.SemaphoreType.DMA(())],
        compiler_params=_params(("arbitrary",), 48),
        name=f"moe_dispatch_l{layer}",
    )(pad_lo, pad_hi, runs, runs, tok_rows, f)


def _expert_kernel(texp_ref, tvalid_ref, xs_ref, w1_ref, w3_ref, w2_ref, ys_ref, xb_ref, acc_ref):
    j = pl.program_id(0)
    k = pl.program_id(1)

    @pl.when((tvalid_ref[j] == 0) & (k == 0))
    def _():
        ys_ref[...] = jnp.zeros_like(ys_ref)

    @pl.when(tvalid_ref[j] == 1)
    def _():
        n_k = D_EXPERT // TF_MOE
        assert n_k >= 2

        def chunk(first, last):
            if first:
                xb = xs_ref[:, :D_MODEL].astype(BF16)
                xb_ref[...] = xb
            else:
                xb = xb_ref[...]
            h1 = jnp.dot(xb, w1_ref[...], preferred_element_type=F32)
            h3 = jnp.dot(xb, w3_ref[...], preferred_element_type=F32)
            y = jnp.dot((_silu(h1) * h3).astype(BF16), w2_ref[...], preferred_element_type=F32)
            if not first:
                y = acc_ref[...] + y
            if last:
                ys_ref[...] = y * xs_ref[:, D_MODEL:D_MODEL + 1]
            else:
                acc_ref[...] = y

        pl.when(k == 0)(functools.partial(chunk, True, False))
        if n_k > 2:
            pl.when((k > 0) & (k < n_k - 1))(functools.partial(chunk, False, False))
        pl.when(k == n_k - 1)(functools.partial(chunk, False, True))


def _experts(layer, xs, texp, tvalid, w1, w3, w2):
    n_k = D_EXPERT // TF_MOE
    row = lambda j, k, texp, tvalid: (j, 0)
    kk = lambda j, k, tvalid: jnp.where(tvalid[j] == 1, k, n_k - 1)
    return pl.pallas_call(
        _expert_kernel,
        grid_spec=pltpu.PrefetchScalarGridSpec(
            num_scalar_prefetch=2,
            grid=(XS_TILES, n_k),
            in_specs=[
                pl.BlockSpec((TM, D_XS), row),
                pl.BlockSpec((None, D_MODEL, TF_MOE), lambda j, k, texp, tvalid: (texp[j], 0, kk(j, k, tvalid))),
                pl.BlockSpec((None, D_MODEL, TF_MOE), lambda j, k, texp, tvalid: (texp[j], 0, kk(j, k, tvalid))),
                pl.BlockSpec((None, TF_MOE, D_MODEL), lambda j, k, texp, tvalid: (texp[j], kk(j, k, tvalid), 0)),
            ],
            out_specs=pl.BlockSpec((TM, D_MODEL), row),
            scratch_shapes=[pltpu.VMEM((TM, D_MODEL), BF16), pltpu.VMEM((TM, D_MODEL), F32)],
        ),
        out_shape=jax.ShapeDtypeStruct((XS_TILES * TM, D_MODEL), F32),
        compiler_params=_params(("arbitrary", "arbitrary"), 56),
        name=f"moe_experts_l{layer}",
    )(texp, tvalid, xs, w1, w3, w2)


def _combine_kernel(runs_ref, runs_next_ref, ys_ref, ids_ref, rank_ref, x_ref, mod_ref, pg_ref, o_ref, buf_ref, sems):
    i = pl.program_id(0)
    n = pl.num_programs(0)

    def piece(slot, zrow, grow, size):
        return pltpu.make_async_copy(ys_ref.at[pl.ds(grow, size)], buf_ref.at[slot, pl.ds(zrow, size)], sems.at[slot])

    def step(slot):
        @pl.when(i == 0)
        def _():
            buf_ref[...] = jnp.zeros_like(buf_ref)
            _for_each_run_piece(runs_ref, lambda zrow, grow, size: piece(slot, zrow, grow, size).start())

        @pl.when(i + 1 < n)
        def _():
            _for_each_run_piece(runs_next_ref, lambda zrow, grow, size: piece(1 - slot, zrow, grow, size).start())

        _for_each_run_piece(runs_ref, lambda zrow, grow, size: piece(slot, zrow, grow, size).wait())
        lp = _block_position(ids_ref[...], rank_ref[...], runs_ref)
        cols = lax.broadcasted_iota(jnp.int32, (TM, Z_ROWS), 1)
        pick = jnp.where((lp[:, 0:1] == cols) | (lp[:, 1:2] == cols), 1.0, 0.0).astype(BF16)
        y = jnp.dot(pick, buf_ref[slot].astype(BF16), preferred_element_type=F32)
        o_ref[...] = x_ref[...] + mod_ref[5:6, :] * (_rms(y) * pg_ref[...])

    for slot in range(2):
        pl.when(i % 2 == slot)(functools.partial(step, slot))


def _combine(layer, ys, runs, ids, rank, x_lat, mod, post_g):
    row = lambda i: (i, 0)
    runs_spec = lambda fwd: pl.BlockSpec(
        (None, 1, 3 * N_EXPERTS), lambda i: (jnp.minimum(i + fwd, N_TOK_TILES - 1), 0, 0), memory_space=pltpu.SMEM)
    return pl.pallas_call(
        _combine_kernel,
        grid=(N_TOK_TILES,),
        in_specs=[
            runs_spec(0), runs_spec(1),
            pl.BlockSpec(memory_space=pl.ANY),
            pl.BlockSpec((TM, 2), row),
            pl.BlockSpec((TM, 2), row),
            pl.BlockSpec((TM, D_MODEL), row),
            _mod_spec(layer, TM),
            _resident((1, D_MODEL), lambda i: (0, 0)),
        ],
        out_specs=pl.BlockSpec((TM, D_MODEL), row),
        out_shape=jax.ShapeDtypeStruct((N_LAT, D_MODEL), F32),
        scratch_shapes=[pltpu.VMEM((2, Z_ROWS, D_MODEL), F32), pltpu.SemaphoreType.DMA((2,))],
        compiler_params=_params(("arbitrary",), 40),
        name=f"moe_combine_l{layer}",
    )(runs, runs, ys, ids, rank, x_lat, mod, post_g)


def kernel(x, c, ctx, c_ctx, ada_w, ada_b, pre_mix_g, post_mix_g, pre_ffn_g, post_ffn_g, w_in, w_out,
           lam_q1, lam_k1, lam_q2, lam_k2, subln_g, sg_ln_g, sg_ln_b, sg_w, sg_b,
           ffn_w1, ffn_w3, ffn_w2, router_w, moe_w1, moe_w3, moe_w2):
    x_lat = x.reshape(N_LAT, D_MODEL)
    x_ctx = ctx.reshape(N_CTX, D_MODEL)
    cc = jnp.concatenate([c, c_ctx[None, :], jnp.zeros((MOD_ROWS - BATCH - 1, D_MODEL), F32)], axis=0)
    mod = _modulation(cc, ada_w, ada_b)
    rope_tabs = _rope_tables()
    n_lat_tiles = N_LAT // TP
    x_all = None
    for l in range(DEPTH):
        last = l == DEPTH - 1
        vec = lambda a: a[l][None, :]
        if l == 0:
            xa, xb, xb_tile0 = x_lat, x_ctx, 0
        else:
            xa, xb, xb_tile0 = x_all, x_all, n_lat_tiles
        if l == 0:
            w_in_l, w_out_l, sg_w_l = w_in[0].astype(BF16), w_out[0].astype(BF16), sg_w[0].astype(BF16)
            late = (ffn_w1, ffn_w3, ffn_w2, moe_w1, moe_w3, moe_w2, w_in[1:], w_out[1:], sg_w[1:])
            casts = [w.reshape(-1, w.shape[-1]) for w in late]
        else:
            w_in_l, w_out_l, sg_w_l, casts = w_in_late[l - 1], w_out_late[l - 1], sg_w_late[l - 1], []
        q, kt, v, u, gn = _in_proj(l, xa, xb, xb_tile0, mod, vec(pre_mix_g), w_in_l,
                                   rope_tabs, vec(sg_ln_g), vec(sg_ln_b))
        lam_params = [vec(lam_q1), vec(lam_k1), vec(lam_q2), vec(lam_k2)]
        a_lat, cast_out = _attention(l, q, kt, v, lam_params, vec(subln_g), latent=True, casts=casts)
        if l == 0:
            late_bf16 = [w.reshape(o.shape) for w, o in zip(cast_out, late)]
            ffn_bf16, moe_bf16 = late_bf16[0:3], late_bf16[3:6]
            w_in_late, w_out_late, sg_w_late = late_bf16[6:9]
        a_ctx = a_lat if last else _attention(l, q, kt, v, lam_params, vec(subln_g), latent=False)[0]
        sg_bias = jnp.repeat(sg_b[l].T, CHUNK, axis=1)
        n_tiles = n_lat_tiles if last else N_ALL // TP
        x_mid = _out_proj(l, n_tiles, a_lat, a_ctx, u, gn, sg_w_l, sg_bias, w_out_l, vec(post_mix_g), mod,
                          xa, xb, xb_tile0)
        if l % 2 == 0:
            i = l // 2
            x_all = _ffn(l, x_mid, mod, vec(pre_ffn_g), ffn_bf16[0][i], ffn_bf16[1][i], ffn_bf16[2][i],
                         vec(post_ffn_g))
        else:
            i = l // 2
            f, ids, rank, tok_rows, cnt = _router(l, x_mid, mod, vec(pre_ffn_g), router_w[i])
            runs, texp, tvalid, pad_lo, pad_hi = _routing_tables(cnt)
            xs = _dispatch(l, f, runs, tok_rows, pad_lo, pad_hi)
            ys = _experts(l, xs, texp, tvalid, moe_bf16[0][i], moe_bf16[1][i], moe_bf16[2][i])
            x_all = _combine(l, ys, runs, ids, rank, x_mid, mod, vec(post_ffn_g))
    return x_all.reshape(BATCH, SEQ, D_MODEL)
```

```python
import functools
import math

import jax
import jax.numpy as jnp
from jax import lax
from jax.experimental import pallas as pl
from jax.experimental.pallas import tpu as pltpu

D_MODEL = 1024
BATCH = 16
SEQ = 2048
DEPTH = 2
GRID_W = 64
CTX_LEN = 256
DA_WIDTH = 512
DA_HEADS = 4
DA_VDIM = 128
DA_HEAD_DIM = 64
SG_WIDTH = 512
SG_GROUPS = 4
CHUNK = 128
ROPE_BASE = 10000.0
D_FF = 2816
N_EXPERTS = 8
D_EXPERT = 3584
EPS = 1e-6
IN_COLS = 3 * DA_WIDTH + 2 * SG_WIDTH

N_LAT = BATCH * SEQ
N_CTX = BATCH * CTX_LEN
N_ALL = N_LAT + N_CTX
MOD_ROWS = 32

TP = 1024
TP_SUB = 512
TM = 512
TQ = 2048
TQ_SUB = 512
KEY_SEG = 2048
TF_DENSE = 256
TF_MOE = 1792
N_TOK_TILES = N_LAT // TM
ROUTE_ROWS = 16
ROW_ALIGN = 8
RUN_SIZES = (512, 256, 128, 64, 32, 16, 8)
Z_ROWS = 2 * TM + N_EXPERTS * ROW_ALIGN
D_XS = D_MODEL + 128
XS_TILES = (2 * N_LAT + N_TOK_TILES * N_EXPERTS * (ROW_ALIGN - 1)) // TM + 1 + N_EXPERTS
MIB = 2 ** 20

BF16 = jnp.bfloat16
F32 = jnp.float32


def _rms(x):
    return x * lax.rsqrt(jnp.mean(x * x, axis=-1, keepdims=True) + EPS)


def _silu(x):
    return x * jax.nn.sigmoid(x)


def _resident(block_shape, index_map):
    return pl.BlockSpec(block_shape, index_map, pipeline_mode=pl.Buffered(1))


def _params(semantics, vmem_mib):
    return pltpu.CompilerParams(dimension_semantics=semantics, vmem_limit_bytes=vmem_mib * MIB)


def _mod_kernel(cc_ref, w_ref, b_ref, o_ref):
    a = _silu(cc_ref[...]).astype(BF16)
    o_ref[...] = jnp.dot(a, w_ref[...].astype(BF16), preferred_element_type=F32) + b_ref[...]


def _modulation(cc, ada_w, ada_b):
    tn = 1536
    out = pl.pallas_call(
        _mod_kernel,
        grid=(DEPTH, 6 * D_MODEL // tn),
        in_specs=[
            pl.BlockSpec((MOD_ROWS, D_MODEL), lambda l, j: (0, 0)),
            pl.BlockSpec((None, D_MODEL, tn), lambda l, j: (l, 0, j)),
            pl.BlockSpec((None, 1, tn), lambda l, j: (l, 0, j)),
        ],
        out_specs=pl.BlockSpec((None, MOD_ROWS, tn), lambda l, j: (l, 0, j)),
        out_shape=jax.ShapeDtypeStruct((DEPTH, MOD_ROWS, 6 * D_MODEL), F32),
        compiler_params=_params(("parallel", "parallel"), 40),
        name="modulation",
    )(cc, ada_w, ada_b.reshape(DEPTH, 1, 6 * D_MODEL))
    return out.reshape(DEPTH, MOD_ROWS, 6, D_MODEL)


def _mod_spec(layer, tile):
    tiles_per_batch = SEQ // tile
    return pl.BlockSpec((None, None, 6, D_MODEL),
                        lambda i: (layer, jnp.minimum(i // tiles_per_batch, BATCH), 0, 0))


def _in_proj_kernel(n_lat_tiles, xa_ref, xb_ref, mod_ref, g_ref, w_ref, cos_ref, sa_ref, sb_ref,
                    lng_ref, lnb_ref, q_ref, kt_ref, v_ref, u_ref, gn_ref):
    i = pl.program_id(0)
    def project(r0):
        rows = slice(r0, r0 + TP_SUB)
        x = jnp.where(i < n_lat_tiles, xa_ref[rows, :], xb_ref[rows, :])
        h = _rms(x) * g_ref[...]
        h = h * (1.0 + mod_ref[1:2, :]) + mod_ref[0:1, :]
        return jnp.dot(h.astype(BF16), w_ref[...], preferred_element_type=F32)

    def finish(r0, p):
        rows = slice(r0, r0 + TP_SUB)
        cos, sa, sb = cos_ref[rows, :], sa_ref[rows, :], sb_ref[rows, :]

        def rope(t):
            return t * cos + pltpu.roll(t, 112, 1) * sa + pltpu.roll(t, 16, 1) * sb

        for hd in range(DA_HEADS):
            lo, hi = hd * DA_VDIM, (hd + 1) * DA_VDIM
            q_ref[rows, lo:hi] = (rope(p[:, lo:hi]) * (DA_HEAD_DIM ** -0.5 * math.log2(math.e))).astype(BF16)
            kt_ref[lo:hi, rows] = rope(p[:, DA_WIDTH + lo:DA_WIDTH + hi]).T.astype(BF16)
        v_ref[rows, :] = p[:, 2 * DA_WIDTH:3 * DA_WIDTH].astype(BF16)
        u_ref[rows, :] = p[:, 3 * DA_WIDTH:3 * DA_WIDTH + SG_WIDTH].astype(BF16)
        gv = p[:, 3 * DA_WIDTH + SG_WIDTH:]
        mu = jnp.mean(gv, axis=-1, keepdims=True)
        var = jnp.mean(jnp.square(gv - mu), axis=-1, keepdims=True)
        gn_ref[rows, :] = ((gv - mu) * lax.rsqrt(var + EPS) * lng_ref[...] + lnb_ref[...]).astype(BF16)

    starts = list(range(0, TP, TP_SUB))
    pending = project(starts[0])
    for idx, r0 in enumerate(starts):
        nxt = project(starts[idx + 1]) if idx + 1 < len(starts) else None
        finish(r0, pending)
        pending = nxt


def _in_proj(layer, xa, xb, xb_tile0, mod, pre_g, w_in, rope_tabs, ln_g, ln_b):
    n_lat_tiles = N_LAT // TP
    n_tiles = N_ALL // TP
    tiles_per_batch = SEQ // TP
    tab_spec = pl.BlockSpec(
        (TP, DA_VDIM), lambda i: (jnp.where(i < n_lat_tiles, i % tiles_per_batch, tiles_per_batch), 0))
    row = lambda i: (i, 0)
    outs = pl.pallas_call(
        functools.partial(_in_proj_kernel, n_lat_tiles),
        grid=(n_tiles,),
        in_specs=[
            pl.BlockSpec((TP, D_MODEL), lambda i: (jnp.minimum(i, n_lat_tiles - 1), 0)),
            pl.BlockSpec((TP, D_MODEL), lambda i: (xb_tile0 + jnp.maximum(i - n_lat_tiles, 0), 0)),
            _mod_spec(layer, TP),
            _resident((1, D_MODEL), lambda i: (0, 0)),
            _resident((D_MODEL, IN_COLS), lambda i: (0, 0)),
            tab_spec, tab_spec, tab_spec,
            _resident((1, SG_WIDTH), lambda i: (0, 0)),
            _resident((1, SG_WIDTH), lambda i: (0, 0)),
        ],
        out_specs=[
            pl.BlockSpec((TP, DA_WIDTH), row),
            pl.BlockSpec((DA_WIDTH, TP), lambda i: (0, i)),
            pl.BlockSpec((TP, DA_WIDTH), row),
            pl.BlockSpec((TP, SG_WIDTH), row),
            pl.BlockSpec((TP, SG_WIDTH), row),
        ],
        out_shape=[
            jax.ShapeDtypeStruct((N_ALL, DA_WIDTH), BF16),
            jax.ShapeDtypeStruct((DA_WIDTH, N_ALL), BF16),
            jax.ShapeDtypeStruct((N_ALL, DA_WIDTH), BF16),
            jax.ShapeDtypeStruct((N_ALL, SG_WIDTH), BF16),
            jax.ShapeDtypeStruct((N_ALL, SG_WIDTH), BF16),
        ],
        compiler_params=_params(("parallel",), 48),
        name=f"in_proj_l{layer}",
    )(xa, xb, mod, pre_g, w_in, *rope_tabs, ln_g, ln_b)
    return outs


def _rope_tables():
    pos = jnp.arange(SEQ, dtype=jnp.int32)
    row = (pos // GRID_W).astype(F32)
    col = (pos % GRID_W).astype(F32)
    half = DA_HEAD_DIM // 2
    inv = ROPE_BASE ** (-jnp.arange(0, half, 2, dtype=F32) / half)
    ang_r = row[:, None] * inv
    ang_c = col[:, None] * inv
    ang = jnp.concatenate([ang_r, ang_r, ang_c, ang_c], axis=-1)
    cos = jnp.tile(jnp.cos(ang), (1, 2))
    sin = jnp.tile(jnp.sin(ang), (1, 2))
    first = (jnp.arange(DA_VDIM) % 32) < 16
    sa = jnp.where(first, -sin, 0.0)
    sb = jnp.where(first, 0.0, sin)
    ident = jnp.zeros((TP, DA_VDIM), F32)
    return (jnp.concatenate([cos, ident + 1.0], axis=0),
            jnp.concatenate([sa, ident], axis=0),
            jnp.concatenate([sb, ident], axis=0))


def _attn_kernel(n_src, n_cast, lambda_init, q_ref, *refs):
    kt_refs = refs[:n_src]
    v_refs = refs[n_src:2 * n_src]
    lq1_ref, lk1_ref, lq2_ref, lk2_ref, subg_ref = refs[2 * n_src:2 * n_src + 5]
    n_in = 2 * n_src + 5
    cast_in_refs = refs[n_in:n_in + n_cast]
    o_ref = refs[n_in + n_cast]
    cast_out_refs = refs[n_in + n_cast + 1:n_in + 2 * n_cast + 1]
    vext_refs = refs[-n_src:]

    for src_ref, dst_ref in zip(cast_in_refs, cast_out_refs):
        dst_ref[...] = src_ref[...].astype(BF16)

    n_heads = q_ref.shape[1] // DA_VDIM
    head = lambda hd: slice(hd * DA_VDIM, (hd + 1) * DA_VDIM)
    head_ext = lambda hd: slice(hd * 2 * DA_VDIM, (hd + 1) * 2 * DA_VDIM)

    @pl.when(pl.program_id(2) == 0)
    def _():
        for v_ref, ve_ref in zip(v_refs, vext_refs):
            for hd in range(n_heads):
                ve_ref[:, hd * 2 * DA_VDIM:hd * 2 * DA_VDIM + DA_VDIM] = v_ref[:, head(hd)]
                ve_ref[:, hd * 2 * DA_VDIM + DA_VDIM:(hd + 1) * 2 * DA_VDIM] = jnp.ones((v_ref.shape[0], DA_VDIM), BF16)

    segs = []
    for kt_ref, ve_ref in zip(kt_refs, vext_refs):
        n_keys = ve_ref.shape[0]
        for lo in range(0, n_keys, KEY_SEG):
            segs.append((kt_ref, ve_ref, lo, min(lo + KEY_SEG, n_keys)))

    lam = (jnp.exp(jnp.sum(lq1_ref[...] * lk1_ref[...], axis=-1, keepdims=True))
           - jnp.exp(jnp.sum(lq2_ref[...] * lk2_ref[...], axis=-1, keepdims=True)) + lambda_init)
    tq_sub = min(TQ_SUB, q_ref.shape[0])

    def scores(hd, r0):
        q = q_ref[r0:r0 + tq_sub, head(hd)]
        first = lax.broadcasted_iota(jnp.int32, q.shape, 1) < DA_HEAD_DIM
        zero = jnp.zeros_like(q)
        qms = (jnp.where(first, q, zero), jnp.where(first, zero, q))
        return [[jnp.dot(qm, kt_ref[head(hd), lo:hi], preferred_element_type=F32) for kt_ref, _, lo, hi in segs]
                for qm in qms]

    def finish(hd, r0, ss_maps):
        maps = []
        for ss in ss_maps:
            m = functools.reduce(jnp.maximum, [jnp.max(s, axis=-1, keepdims=True) for s in ss])
            ts = [jnp.dot(jnp.exp2(s - m).astype(BF16), ve_ref[lo:hi, head_ext(hd)], preferred_element_type=F32)
                  for s, (_, ve_ref, lo, hi) in zip(ss, segs)]
            acc = functools.reduce(jnp.add, ts)
            maps.append(acc[:, :DA_VDIM] / acc[:, DA_VDIM:])
        o = maps[0] - lam * maps[1]
        o_ref[r0:r0 + tq_sub, head(hd)] = (_rms(o) * subg_ref[...] * (1.0 - lambda_init)).astype(BF16)

    items = [(hd, r0) for hd in range(n_heads) for r0 in range(0, q_ref.shape[0], tq_sub)]
    pending = scores(*items[0])
    for idx, item in enumerate(items):
        nxt = scores(*items[idx + 1]) if idx + 1 < len(items) else None
        finish(*item, pending)
        pending = nxt


def _attention(layer, q, kt, v, lam_params, subg, latent, casts=()):
    lambda_init = 0.8 - 0.6 * math.exp(-0.3 * layer)
    ctx_blk0 = N_LAT // CTX_LEN
    small = lambda: pl.BlockSpec((1, DA_HEAD_DIM), lambda b, h, t: (0, 0))
    tail_specs = [small(), small(), small(), small(), pl.BlockSpec((1, DA_VDIM), lambda b, h, t: (0, 0))]
    hw = DA_VDIM if latent else DA_WIDTH
    n_hblk = DA_WIDTH // hw
    kt_ctx = pl.BlockSpec((hw, CTX_LEN), lambda b, h, t: (h, ctx_blk0 + b))
    v_ctx = pl.BlockSpec((CTX_LEN, hw), lambda b, h, t: (ctx_blk0 + b, h))
    if latent:
        tq, n_q, n_rows = TQ, SEQ // TQ, N_LAT
        q_map = lambda b, h, t: (b * n_q + t, h)
        o_map = q_map
        kt_specs = [kt_ctx, pl.BlockSpec((hw, SEQ), lambda b, h, t: (h, b))]
        v_specs = [v_ctx, pl.BlockSpec((SEQ, hw), lambda b, h, t: (b, h))]
        key_lens = [CTX_LEN, SEQ]
    else:
        tq, n_q, n_rows = CTX_LEN, 1, N_CTX
        q_map = lambda b, h, t: (ctx_blk0 + b, h)
        o_map = lambda b, h, t: (b, h)
        kt_specs, v_specs, key_lens = [kt_ctx], [v_ctx], [CTX_LEN]
    n_src = len(key_lens)
    n_steps = BATCH * n_hblk * n_q
    cast_specs = []
    for w in casts:
        rows = next(r for r in range(16, w.shape[0] + 1, 16) if w.shape[0] % r == 0 and w.shape[0] // r <= n_steps)
        last = w.shape[0] // rows - 1
        cast_specs.append(pl.BlockSpec(
            (rows, w.shape[1]),
            lambda b, h, t, last=last: (jnp.minimum((b * n_hblk + h) * n_q + t, last), 0)))
    outs = pl.pallas_call(
        functools.partial(_attn_kernel, n_src, len(casts), lambda_init),
        grid=(BATCH, n_hblk, n_q),
        in_specs=[pl.BlockSpec((tq, hw), q_map)] + kt_specs + v_specs + tail_specs + cast_specs,
        out_specs=[pl.BlockSpec((tq, hw), o_map)] + cast_specs,
        out_shape=[jax.ShapeDtypeStruct((n_rows, DA_WIDTH), BF16)]
                  + [jax.ShapeDtypeStruct(w.shape, BF16) for w in casts],
        scratch_shapes=[pltpu.VMEM((n, 2 * hw), BF16) for n in key_lens],
        compiler_params=_params(("parallel", "parallel", "arbitrary"), 56),
        name=f"attn_l{layer}_{'lat' if latent else 'ctx'}",
    )(q, *([kt] * n_src), *([v] * n_src), *lam_params, subg, *casts)
    return outs[0], outs[1:]


def _out_proj_kernel(n_lat_tiles, aa_ref, ab_ref, u_ref, gn_ref, sgw_ref, sgb_ref, w_ref, pg_ref, mod_ref,
                     xa_ref, xb_ref, o_ref, s_ref):
    i = pl.program_id(0)
    s_ref[:, :DA_WIDTH] = jnp.where(i < n_lat_tiles, aa_ref[...], ab_ref[...])
    for c in range(TP // CHUNK):
        rows = slice(c * CHUNK, (c + 1) * CHUNK)
        for g in range(SG_GROUPS):
            cols = slice(g * CHUNK, (g + 1) * CHUNK)
            out_cols = slice(DA_WIDTH + g * CHUNK, DA_WIDTH + (g + 1) * CHUNK)
            mixed = jnp.dot(sgw_ref[g], gn_ref[rows, cols], preferred_element_type=F32) + sgb_ref[:, cols]
            s_ref[rows, out_cols] = (u_ref[rows, cols].astype(F32) * mixed).astype(BF16)
    o = jnp.dot(s_ref[...], w_ref[...], preferred_element_type=F32)
    x = jnp.where(i < n_lat_tiles, xa_ref[...], xb_ref[...])
    o_ref[...] = x + mod_ref[2:3, :] * (_rms(o) * pg_ref[...])


def _out_proj(layer, n_tiles, a_lat, a_ctx, u, gn, sg_w, sg_bias, w_out, post_g, mod, xa, xb, xb_tile0):
    n_lat_tiles = N_LAT // TP
    row = lambda i: (i, 0)
    half = lambda: pl.BlockSpec((TP, DA_WIDTH), row)
    return pl.pallas_call(
        functools.partial(_out_proj_kernel, n_lat_tiles),
        grid=(n_tiles,),
        in_specs=[
            pl.BlockSpec((TP, DA_WIDTH), lambda i: (jnp.minimum(i, n_lat_tiles - 1), 0)),
            pl.BlockSpec((TP, DA_WIDTH), lambda i: (jnp.maximum(i - n_lat_tiles, 0), 0)),
            half(), half(),
            _resident((SG_GROUPS, CHUNK, CHUNK), lambda i: (0, 0, 0)),
            _resident((CHUNK, SG_WIDTH), lambda i: (0, 0)),
            _resident((D_MODEL, D_MODEL), lambda i: (0, 0)),
            _resident((1, D_MODEL), lambda i: (0, 0)),
            _mod_spec(layer, TP),
            pl.BlockSpec((TP, D_MODEL), lambda i: (jnp.minimum(i, n_lat_tiles - 1), 0)),
            pl.BlockSpec((TP, D_MODEL), lambda i: (xb_tile0 + jnp.maximum(i - n_lat_tiles, 0), 0)),
        ],
        out_specs=pl.BlockSpec((TP, D_MODEL), row),
        out_shape=jax.ShapeDtypeStruct((n_tiles * TP, D_MODEL), F32),
        scratch_shapes=[pltpu.VMEM((TP, DA_WIDTH + SG_WIDTH), BF16)],
        compiler_params=_params(("parallel",), 48),
        name=f"out_proj_l{layer}",
    )(a_lat, a_ctx, u, gn, sg_w, sg_bias, w_out, post_g, mod, xa, xb)


def _ffn_kernel(x_ref, mod_ref, g_ref, w1_ref, w3_ref, w2_ref, pg_ref, o_ref):
    x = x_ref[...]
    f = ((_rms(x) * g_ref[...]) * (1.0 + mod_ref[4:5, :]) + mod_ref[3:4, :]).astype(BF16)
    y = None
    for k in range(D_FF // TF_DENSE):
        cols = slice(k * TF_DENSE, (k + 1) * TF_DENSE)
        h1 = jnp.dot(f, w1_ref[:, cols], preferred_element_type=F32)
        h3 = jnp.dot(f, w3_ref[:, cols], preferred_element_type=F32)
        t = jnp.dot((_silu(h1) * h3).astype(BF16), w2_ref[cols, :], preferred_element_type=F32)
        y = t if y is None else y + t
    o_ref[...] = x + mod_ref[5:6, :] * (_rms(y) * pg_ref[...])


def _ffn(layer, x_all, mod, pre_g, w1, w3, w2, post_g):
    row = lambda i: (i, 0)
    return pl.pallas_call(
        _ffn_kernel,
        grid=(N_ALL // TP,),
        in_specs=[
            pl.BlockSpec((TP, D_MODEL), row),
            _mod_spec(layer, TP),
            _resident((1, D_MODEL), lambda i: (0, 0)),
            _resident((D_MODEL, D_FF), lambda i: (0, 0)),
            _resident((D_MODEL, D_FF), lambda i: (0, 0)),
            _resident((D_FF, D_MODEL), lambda i: (0, 0)),
            _resident((1, D_MODEL), lambda i: (0, 0)),
        ],
        out_specs=pl.BlockSpec((TP, D_MODEL), row),
        out_shape=jax.ShapeDtypeStruct((N_ALL, D_MODEL), F32),
        compiler_params=_params(("parallel",), 56),
        name=f"ffn_l{layer}",
    )(x_all, mod, pre_g, w1, w3, w2, post_g)


def _ffn_input(x, mod_ref, g_ref):
    return (_rms(x) * g_ref[...]) * (1.0 + mod_ref[4:5, :]) + mod_ref[3:4, :]


def _router_kernel(x_ref, mod_ref, g_ref, rw_ref, f_ref, ids_ref, rank_ref, rows_ref, cnt_ref):
    f = _ffn_input(x_ref[...], mod_ref, g_ref)
    rw = rw_ref[...]
    f_hi, rw_hi = f.astype(BF16), rw.astype(BF16)
    f_ref[...] = f_hi
    f_lo, rw_lo = (f - f_hi.astype(F32)).astype(BF16), (rw - rw_hi.astype(F32)).astype(BF16)
    logits = (jnp.dot(f_hi, rw_hi, preferred_element_type=F32) + jnp.dot(f_lo, rw_hi, preferred_element_type=F32)
              + jnp.dot(f_hi, rw_lo, preferred_element_type=F32))
    row = lax.broadcasted_iota(jnp.int32, (ROUTE_ROWS, TM), 0)
    lg = jnp.where(row < N_EXPERTS, logits.T[:ROUTE_ROWS, :], -jnp.inf)
    v1 = jnp.max(lg, axis=0, keepdims=True)
    i1 = jnp.min(jnp.where(lg == v1, row, N_EXPERTS), axis=0, keepdims=True)
    rest = jnp.where(row == i1, -jnp.inf, lg)
    v2 = jnp.max(rest, axis=0, keepdims=True)
    i2 = jnp.min(jnp.where(rest == v2, row, N_EXPERTS), axis=0, keepdims=True)
    e = jnp.exp(v2 - v1)
    w1 = 1.0 / (1.0 + e)

    sel1, sel2 = row == i1, row == i2
    onehot = jnp.where(sel1 | sel2, 1.0, 0.0)
    r = lax.broadcasted_iota(jnp.int32, (TM, TM), 0)
    c = lax.broadcasted_iota(jnp.int32, (TM, TM), 1)
    earlier = jnp.where(r < c, 1.0, 0.0).astype(BF16)
    prefix = jnp.dot(onehot.astype(BF16), earlier, preferred_element_type=F32)
    rank1 = jnp.sum(jnp.where(sel1, prefix, 0.0), axis=0, keepdims=True)
    rank2 = jnp.sum(jnp.where(sel2, prefix, 0.0), axis=0, keepdims=True)
    cnt_ref[...] = jnp.sum(onehot, axis=1, keepdims=True).astype(jnp.int32)

    tok = jnp.zeros((ROUTE_ROWS, TM), F32)
    for k, val in enumerate([i1.astype(F32), i2.astype(F32), rank1, rank2, w1, e * w1]):
        tok = jnp.where(row == k, val, tok)
    rows_ref[...] = tok[:8, :]
    cols = jnp.concatenate([tok, jnp.zeros((128 - ROUTE_ROWS, TM), F32)], axis=0).T
    ids_ref[...] = cols[:, 0:2].astype(jnp.int32)
    rank_ref[...] = cols[:, 2:4].astype(jnp.int32)


def _router(layer, x_lat, mod, pre_g, router_w):
    row = lambda i: (i, 0)
    pair = lambda: pl.BlockSpec((TM, 2), row)
    return pl.pallas_call(
        _router_kernel,
        grid=(N_TOK_TILES,),
        in_specs=[
            pl.BlockSpec((TM, D_MODEL), row),
            _mod_spec(layer, TM),
            _resident((1, D_MODEL), lambda i: (0, 0)),
            _resident((D_MODEL, 128), lambda i: (0, 0)),
        ],
        out_specs=[pl.BlockSpec((TM, D_MODEL), row), pair(), pair(),
                   pl.BlockSpec((None, 8, TM), lambda i: (i, 0, 0)),
                   pl.BlockSpec((None, ROUTE_ROWS, 1), lambda i: (i, 0, 0))],
        out_shape=[jax.ShapeDtypeStruct((N_LAT, D_MODEL), BF16),
                   jax.ShapeDtypeStruct((N_LAT, 2), jnp.int32),
                   jax.ShapeDtypeStruct((N_LAT, 2), jnp.int32),
                   jax.ShapeDtypeStruct((N_TOK_TILES, 8, TM), F32),
                   jax.ShapeDtypeStruct((N_TOK_TILES, ROUTE_ROWS, 1), jnp.int32)],
        compiler_params=_params(("parallel",), 32),
        name=f"router_l{layer}",
    )(x_lat, mod, pre_g, jnp.pad(router_w, ((0, 0), (0, 128 - N_EXPERTS))))


def _routing_tables(cnt):
    cnt = cnt[:, :N_EXPERTS, 0]
    run =(cnt + ROW_ALIGN - 1) // ROW_ALIGN * ROW_ALIGN
    zoff = jnp.cumsum(run, axis=1) - run
    grp = jnp.sum(run, axis=0)
    tiles = (grp + TM - 1) // TM
    tile_end = jnp.cumsum(tiles)
    off = (tile_end - tiles) * TM
    gstart = off[None, :] + jnp.cumsum(run, axis=0) - run
    runs = jnp.concatenate([run, zoff, gstart], axis=1).astype(jnp.int32).reshape(N_TOK_TILES, 1, 3 * N_EXPERTS)

    j = jnp.arange(XS_TILES, dtype=jnp.int32)
    n_used = tile_end[-1]
    valid = (j < n_used).astype(jnp.int32)
    jc = jnp.minimum(j, n_used - 1)
    texp = jnp.minimum(jnp.sum(jc[:, None] >= tile_end[None, :], axis=1), N_EXPERTS - 1).astype(jnp.int32)
    pad_lo = (off + grp).astype(jnp.int32)
    pad_hi = (tile_end * TM).astype(jnp.int32).at[N_EXPERTS - 1].set(XS_TILES * TM)
    return runs, texp, valid, pad_lo, pad_hi


def _for_each_run_piece(runs_ref, fn):
    for e in range(N_EXPERTS):
        n = runs_ref[0, e]
        zo = runs_ref[0, N_EXPERTS + e]
        go = runs_ref[0, 2 * N_EXPERTS + e]
        for size in RUN_SIZES:
            done = n & ~(2 * size - 1)
            pl.when((n & size) != 0)(functools.partial(
                fn, pl.multiple_of(zo + done, ROW_ALIGN), pl.multiple_of(go + done, ROW_ALIGN), size))


def _block_position(ids, rank, runs_ref):
    start = jnp.zeros_like(ids)
    for e in range(N_EXPERTS):
        start = jnp.where(ids == e, runs_ref[0, N_EXPERTS + e], start)
    return start + rank


def _dispatch_kernel(pad_lo_ref, pad_hi_ref, runs_ref, runs_prev_ref, tok_ref, f_ref,
                     xs_ref, z_ref, zero_ref, sems, zsem):
    i = pl.program_id(0)
    n = pl.num_programs(0)

    @pl.when(i == 0)
    def _():
        zero_ref[...] = jnp.zeros_like(zero_ref)
        for e in range(N_EXPERTS):
            lo, hi = pad_lo_ref[e] // ROW_ALIGN, pad_hi_ref[e] // ROW_ALIGN
            blk = lambda r: pltpu.make_async_copy(
                zero_ref, xs_ref.at[pl.ds(pl.multiple_of(r * ROW_ALIGN, ROW_ALIGN), ROW_ALIGN)], zsem)
            lax.fori_loop(lo, hi, lambda r, _: blk(r).start(), None)
            lax.fori_loop(lo, hi, lambda r, _: blk(r).wait(), None)

    def piece(slot, zrow, grow, size):
        return pltpu.make_async_copy(z_ref.at[slot, pl.ds(zrow, size)], xs_ref.at[pl.ds(grow, size)], sems.at[slot])

    def step(slot):
        f = f_ref[...]
        tok = tok_ref[...]
        lp = _block_position(tok[0:2, :].astype(jnp.int32), tok[2:4, :].astype(jnp.int32), runs_ref)
        rows = lax.broadcasted_iota(jnp.int32, (Z_ROWS, TM), 0)
        hit0, hit1 = lp[0:1, :] == rows, lp[1:2, :] == rows
        sel = jnp.where(hit0 | hit1, 1.0, 0.0).astype(BF16)
        z_ref[slot, :, :D_MODEL] = jnp.dot(sel, f, preferred_element_type=F32)
        gate = jnp.sum(jnp.where(hit0, tok[4:5, :], 0.0) + jnp.where(hit1, tok[5:6, :], 0.0),
                       axis=-1, keepdims=True)
        z_ref[slot, :, D_MODEL:] = jnp.broadcast_to(gate, (Z_ROWS, D_XS - D_MODEL))
        _for_each_run_piece(runs_ref, lambda zrow, grow, size: piece(slot, zrow, grow, size).start())

        @pl.when(i >= 1)
        def _():
            _for_each_run_piece(runs_prev_ref, lambda zrow, grow, size: piece(1 - slot, zrow, grow, size).wait())

        @pl.when(i == n - 1)
        def _():
            _for_each_run_piece(runs_ref, lambda zrow, grow, size: piece(slot, zrow, grow, size).wait())

    for slot in range(2):
        pl.when(i % 2 == slot)(functools.partial(step, slot))


def _dispatch(layer, f, runs, tok_rows, pad_lo, pad_hi):
    smem = lambda: pl.BlockSpec(memory_space=pltpu.SMEM)
    runs_spec = lambda back: pl.BlockSpec((None, 1, 3 * N_EXPERTS), lambda i: (jnp.maximum(i - back, 0), 0, 0),
                                          memory_space=pltpu.SMEM)
    return pl.pallas_call(
        _dispatch_kernel,
        grid=(N_TOK_TILES,),
        in_specs=[
            smem(), smem(), runs_spec(0), runs_spec(1),
            pl.BlockSpec((None, 8, TM), lambda i: (i, 0, 0)),
            pl.BlockSpec((TM, D_MODEL), lambda i: (i, 0)),
        ],
        out_specs=pl.BlockSpec(memory_space=pl.ANY),
        out_shape=jax.ShapeDtypeStruct((XS_TILES * TM, D_XS), F32),
        scratch_shapes=[pltpu.VMEM((2, Z_ROWS, D_XS), F32), pltpu.VMEM((ROW_ALIGN, D_XS), F32),
                        pltpu.SemaphoreType.DMA((2,)), pltpu.SemaphoreType.DMA(())],
        compiler_params=_params(("arbitrary",), 48),
        name=f"moe_dispatch_l{layer}",
    )(pad_lo, pad_hi, runs, runs, tok_rows, f)


def _expert_kernel(texp_ref, tvalid_ref, xs_ref, w1_ref, w3_ref, w2_ref, ys_ref, xb_ref, acc_ref):
    j = pl.program_id(0)
    k = pl.program_id(1)

    @pl.when((tvalid_ref[j] == 0) & (k == 0))
    def _():
        ys_ref[...] = jnp.zeros_like(ys_ref)

    @pl.when(tvalid_ref[j] == 1)
    def _():
        n_k = D_EXPERT // TF_MOE
        assert n_k >= 2

        def chunk(first, last):
            if first:
                xb = xs_ref[:, :D_MODEL].astype(BF16)
                xb_ref[...] = xb
            else:
                xb = xb_ref[...]
            h1 = jnp.dot(xb, w1_ref[...], preferred_element_type=F32)
            h3 = jnp.dot(xb, w3_ref[...], preferred_element_type=F32)
            y = jnp.dot((_silu(h1) * h3).astype(BF16), w2_ref[...], preferred_element_type=F32)
            if not first:
                y = acc_ref[...] + y
            if last:
                ys_ref[...] = y * xs_ref[:, D_MODEL:D_MODEL + 1]
            else:
                acc_ref[...] = y

        pl.when(k == 0)(functools.partial(chunk, True, False))
        if n_k > 2:
            pl.when((k > 0) & (k < n_k - 1))(functools.partial(chunk, False, False))
        pl.when(k == n_k - 1)(functools.partial(chunk, False, True))


def _experts(layer, xs, texp, tvalid, w1, w3, w2):
    n_k = D_EXPERT // TF_MOE
    row = lambda j, k, texp, tvalid: (j, 0)
    kk = lambda j, k, tvalid: jnp.where(tvalid[j] == 1, k, n_k - 1)
    return pl.pallas_call(
        _expert_kernel,
        grid_spec=pltpu.PrefetchScalarGridSpec(
            num_scalar_prefetch=2,
            grid=(XS_TILES, n_k),
            in_specs=[
                pl.BlockSpec((TM, D_XS), row),
                pl.BlockSpec((None, D_MODEL, TF_MOE), lambda j, k, texp, tvalid: (texp[j], 0, kk(j, k, tvalid))),
                pl.BlockSpec((None, D_MODEL, TF_MOE), lambda j, k, texp, tvalid: (texp[j], 0, kk(j, k, tvalid))),
                pl.BlockSpec((None, TF_MOE, D_MODEL), lambda j, k, texp, tvalid: (texp[j], kk(j, k, tvalid), 0)),
            ],
            out_specs=pl.BlockSpec((TM, D_MODEL), row),
            scratch_shapes=[pltpu.VMEM((TM, D_MODEL), BF16), pltpu.VMEM((TM, D_MODEL), F32)],
        ),
        out_shape=jax.ShapeDtypeStruct((XS_TILES * TM, D_MODEL), F32),
        compiler_params=_params(("arbitrary", "arbitrary"), 56),
        name=f"moe_experts_l{layer}",
    )(texp, tvalid, xs, w1, w3, w2)


def _combine_kernel(runs_ref, runs_next_ref, ys_ref, ids_ref, rank_ref, x_ref, mod_ref, pg_ref, o_ref, buf_ref, sems):
    i = pl.program_id(0)
    n = pl.num_programs(0)

    def piece(slot, zrow, grow, size):
        return pltpu.make_async_copy(ys_ref.at[pl.ds(grow, size)], buf_ref.at[slot, pl.ds(zrow, size)], sems.at[slot])

    def step(slot):
        @pl.when(i == 0)
        def _():
            buf_ref[...] = jnp.zeros_like(buf_ref)
            _for_each_run_piece(runs_ref, lambda zrow, grow, size: piece(slot, zrow, grow, size).start())

        @pl.when(i + 1 < n)
        def _():
            _for_each_run_piece(runs_next_ref, lambda zrow, grow, size: piece(1 - slot, zrow, grow, size).start())

        _for_each_run_piece(runs_ref, lambda zrow, grow, size: piece(slot, zrow, grow, size).wait())
        lp = _block_position(ids_ref[...], rank_ref[...], runs_ref)
        cols = lax.broadcasted_iota(jnp.int32, (TM, Z_ROWS), 1)
        pick = jnp.where((lp[:, 0:1] == cols) | (lp[:, 1:2] == cols), 1.0, 0.0).astype(BF16)
        y = jnp.dot(pick, buf_ref[slot].astype(BF16), preferred_element_type=F32)
        o_ref[...] = x_ref[...] + mod_ref[5:6, :] * (_rms(y) * pg_ref[...])

    for slot in range(2):
        pl.when(i % 2 == slot)(functools.partial(step, slot))


def _combine(layer, ys, runs, ids, rank, x_lat, mod, post_g):
    row = lambda i: (i, 0)
    runs_spec = lambda fwd: pl.BlockSpec(
        (None, 1, 3 * N_EXPERTS), lambda i: (jnp.minimum(i + fwd, N_TOK_TILES - 1), 0, 0), memory_space=pltpu.SMEM)
    return pl.pallas_call(
        _combine_kernel,
        grid=(N_TOK_TILES,),
        in_specs=[
            runs_spec(0), runs_spec(1),
            pl.BlockSpec(memory_space=pl.ANY),
            pl.BlockSpec((TM, 2), row),
            pl.BlockSpec((TM, 2), row),
            pl.BlockSpec((TM, D_MODEL), row),
            _mod_spec(layer, TM),
            _resident((1, D_MODEL), lambda i: (0, 0)),
        ],
        out_specs=pl.BlockSpec((TM, D_MODEL), row),
        out_shape=jax.ShapeDtypeStruct((N_LAT, D_MODEL), F32),
        scratch_shapes=[pltpu.VMEM((2, Z_ROWS, D_MODEL), F32), pltpu.SemaphoreType.DMA((2,))],
        compiler_params=_params(("arbitrary",), 40),
        name=f"moe_combine_l{layer}",
    )(runs, runs, ys, ids, rank, x_lat, mod, post_g)


def kernel(x, c, ctx, c_ctx, ada_w, ada_b, pre_mix_g, post_mix_g, pre_ffn_g, post_ffn_g, w_in, w_out,
           lam_q1, lam_k1, lam_q2, lam_k2, subln_g, sg_ln_g, sg_ln_b, sg_w, sg_b,
           ffn_w1, ffn_w3, ffn_w2, router_w, moe_w1, moe_w3, moe_w2):
    x_lat = x.reshape(N_LAT, D_MODEL)
    x_ctx = ctx.reshape(N_CTX, D_MODEL)
    cc = jnp.concatenate([c, c_ctx[None, :], jnp.zeros((MOD_ROWS - BATCH - 1, D_MODEL), F32)], axis=0)
    mod = _modulation(cc, ada_w, ada_b)
    rope_tabs = _rope_tables()
    n_lat_tiles = N_LAT // TP
    x_all = None
    for l in range(DEPTH):
        last = l == DEPTH - 1
        vec = lambda a: a[l][None, :]
        if l == 0:
            xa, xb, xb_tile0 = x_lat, x_ctx, 0
        else:
            xa, xb, xb_tile0 = x_all, x_all, n_lat_tiles
        if l == 0:
            w_in_l, w_out_l, sg_w_l = w_in[0].astype(BF16), w_out[0].astype(BF16), sg_w[0].astype(BF16)
            late = (ffn_w1, ffn_w3, ffn_w2, moe_w1, moe_w3, moe_w2, w_in[1:], w_out[1:], sg_w[1:])
            casts = [w.reshape(-1, w.shape[-1]) for w in late]
        else:
            w_in_l, w_out_l, sg_w_l, casts = w_in_late[l - 1], w_out_late[l - 1], sg_w_late[l - 1], []
        q, kt, v, u, gn = _in_proj(l, xa, xb, xb_tile0, mod, vec(pre_mix_g), w_in_l,
                                   rope_tabs, vec(sg_ln_g), vec(sg_ln_b))
        lam_params = [vec(lam_q1), vec(lam_k1), vec(lam_q2), vec(lam_k2)]
        a_lat, cast_out = _attention(l, q, kt, v, lam_params, vec(subln_g), latent=True, casts=casts)
        if l == 0:
            late_bf16 = [w.reshape(o.shape) for w, o in zip(cast_out, late)]
            ffn_bf16, moe_bf16 = late_bf16[0:3], late_bf16[3:6]
            w_in_late, w_out_late, sg_w_late = late_bf16[6:9]
        a_ctx = a_lat if last else _attention(l, q, kt, v, lam_params, vec(subln_g), latent=False)[0]
        sg_bias = jnp.repeat(sg_b[l].T, CHUNK, axis=1)
        n_tiles = n_lat_tiles if last else N_ALL // TP
        x_mid = _out_proj(l, n_tiles, a_lat, a_ctx, u, gn, sg_w_l, sg_bias, w_out_l, vec(post_mix_g), mod,
                          xa, xb, xb_tile0)
        if l % 2 == 0:
            i = l // 2
            x_all = _ffn(l, x_mid, mod, vec(pre_ffn_g), ffn_bf16[0][i], ffn_bf16[1][i], ffn_bf16[2][i],
                         vec(post_ffn_g))
        else:
            i = l // 2
            f, ids, rank, tok_rows, cnt = _router(l, x_mid, mod, vec(pre_ffn_g), router_w[i])
            runs, texp, tvalid, pad_lo, pad_hi = _routing_tables(cnt)
            xs = _dispatch(l, f, runs, tok_rows, pad_lo, pad_hi)
            ys = _experts(l, xs, texp, tvalid, moe_bf16[0][i], moe_bf16[1][i], moe_bf16[2][i])
            x_all = _combine(l, ys, runs, ids, rank, x_mid, mod, vec(post_ffn_g))
    return x_all.reshape(BATCH, SEQ, D_MODEL)
```
